```python
import math
import jax, jax.numpy as jnp
from jax import lax
import numpy as np

D_MODEL = 2048
BATCH = 1
SEQ = 16384
DEPTH = 1
DEC_BATCH = 1
DEC_SEQ = 8192
PAST_LEN = 128

HEAD_DIM = 128
GDN_HEADS = 8
GDN_W = GDN_HEADS * HEAD_DIM
ATT_Q_HEADS = 8
ATT_KV_HEADS = 2
ATT_GROUP = ATT_Q_HEADS // ATT_KV_HEADS
ATT_Q_W = ATT_Q_HEADS * HEAD_DIM
ATT_KV_W = ATT_KV_HEADS * HEAD_DIM
MIX_W = GDN_W + ATT_Q_W
IN_DIM = 3 * GDN_W + GDN_W + 4 * GDN_HEADS + ATT_Q_W + 2 * ATT_KV_W
D_FF = 5632
CONV_K = 5
CHUNK = 64
Q_BLOCK = 128
GRID_W = 64
AXIS_DIM = HEAD_DIM // 2
ROPE_THETA = 10000.0
EPS = 1e-6

kernel_name = "hybrid_gdn_gqa_axial_macaron_encoder"


def rmsnorm(x, g):
    xf = x.astype(jnp.float32)
    y = xf * lax.rsqrt(jnp.mean(xf * xf, axis=-1, keepdims=True) + EPS)
    return (y * g.astype(jnp.float32)).astype(x.dtype)


def l2norm(x):
    return x * lax.rsqrt(jnp.sum(x * x, axis=-1, keepdims=True) + EPS)


def swiglu(x, w_gate, w_up, w_down):
    return (jax.nn.silu(x @ w_gate) * (x @ w_up)) @ w_down


def centred_conv(x, w):
    pad = CONV_K // 2
    L = x.shape[1]
    xp = jnp.pad(x, ((0, 0), (pad, pad), (0, 0)))
    return sum(xp[:, i:i + L] * w[i] for i in range(CONV_K))


def gated_delta_chunked(q, k, v, g, beta):
    Bd, H, L, dk = q.shape
    dv = v.shape[-1]
    N = L // CHUNK
    q = q.reshape(Bd, H, N, CHUNK, dk)
    k = k.reshape(Bd, H, N, CHUNK, dk)
    v = v.reshape(Bd, H, N, CHUNK, dv)
    g = g.reshape(Bd, H, N, CHUNK)
    beta = beta.reshape(Bd, H, N, CHUNK)
    gc = jnp.cumsum(g, axis=-1)
    tri_incl = jnp.tril(jnp.ones((CHUNK, CHUNK), dtype=bool))
    tri_strict = jnp.tril(jnp.ones((CHUNK, CHUNK), dtype=bool), k=-1)
    diff = gc[..., :, None] - gc[..., None, :]
    decay = jnp.where(tri_incl, jnp.exp(jnp.where(tri_incl, diff, 0.0)), 0.0)
    kb = k * beta[..., None]
    vb = v * beta[..., None]
    a_strict = jnp.where(tri_strict, jnp.einsum('bhncd,bhnsd->bhncs', kb, k) * decay, 0.0)
    rhs = jnp.concatenate([vb, kb * jnp.exp(gc)[..., None]], axis=-1)
    sol = lax.linalg.triangular_solve(a_strict, rhs, left_side=True, lower=True, unit_diagonal=True)
    u = sol[..., :dv]
    w = sol[..., dv:]
    attn_qk = jnp.einsum('bhncd,bhnsd->bhncs', q, k) * decay
    q_dec = q * jnp.exp(gc)[..., None]
    k_dec = k * jnp.exp(gc[..., -1:] - gc)[..., None]
    chunk_decay = jnp.exp(gc[..., -1])

    def step(S, inp):
        u_c, w_c, qd_c, kd_c, aqk_c, cd_c = inp
        v_new = u_c - jnp.einsum('bhcd,bhde->bhce', w_c, S)
        o_c = jnp.einsum('bhcd,bhde->bhce', qd_c, S) + jnp.einsum('bhcs,bhse->bhce', aqk_c, v_new)
        S = S * cd_c[..., None, None] + jnp.einsum('bhcd,bhce->bhde', kd_c, v_new)
        return S, o_c

    xs = tuple(jnp.moveaxis(t, 2, 0) for t in (u, w, q_dec, k_dec, attn_qk, chunk_decay))
    S0 = jnp.zeros((Bd, H, dk, dv), jnp.float32)
    _, o = lax.scan(step, S0, xs)
    return jnp.moveaxis(o, 0, 2).reshape(Bd, H, L, dv)


def axial_rope_tables(rows):
    row = jnp.repeat(jnp.arange(rows, dtype=jnp.float32), GRID_W)
    col = jnp.tile(jnp.arange(GRID_W, dtype=jnp.float32), rows)
    freqs = ROPE_THETA ** (-jnp.arange(0, AXIS_DIM, 2, dtype=jnp.float32) / AXIS_DIM)
    ang_r = row[:, None] * freqs[None, :]
    ang_c = col[:, None] * freqs[None, :]
    return jnp.cos(ang_r), jnp.sin(ang_r), jnp.cos(ang_c), jnp.sin(ang_c)


def rotate(x, cos, sin):
    half = x.shape[-1] // 2
    x1, x2 = x[..., :half], x[..., half:]
    c = cos[None, :, None, :].astype(x.dtype)
    s = sin[None, :, None, :].astype(x.dtype)
    return jnp.concatenate([x1 * c - x2 * s, x2 * c + x1 * s], axis=-1)


def apply_axial_rope(x, tables):
    cr, sr, cc, sc = tables
    return jnp.concatenate([rotate(x[..., :AXIS_DIM], cr, sr), rotate(x[..., AXIS_DIM:], cc, sc)], axis=-1)


def block_attention(q, k, v):
    B, L, _, dh = q.shape
    nb = L // Q_BLOCK
    qb = q.reshape(B, nb, Q_BLOCK, ATT_KV_HEADS, ATT_GROUP, dh).transpose(1, 0, 3, 4, 2, 5)
    kt = k.transpose(0, 2, 1, 3)
    vt = v.transpose(0, 2, 1, 3)
    scale = HEAD_DIM ** -0.5

    def one_block(qi):
        s = jnp.einsum('bhgqd,bhkd->bhgqk', qi, kt).astype(jnp.float32) * scale
        p = jax.nn.softmax(s, axis=-1).astype(vt.dtype)
        return jnp.einsum('bhgqk,bhkd->bhgqd', p, vt)

    o = lax.map(one_block, qb)
    return o.transpose(1, 0, 4, 2, 3, 5).reshape(B, L, ATT_Q_HEADS, dh)


def token_mixer(u, rope_tables, w_in, conv_w, a_log_fwd, a_log_bwd, dt_bias_fwd, dt_bias_bwd,
                gdn_out_norm, q_norm, k_norm, attn_out_norm, w_out):
    B, L, _ = u.shape
    proj = u @ w_in
    splits = np.cumsum([3 * GDN_W, GDN_W, 2 * GDN_HEADS, 2 * GDN_HEADS, ATT_Q_W, ATT_KV_W]).tolist()
    qkv_g, z, a, b, q_a, k_a, v_a = jnp.split(proj, splits, axis=-1)

    qkv_g = jax.nn.silu(centred_conv(qkv_g, conv_w)).astype(jnp.float32)
    qg, kg, vg = jnp.split(qkv_g, 3, axis=-1)
    to_heads = lambda t: t.reshape(B, L, GDN_HEADS, HEAD_DIM).transpose(0, 2, 1, 3)
    qg = l2norm(to_heads(qg)) * (HEAD_DIM ** -0.5)
    kg = l2norm(to_heads(kg))
    vg = to_heads(vg)
    a = a.astype(jnp.float32)
    b = b.astype(jnp.float32)
    g_f = -jnp.exp(a_log_fwd.astype(jnp.float32)) * jax.nn.softplus(a[..., :GDN_HEADS] + dt_bias_fwd.astype(jnp.float32))
    g_b = -jnp.exp(a_log_bwd.astype(jnp.float32)) * jax.nn.softplus(a[..., GDN_HEADS:] + dt_bias_bwd.astype(jnp.float32))
    beta_f = jax.nn.sigmoid(b[..., :GDN_HEADS])
    beta_b = jax.nn.sigmoid(b[..., GDN_HEADS:])
    to_bhl = lambda t: t.transpose(0, 2, 1)
    flip = lambda t: jnp.flip(t, axis=2)
    q2 = jnp.concatenate([qg, flip(qg)], axis=0)
    k2 = jnp.concatenate([kg, flip(kg)], axis=0)
    v2 = jnp.concatenate([vg, flip(vg)], axis=0)
    g2 = jnp.concatenate([to_bhl(g_f), flip(to_bhl(g_b))], axis=0)
    beta2 = jnp.concatenate([to_bhl(beta_f), flip(to_bhl(beta_b))], axis=0)
    o2 = gated_delta_chunked(q2, k2, v2, g2, beta2)
    o_g = (o2[:B] + flip(o2[B:])).transpose(0, 2, 1, 3)
    z_h = z.reshape(B, L, GDN_HEADS, HEAD_DIM).astype(jnp.float32)
    o_g = rmsnorm(o_g, gdn_out_norm) * jax.nn.silu(z_h)
    o_g = o_g.reshape(B, L, GDN_W).astype(u.dtype)

    qa = rmsnorm(q_a.reshape(B, L, ATT_Q_HEADS, HEAD_DIM), q_norm)
    ka = rmsnorm(k_a.reshape(B, L, ATT_KV_HEADS, HEAD_DIM), k_norm)
    va = v_a.reshape(B, L, ATT_KV_HEADS, HEAD_DIM)
    qa = apply_axial_rope(qa, rope_tables)
    ka = apply_axial_rope(ka, rope_tables)
    o_a = block_attention(qa, ka, va)
    o_a = rmsnorm(o_a, attn_out_norm).reshape(B, L, ATT_Q_W).astype(u.dtype)

    return jnp.concatenate([o_g, o_a], axis=-1) @ w_out


def encode(x, rows, ffn1_norm, ffn1_w_gate, ffn1_w_up, ffn1_w_down, mix_norm, w_in, conv_w,
           a_log_fwd, a_log_bwd, dt_bias_fwd, dt_bias_bwd, gdn_out_norm, q_norm, k_norm,
           attn_out_norm, w_out, ffn2_norm, ffn2_w_gate, ffn2_w_up, ffn2_w_down, final_norm):
    rope_tables = axial_rope_tables(rows)
    h = x
    for l in range(DEPTH):
        h = h + 0.5 * swiglu(rmsnorm(h, ffn1_norm[l]), ffn1_w_gate[l], ffn1_w_up[l], ffn1_w_down[l])
        h = h + token_mixer(rmsnorm(h, mix_norm[l]), rope_tables, w_in[l], conv_w[l],
                            a_log_fwd[l], a_log_bwd[l], dt_bias_fwd[l], dt_bias_bwd[l],
                            gdn_out_norm[l], q_norm[l], k_norm[l], attn_out_norm[l], w_out[l])
        h = h + 0.5 * swiglu(rmsnorm(h, ffn2_norm[l]), ffn2_w_gate[l], ffn2_w_up[l], ffn2_w_down[l])
    return rmsnorm(h, final_norm)


def setup_inputs(seed: int = 0) -> dict:
    key = jax.random.key(seed)
    ks = jax.random.split(key, 32)
    f32 = jnp.float32

    def w(k, shape, fan_in):
        return jax.random.normal(k, shape, f32) * (fan_in ** -0.5)

    def gain(k, shape):
        return 1.0 + 0.02 * jax.random.normal(k, shape, f32)

    def a_log(k):
        return jnp.log(jax.random.uniform(k, (DEPTH, GDN_HEADS), f32, 1.0, 16.0))

    def dt_bias(k):
        u = jax.random.uniform(k, (DEPTH, GDN_HEADS), f32)
        dt = jnp.exp(u * (math.log(0.1) - math.log(0.001)) + math.log(0.001))
        return dt + jnp.log(-jnp.expm1(-dt))

    return {
        "x_prompt": jax.random.normal(ks[0], (BATCH, SEQ, D_MODEL), f32),
        "x_sample": jax.random.normal(ks[1], (DEC_BATCH, DEC_SEQ, D_MODEL), f32),
        "ffn1_norm": gain(ks[2], (DEPTH, D_MODEL)),
        "ffn1_w_gate": w(ks[3], (DEPTH, D_MODEL, D_FF), D_MODEL),
        "ffn1_w_up": w(ks[4], (DEPTH, D_MODEL, D_FF), D_MODEL),
        "ffn1_w_down": w(ks[5], (DEPTH, D_FF, D_MODEL), D_FF),
        "mix_norm": gain(ks[6], (DEPTH, D_MODEL)),
        "w_in": w(ks[7], (DEPTH, D_MODEL, IN_DIM), D_MODEL),
        "conv_w": w(ks[8], (DEPTH, CONV_K, 3 * GDN_W), CONV_K),
        "a_log_fwd": a_log(ks[9]),
        "a_log_bwd": a_log(ks[10]),
        "dt_bias_fwd": dt_bias(ks[11]),
        "dt_bias_bwd": dt_bias(ks[12]),
        "gdn_out_norm": gain(ks[13], (DEPTH, HEAD_DIM)),
        "q_norm": gain(ks[14], (DEPTH, HEAD_DIM)),
        "k_norm": gain(ks[15], (DEPTH, HEAD_DIM)),
        "attn_out_norm": gain(ks[16], (DEPTH, HEAD_DIM)),
        "w_out": w(ks[17], (DEPTH, MIX_W, D_MODEL), MIX_W),
        "ffn2_norm": gain(ks[18], (DEPTH, D_MODEL)),
        "ffn2_w_gate": w(ks[19], (DEPTH, D_MODEL, D_FF), D_MODEL),
        "ffn2_w_up": w(ks[20], (DEPTH, D_MODEL, D_FF), D_MODEL),
        "ffn2_w_down": w(ks[21], (DEPTH, D_FF, D_MODEL), D_FF),
        "final_norm": gain(ks[22], (D_MODEL,)),
    }


def reference(x_prompt, x_sample, ffn1_norm, ffn1_w_gate, ffn1_w_up, ffn1_w_down, mix_norm, w_in, conv_w,
              a_log_fwd, a_log_bwd, dt_bias_fwd, dt_bias_bwd, gdn_out_norm, q_norm, k_norm,
              attn_out_norm, w_out, ffn2_norm, ffn2_w_gate, ffn2_w_up, ffn2_w_down, final_norm):
    rows_prompt = x_prompt.shape[1] // GRID_W
    rows_sample = x_sample.shape[1] // GRID_W
    y_prompt = encode(x_prompt, rows_prompt, ffn1_norm, ffn1_w_gate, ffn1_w_up, ffn1_w_down, mix_norm, w_in,
                      conv_w, a_log_fwd, a_log_bwd, dt_bias_fwd, dt_bias_bwd, gdn_out_norm, q_norm, k_norm,
                      attn_out_norm, w_out, ffn2_norm, ffn2_w_gate, ffn2_w_up, ffn2_w_down, final_norm)
    y_sample = encode(x_sample, rows_sample, ffn1_norm, ffn1_w_gate, ffn1_w_up, ffn1_w_down, mix_norm, w_in,
                      conv_w, a_log_fwd, a_log_bwd, dt_bias_fwd, dt_bias_bwd, gdn_out_norm, q_norm, k_norm,
                      attn_out_norm, w_out, ffn2_norm, ffn2_w_gate, ffn2_w_up, ffn2_w_down, final_norm)
    return (y_prompt, y_sample)
```

```python
import functools

import jax
import jax.numpy as jnp
from jax import lax
from jax.experimental import pallas as pl
from jax.experimental.pallas import tpu as pltpu

D_MODEL = 2048
HEAD_DIM = 128
GDN_HEADS = 8
GDN_W = GDN_HEADS * HEAD_DIM
ATT_Q_HEADS = 8
ATT_KV_HEADS = 2
ATT_GROUP = ATT_Q_HEADS // ATT_KV_HEADS
ATT_Q_W = ATT_Q_HEADS * HEAD_DIM
ATT_KV_W = ATT_KV_HEADS * HEAD_DIM
ATT_W = ATT_Q_W + 2 * ATT_KV_W
GDN_PROJ_W = 4 * GDN_W
GATE_COLS = 4 * GDN_HEADS
D_FF = 5632
CONV_K = 5
GRID_W = 64
AXIS_DIM = HEAD_DIM // 2
ROPE_THETA = 10000.0
EPS = 1e-6

LANES = 128
SUBLANES = 8
SCAN_CHUNK = 128
NEUMANN_STEPS = 7

F32 = jnp.float32
BF16 = jnp.bfloat16

_VMEM_LIMIT = 56 * 1024 * 1024


def _cparams(semantics):
    return pltpu.CompilerParams(dimension_semantics=semantics, vmem_limit_bytes=_VMEM_LIMIT)


def _rms(x, g):
    return x * lax.rsqrt(jnp.mean(x * x, axis=-1, keepdims=True) + EPS) * g


def _dot(a, b):
    return jnp.dot(a, b, preferred_element_type=F32)


def _dot_nt(a, b):
    return lax.dot_general(a, b, (((1,), (1,)), ((), ())), preferred_element_type=F32)


def _split3(x):
    hi = x.astype(BF16)
    r = x - hi.astype(F32)
    mid = r.astype(BF16)
    lo = (r - mid.astype(F32)).astype(BF16)
    return hi, mid, lo


def _ffn_kernel(x_ref, g_ref, wg_ref, wu_ref, wd_ref, fg_ref, o_ref, xn_ref, *, final):
    j = pl.program_id(1)

    @pl.when(j == 0)
    def _():
        xn_ref[...] = _rms(x_ref[...], g_ref[...]).astype(BF16)
        o_ref[...] = jnp.zeros_like(o_ref)

    xn = xn_ref[...]
    gate = _dot(xn, wg_ref[...])
    up = _dot(xn, wu_ref[...])
    act = (gate * jax.nn.sigmoid(gate) * up).astype(BF16)
    o_ref[...] += _dot(act, wd_ref[...])

    @pl.when(j == pl.num_programs(1) - 1)
    def _():
        h = x_ref[...] + 0.5 * o_ref[...]
        if final:
            h = _rms(h, fg_ref[...])
        o_ref[...] = h


def _ffn(x, norm_g, wg, wu, wd, final_g, *, final, tm=512, tf=512):
    L = x.shape[0]
    grid = (L // tm, D_FF // tf)
    return pl.pallas_call(
        functools.partial(_ffn_kernel, final=final),
        grid=grid,
        in_specs=[
            pl.BlockSpec((tm, D_MODEL), lambda i, j: (i, 0)),
            pl.BlockSpec((1, D_MODEL), lambda i, j: (0, 0)),
            pl.BlockSpec((D_MODEL, tf), lambda i, j: (0, j)),
            pl.BlockSpec((D_MODEL, tf), lambda i, j: (0, j)),
            pl.BlockSpec((tf, D_MODEL), lambda i, j: (j, 0)),
            pl.BlockSpec((1, D_MODEL), lambda i, j: (0, 0)),
        ],
        out_specs=pl.BlockSpec((tm, D_MODEL), lambda i, j: (i, 0)),
        out_shape=jax.ShapeDtypeStruct((L, D_MODEL), F32),
        scratch_shapes=[pltpu.VMEM((tm, D_MODEL), BF16)],
        compiler_params=_cparams(("parallel", "arbitrary")),
        name="ffn",
    )(x, norm_g, wg, wu, wd, final_g)


def _norm_matmul_kernel(x_ref, g_ref, w_ref, o_ref, xn_ref):
    @pl.when(pl.program_id(1) == 0)
    def _():
        xn_ref[...] = _rms(x_ref[...], g_ref[...]).astype(BF16)

    o_ref[...] = _dot(xn_ref[...], w_ref[...])


def _norm_matmul(x, norm_g, w, n_cols, *, tm=512, tn=1024):
    L = x.shape[0]
    grid = (L // tm, n_cols // tn)
    return pl.pallas_call(
        _norm_matmul_kernel,
        grid=grid,
        in_specs=[
            pl.BlockSpec((tm, D_MODEL), lambda i, j: (i, 0)),
            pl.BlockSpec((1, D_MODEL), lambda i, j: (0, 0)),
            pl.BlockSpec((D_MODEL, tn), lambda i, j: (0, j)),
        ],
        out_specs=pl.BlockSpec((tm, tn), lambda i, j: (i, j)),
        out_shape=jax.ShapeDtypeStruct((L, n_cols), F32),
        scratch_shapes=[pltpu.VMEM((tm, D_MODEL), BF16)],
        compiler_params=_cparams(("parallel", "arbitrary")),
        name="norm_matmul",
    )(x, norm_g, w)


def _rope(x, cos, sin_lo, sin_hi):
    return (x * cos + pltpu.roll(x, AXIS_DIM // 2, axis=1) * sin_hi
            + pltpu.roll(x, HEAD_DIM - AXIS_DIM // 2, axis=1) * sin_lo)


def _attn_proj_kernel(x_ref, g_ref, w_ref, wab_ref, qn_ref, kn_ref, cos_ref, slo_ref, shi_ref,
                      alog_ref, dtb_ref, q_ref, k_ref, v_ref, gb_ref):
    xn = _rms(x_ref[...], g_ref[...]).astype(BF16)
    cos = cos_ref[...]
    slo = slo_ref[...]
    shi = shi_ref[...]
    for h in range(ATT_Q_HEADS):
        cols = slice(h * HEAD_DIM, (h + 1) * HEAD_DIM)
        qh = _rms(_dot(xn, w_ref[:, cols]), qn_ref[...])
        q_ref[:, cols] = _rope(qh, cos, slo, shi).astype(BF16)
    for h in range(ATT_KV_HEADS):
        cols = slice(h * HEAD_DIM, (h + 1) * HEAD_DIM)
        kh = _rms(_dot(xn, w_ref[:, ATT_Q_W + h * HEAD_DIM:ATT_Q_W + (h + 1) * HEAD_DIM]), kn_ref[...])
        k_ref[:, cols] = _rope(kh, cos, slo, shi).astype(BF16)
    v_ref[...] = _dot(xn, w_ref[:, ATT_Q_W + ATT_KV_W:]).astype(BF16)

    ab = _dot(xn, wab_ref[...])
    t = ab + dtb_ref[...]
    softplus = jnp.maximum(t, 0.0) + jnp.log1p(jnp.exp(-jnp.abs(t)))
    g = -jnp.exp(alog_ref[...]) * softplus
    lane = lax.broadcasted_iota(jnp.int32, ab.shape, 1)
    gb_ref[...] = jnp.where(lane < 2 * GDN_HEADS, g, jax.nn.sigmoid(ab))


def _attn_proj(x, norm_g, w_att, w_ab, q_norm, k_norm, cos, sin_lo, sin_hi, alog, dtb, *, tm=256):
    L = x.shape[0]
    row = lambda i: (i, 0)
    fixed = lambda i: (0, 0)
    return pl.pallas_call(
        _attn_proj_kernel,
        grid=(L // tm,),
        in_specs=[
            pl.BlockSpec((tm, D_MODEL), row),
            pl.BlockSpec((1, D_MODEL), fixed),
            pl.BlockSpec((D_MODEL, ATT_W), fixed),
            pl.BlockSpec((D_MODEL, LANES), fixed),
            pl.BlockSpec((1, HEAD_DIM), fixed),
            pl.BlockSpec((1, HEAD_DIM), fixed),
            pl.BlockSpec((tm, HEAD_DIM), row),
            pl.BlockSpec((tm, HEAD_DIM), row),
            pl.BlockSpec((tm, HEAD_DIM), row),
            pl.BlockSpec((1, LANES), fixed),
            pl.BlockSpec((1, LANES), fixed),
        ],
        out_specs=[
            pl.BlockSpec((tm, ATT_Q_W), row),
            pl.BlockSpec((tm, ATT_KV_W), row),
            pl.BlockSpec((tm, ATT_KV_W), row),
            pl.BlockSpec((tm, LANES), row),
        ],
        out_shape=[
            jax.ShapeDtypeStruct((L, ATT_Q_W), BF16),
            jax.ShapeDtypeStruct((L, ATT_KV_W), BF16),
            jax.ShapeDtypeStruct((L, ATT_KV_W), BF16),
            jax.ShapeDtypeStruct((L, LANES), F32),
        ],
        compiler_params=_cparams(("parallel",)),
        name="attn_proj",
    )(x, norm_g, w_att, w_ab, q_norm, k_norm, cos, sin_lo, sin_hi, alog, dtb)


def _gdn_prep_kernel(x_ref, prev_ref, next_ref, cw_ref, gb_ref, q_ref, k_ref, v_ref, gcb_ref, gct_ref,
                     ext_ref, *, tm):
    i = pl.program_id(0)
    pad = CONV_K // 2
    ext_ref[0:SUBLANES, :] = jnp.where(i == 0, 0.0, prev_ref[...])
    ext_ref[SUBLANES:SUBLANES + tm, :] = x_ref[...]
    ext_ref[SUBLANES + tm:, :] = jnp.where(i == pl.num_programs(0) - 1, 0.0, next_ref[...])

    for s in range(3 * GDN_HEADS):
        cols = slice(s * HEAD_DIM, (s + 1) * HEAD_DIM)
        y = None
        for t in range(CONV_K):
            term = ext_ref[SUBLANES - pad + t:SUBLANES - pad + t + tm, cols] * cw_ref[t:t + 1, cols]
            y = term if y is None else y + term
        y = y * jax.nn.sigmoid(y)
        part, h = divmod(s, GDN_HEADS)
        hc = slice(h * HEAD_DIM, (h + 1) * HEAD_DIM)
        if part == 0:
            q_ref[:, hc] = y * lax.rsqrt(jnp.sum(y * y, axis=-1, keepdims=True) + EPS) * (HEAD_DIM ** -0.5)
        elif part == 1:
            k_ref[:, hc] = y * lax.rsqrt(jnp.sum(y * y, axis=-1, keepdims=True) + EPS)
        else:
            v_ref[:, hc] = y

    gb = gb_ref[...]
    r = lax.broadcasted_iota(jnp.int32, (tm, tm), 0)
    c = lax.broadcasted_iota(jnp.int32, (tm, tm), 1)
    same = (r // SCAN_CHUNK) == (c // SCAN_CHUNK)
    m_lo = jnp.where(same & (c <= r), 1.0, 0.0).astype(BF16)
    m_up = jnp.where(same & (c >= r), 1.0, 0.0).astype(BF16)
    parts = _split3(gb)
    gc_f = _dot(m_lo, parts[0]) + _dot(m_lo, parts[1]) + _dot(m_lo, parts[2])
    gc_b = _dot(m_up, parts[0]) + _dot(m_up, parts[1]) + _dot(m_up, parts[2])
    lane = lax.broadcasted_iota(jnp.int32, gb.shape, 1)
    gcb = jnp.where(lane < GDN_HEADS, gc_f, jnp.where(lane < 2 * GDN_HEADS, gc_b, gb))
    gcb_ref[...] = gcb
    for n in range(tm // SCAN_CHUNK):
        gct = gcb[n * SCAN_CHUNK:(n + 1) * SCAN_CHUNK, :].T
        gct_ref[n * 2 * GDN_HEADS:(n + 1) * 2 * GDN_HEADS, :] = gct[0:2 * GDN_HEADS, :]


def _gdn_prep(proj, conv_w8, gb, *, tm=256):
    L = proj.shape[0]
    n_sub = L // SUBLANES
    per = tm // SUBLANES
    row = lambda i: (i, 0)
    return pl.pallas_call(
        functools.partial(_gdn_prep_kernel, tm=tm),
        grid=(L // tm,),
        in_specs=[
            pl.BlockSpec((tm, 3 * GDN_W), row),
            pl.BlockSpec((SUBLANES, 3 * GDN_W), lambda i: (jnp.maximum(i * per - 1, 0), 0)),
            pl.BlockSpec((SUBLANES, 3 * GDN_W), lambda i: (jnp.minimum((i + 1) * per, n_sub - 1), 0)),
            pl.BlockSpec((SUBLANES, 3 * GDN_W), lambda i: (0, 0)),
            pl.BlockSpec((tm, LANES), row),
        ],
        out_specs=[
            pl.BlockSpec((tm, GDN_W), row),
            pl.BlockSpec((tm, GDN_W), row),
            pl.BlockSpec((tm, GDN_W), row),
            pl.BlockSpec((tm, LANES), row),
            pl.BlockSpec((tm // SCAN_CHUNK * 2 * GDN_HEADS, SCAN_CHUNK), row),
        ],
        out_shape=[
            jax.ShapeDtypeStruct((L, GDN_W), F32),
            jax.ShapeDtypeStruct((L, GDN_W), F32),
            jax.ShapeDtypeStruct((L, GDN_W), F32),
            jax.ShapeDtypeStruct((L, LANES), F32),
            jax.ShapeDtypeStruct((L // SCAN_CHUNK * 2 * GDN_HEADS, SCAN_CHUNK), F32),
        ],
        scratch_shapes=[pltpu.VMEM((tm + 2 * SUBLANES, 3 * GDN_W), F32)],
        compiler_params=_cparams(("parallel",)),
        name="gdn_prep",
    )(proj, proj, proj, conv_w8, gb)


def _gdn_scan_kernel(q_ref, k_ref, v_ref, gcb_ref, gct_ref, o_ref, s_ref):
    d = pl.program_id(0)
    C = SCAN_CHUNK

    @pl.when(pl.program_id(1) == 0)
    def _():
        s_ref[...] = jnp.zeros_like(s_ref)

    fwd = d == 0
    r = lax.broadcasted_iota(jnp.int32, (C, C), 0)
    c = lax.broadcasted_iota(jnp.int32, (C, C), 1)
    dif = jnp.where(fwd, r - c, c - r)
    m_incl = dif >= 0
    m_strict = dif > 0
    eye = jnp.where(dif == 0, 1.0, 0.0)

    gcb = gcb_ref[...]
    gc = jnp.where(fwd, gcb[:, 0:GDN_HEADS], gcb[:, GDN_HEADS:2 * GDN_HEADS])
    beta = jnp.where(fwd, gcb[:, 2 * GDN_HEADS:3 * GDN_HEADS], gcb[:, 3 * GDN_HEADS:4 * GDN_HEADS])
    g_end = jnp.where(fwd, gc[C - 1:C, :], gc[0:1, :])
    e_gc = jnp.exp(gc)
    e_rest = jnp.exp(g_end - gc)
    e_end = jnp.exp(g_end)
    gct = gct_ref[...]

    for h in range(GDN_HEADS):
        cols = slice(h * HEAD_DIM, (h + 1) * HEAD_DIM)
        hh = slice(h, h + 1)
        q = q_ref[:, cols]
        k = k_ref[:, cols]
        v = v_ref[:, cols]
        kb = k * beta[:, hh]
        vb = v * beta[:, hh]
        kbg = kb * e_gc[:, hh]
        q_dec = q * e_gc[:, hh]
        k_dec = k * e_rest[:, hh]
        k16 = k.astype(BF16)
        diff = gc[:, hh] - gct[hh, :]
        decay = jnp.where(m_incl, jnp.exp(jnp.where(m_incl, diff, 0.0)), 0.0)
        a = jnp.where(m_strict, _dot_nt(kb.astype(BF16), k16) * decay, 0.0)
        attn_qk = _dot_nt(q.astype(BF16), k16) * decay

        inv = eye - a
        a_pow = a
        for _ in range(NEUMANN_STEPS - 1):
            a16 = a_pow.astype(BF16)
            a_pow = _dot(a16, a16)
            inv = inv + _dot(inv.astype(BF16), a_pow.astype(BF16))
        inv16 = inv.astype(BF16)
        u = _dot(inv16, vb.astype(BF16))
        w = _dot(inv16, kbg.astype(BF16))

        s = s_ref[h]
        s16 = s.astype(BF16)
        v_new = u - _dot(w.astype(BF16), s16)
        v_new16 = v_new.astype(BF16)
        o_ref[:, cols] = _dot(q_dec.astype(BF16), s16) + _dot(attn_qk.astype(BF16), v_new16)
        s_ref[h] = s * e_end[:, hh] + _dot(k_dec.T.astype(BF16), v_new16)


def _gdn_scan(q, k, v, gcb, gct):
    L = q.shape[0]
    n = L // SCAN_CHUNK
    chunk = lambda d, i: i + d * (n - 1 - 2 * i)
    row = lambda d, i: (chunk(d, i), 0)
    return pl.pallas_call(
        _gdn_scan_kernel,
        grid=(2, n),
        in_specs=[
            pl.BlockSpec((SCAN_CHUNK, GDN_W), row),
            pl.BlockSpec((SCAN_CHUNK, GDN_W), row),
            pl.BlockSpec((SCAN_CHUNK, GDN_W), row),
            pl.BlockSpec((SCAN_CHUNK, LANES), row),
            pl.BlockSpec((GDN_HEADS, SCAN_CHUNK), lambda d, i: (2 * chunk(d, i) + d, 0)),
        ],
        out_specs=pl.BlockSpec((None, SCAN_CHUNK, GDN_W), lambda d, i: (d, chunk(d, i), 0)),
        out_shape=jax.ShapeDtypeStruct((2, L, GDN_W), F32),
        scratch_shapes=[pltpu.VMEM((GDN_HEADS, HEAD_DIM, HEAD_DIM), F32)],
        compiler_params=_cparams(("arbitrary", "arbitrary")),
        name="gdn_scan",
    )(q, k, v, gcb, gct)


def _flash_kernel(q_ref, k_ref, v_ref, on_ref, o_ref, m_ref, l_ref, acc_ref):
    ki = pl.program_id(2)

    @pl.when(ki == 0)
    def _():
        m_ref[...] = jnp.full_like(m_ref, -jnp.inf)
        l_ref[...] = jnp.zeros_like(l_ref)
        acc_ref[...] = jnp.zeros_like(acc_ref)

    k = k_ref[...]
    v = v_ref[...]
    scale = HEAD_DIM ** -0.5
    for h in range(ATT_GROUP):
        cols = slice(h * HEAD_DIM, (h + 1) * HEAD_DIM)
        s = _dot_nt(q_ref[:, cols], k) * scale
        m_prev = m_ref[h][:, 0:1]
        m_new = jnp.maximum(m_prev, jnp.max(s, axis=-1, keepdims=True))
        alpha = jnp.exp(m_prev - m_new)
        p = jnp.exp(s - m_new)
        l_new = alpha * l_ref[h][:, 0:1] + jnp.sum(p, axis=-1, keepdims=True)
        acc_ref[h] = alpha * acc_ref[h] + _dot(p.astype(BF16), v)
        m_ref[h] = jnp.broadcast_to(m_new, m_ref.shape[1:])
        l_ref[h] = jnp.broadcast_to(l_new, l_ref.shape[1:])

    @pl.when(ki == pl.num_programs(2) - 1)
    def _():
        for h in range(ATT_GROUP):
            cols = slice(h * HEAD_DIM, (h + 1) * HEAD_DIM)
            o = acc_ref[h] / l_ref[h][:, 0:1]
            o_ref[:, cols] = _rms(o, on_ref[...]).astype(BF16)


def _flash_attn(q, k, v, out_norm, *, tq=512, tk=512):
    L = q.shape[0]
    gw = ATT_GROUP * HEAD_DIM
    return pl.pallas_call(
        _flash_kernel,
        grid=(ATT_KV_HEADS, L // tq, L // tk),
        in_specs=[
            pl.BlockSpec((tq, gw), lambda g, i, j: (i, g)),
            pl.BlockSpec((tk, HEAD_DIM), lambda g, i, j: (j, g)),
            pl.BlockSpec((tk, HEAD_DIM), lambda g, i, j: (j, g)),
            pl.BlockSpec((1, HEAD_DIM), lambda g, i, j: (0, 0)),
        ],
        out_specs=pl.BlockSpec((tq, gw), lambda g, i, j: (i, g)),
        out_shape=jax.ShapeDtypeStruct((L, ATT_Q_W), BF16),
        scratch_shapes=[
            pltpu.VMEM((ATT_GROUP, tq, LANES), F32),
            pltpu.VMEM((ATT_GROUP, tq, LANES), F32),
            pltpu.VMEM((ATT_GROUP, tq, HEAD_DIM), F32),
        ],
        compiler_params=_cparams(("parallel", "parallel", "arbitrary")),
        name="flash_attn",
    )(q, k, v, out_norm)


def _out_proj_kernel(og_ref, z_ref, oa_ref, h_ref, gn_ref, w_ref, o_ref, mix_ref):
    for h in range(GDN_HEADS):
        cols = slice(h * HEAD_DIM, (h + 1) * HEAD_DIM)
        o = _rms(og_ref[0, :, cols] + og_ref[1, :, cols], gn_ref[...])
        z = z_ref[:, cols]
        mix_ref[:, cols] = (o * (z * jax.nn.sigmoid(z))).astype(BF16)
    mix_ref[:, GDN_W:] = oa_ref[...]
    o_ref[...] = h_ref[...] + _dot(mix_ref[...], w_ref[...])


def _out_proj(og, proj, oa, h1, gdn_norm, w_out, *, tm=256):
    L = h1.shape[0]
    row = lambda i: (i, 0)
    return pl.pallas_call(
        _out_proj_kernel,
        grid=(L // tm,),
        in_specs=[
            pl.BlockSpec((2, tm, GDN_W), lambda i: (0, i, 0)),
            pl.BlockSpec((tm, GDN_W), lambda i: (i, 3)),
            pl.BlockSpec((tm, ATT_Q_W), row),
            pl.BlockSpec((tm, D_MODEL), row),
            pl.BlockSpec((1, HEAD_DIM), lambda i: (0, 0)),
            pl.BlockSpec((GDN_W + ATT_Q_W, D_MODEL), lambda i: (0, 0)),
        ],
        out_specs=pl.BlockSpec((tm, D_MODEL), row),
        out_shape=jax.ShapeDtypeStruct((L, D_MODEL), F32),
        scratch_shapes=[pltpu.VMEM((tm, GDN_W + ATT_Q_W), BF16)],
        compiler_params=_cparams(("parallel",)),
        name="out_proj",
    )(og, proj, oa, h1, gdn_norm, w_out)


def _rope_tables(L):
    t = jnp.arange(L, dtype=jnp.int32)
    row = (t // GRID_W).astype(F32)
    col = (t % GRID_W).astype(F32)
    freqs = ROPE_THETA ** (-jnp.arange(0, AXIS_DIM, 2, dtype=F32) / AXIS_DIM)
    ang_r = row[:, None] * freqs[None, :]
    ang_c = col[:, None] * freqs[None, :]
    zero = jnp.zeros_like(ang_r)
    cos = jnp.concatenate([jnp.cos(ang_r)] * 2 + [jnp.cos(ang_c)] * 2, axis=-1)
    sin_lo = jnp.concatenate([-jnp.sin(ang_r), zero, -jnp.sin(ang_c), zero], axis=-1)
    sin_hi = jnp.concatenate([zero, jnp.sin(ang_r), zero, jnp.sin(ang_c)], axis=-1)
    return cos, sin_lo, sin_hi


def _pad_lanes(x):
    x = x.reshape(1, -1)
    return jnp.pad(x, ((0, 0), (0, LANES - x.shape[1])))


def _encode(x, p):
    L = x.shape[0]
    h1 = _ffn(x, p["ffn1_norm"], p["ffn1_wg"], p["ffn1_wu"], p["ffn1_wd"], p["final_norm"], final=False)
    proj = _norm_matmul(h1, p["mix_norm"], p["w_in"], GDN_PROJ_W)
    cos, sin_lo, sin_hi = _rope_tables(L)
    qa, ka, va, gb = _attn_proj(h1, p["mix_norm"], p["w_att"], p["w_ab"], p["q_norm"], p["k_norm"],
                                cos, sin_lo, sin_hi, p["alog"], p["dtb"])
    qg, kg, vg, gcb, gct = _gdn_prep(proj, p["conv_w"], gb)
    og = _gdn_scan(qg, kg, vg, gcb, gct)
    oa = _flash_attn(qa, ka, va, p["attn_out_norm"])
    h2 = _out_proj(og, proj, oa, h1, p["gdn_out_norm"], p["w_out"])
    return _ffn(h2, p["ffn2_norm"], p["ffn2_wg"], p["ffn2_wu"], p["ffn2_wd"], p["final_norm"], final=True)


def _prepare_params(ffn1_norm, ffn1_w_gate, ffn1_w_up, ffn1_w_down, mix_norm, w_in, conv_w, a_log_fwd, a_log_bwd,
                    dt_bias_fwd, dt_bias_bwd, gdn_out_norm, q_norm, k_norm, attn_out_norm, w_out, ffn2_norm,
                    ffn2_w_gate, ffn2_w_up, ffn2_w_down, final_norm):
    w_in0 = w_in[0]
    att0 = GDN_PROJ_W + GATE_COLS
    w_ab = jnp.pad(w_in0[:, GDN_PROJ_W:att0], ((0, 0), (0, LANES - GATE_COLS)))
    return dict(
        ffn1_norm=ffn1_norm[0].reshape(1, -1),
        ffn1_wg=ffn1_w_gate[0].astype(BF16), ffn1_wu=ffn1_w_up[0].astype(BF16), ffn1_wd=ffn1_w_down[0].astype(BF16),
        mix_norm=mix_norm[0].reshape(1, -1),
        w_in=w_in0.astype(BF16),
        w_att=w_in0[:, att0:].astype(BF16),
        w_ab=w_ab.astype(BF16),
        conv_w=jnp.pad(conv_w[0], ((0, SUBLANES - CONV_K), (0, 0))),
        alog=_pad_lanes(jnp.concatenate([a_log_fwd[0], a_log_bwd[0]])),
        dtb=_pad_lanes(jnp.concatenate([dt_bias_fwd[0], dt_bias_bwd[0]])),
        gdn_out_norm=gdn_out_norm[0].reshape(1, -1),
        q_norm=q_norm[0].reshape(1, -1), k_norm=k_norm[0].reshape(1, -1),
        attn_out_norm=attn_out_norm[0].reshape(1, -1),
        w_out=w_out[0].astype(BF16),
        ffn2_norm=ffn2_norm[0].reshape(1, -1),
        ffn2_wg=ffn2_w_gate[0].astype(BF16), ffn2_wu=ffn2_w_up[0].astype(BF16), ffn2_wd=ffn2_w_down[0].astype(BF16),
        final_norm=final_norm.reshape(1, -1),
    )


def kernel(x_prompt, x_sample, ffn1_norm, ffn1_w_gate, ffn1_w_up, ffn1_w_down, mix_norm, w_in, conv_w, a_log_fwd,
           a_log_bwd, dt_bias_fwd, dt_bias_bwd, gdn_out_norm, q_norm, k_norm, attn_out_norm, w_out, ffn2_norm,
           ffn2_w_gate, ffn2_w_up, ffn2_w_down, final_norm):
    assert x_prompt.shape[0] == 1 and x_sample.shape[0] == 1
    p = _prepare_params(ffn1_norm, ffn1_w_gate, ffn1_w_up, ffn1_w_down, mix_norm, w_in, conv_w, a_log_fwd,
                        a_log_bwd, dt_bias_fwd, dt_bias_bwd, gdn_out_norm, q_norm, k_norm, attn_out_norm, w_out,
                        ffn2_norm, ffn2_w_gate, ffn2_w_up, ffn2_w_down, final_norm)
    y_prompt = _encode(x_prompt[0], p)
    y_sample = _encode(x_sample[0], p)
    return (y_prompt[None], y_sample[None])
```

```python
import functools
import math

import jax
import jax.numpy as jnp
from jax import lax
from jax.experimental import pallas as pl
from jax.experimental.pallas import tpu as pltpu

D_MODEL = 2048
HEAD_DIM = 128
GDN_HEADS = 8
GDN_W = GDN_HEADS * HEAD_DIM
ATT_Q_HEADS = 8
ATT_KV_HEADS = 2
ATT_GROUP = ATT_Q_HEADS // ATT_KV_HEADS
ATT_Q_W = ATT_Q_HEADS * HEAD_DIM
ATT_KV_W = ATT_KV_HEADS * HEAD_DIM
ATT_W = ATT_Q_W + 2 * ATT_KV_W
GDN_PROJ_W = 4 * GDN_W
GATE_COLS = 4 * GDN_HEADS
D_FF = 5632
CONV_K = 5
GRID_W = 64
AXIS_DIM = HEAD_DIM // 2
ROPE_THETA = 10000.0
EPS = 1e-6

LANES = 128
SUBLANES = 8
SCAN_CHUNK = 128
NEUMANN_STEPS = 7
EXP2_SCALE = HEAD_DIM ** -0.5 * math.log2(math.e)

F32 = jnp.float32
BF16 = jnp.bfloat16

_VMEM_LIMIT = 56 * 1024 * 1024


def _cparams(semantics):
    return pltpu.CompilerParams(dimension_semantics=semantics, vmem_limit_bytes=_VMEM_LIMIT)


def _rms(x, g):
    return x * lax.rsqrt(jnp.mean(x * x, axis=-1, keepdims=True) + EPS) * g


def _dot(a, b):
    return jnp.dot(a, b, preferred_element_type=F32)


def _dot_nt(a, b):
    return lax.dot_general(a, b, (((1,), (1,)), ((), ())), preferred_element_type=F32)


def _split3(x):
    hi = x.astype(BF16)
    r = x - hi.astype(F32)
    mid = r.astype(BF16)
    lo = (r - mid.astype(F32)).astype(BF16)
    return hi, mid, lo


def _ffn_kernel(x_ref, g_ref, wg_ref, wu_ref, wd_ref, fg_ref, o_ref, xn_ref, *, final):
    j = pl.program_id(1)

    @pl.when(j == 0)
    def _():
        xn_ref[...] = _rms(x_ref[...], g_ref[...]).astype(BF16)
        o_ref[...] = jnp.zeros_like(o_ref)

    xn = xn_ref[...]
    gate = _dot(xn, wg_ref[...])
    up = _dot(xn, wu_ref[...])
    act = (gate * jax.nn.sigmoid(gate) * up).astype(BF16)
    o_ref[...] += _dot(act, wd_ref[...])

    @pl.when(j == pl.num_programs(1) - 1)
    def _():
        h = x_ref[...] + 0.5 * o_ref[...]
        if final:
            h = _rms(h, fg_ref[...])
        o_ref[...] = h


def _ffn(x, norm_g, wg, wu, wd, final_g, *, final, tm=512, tf=512):
    L = x.shape[0]
    grid = (L // tm, D_FF // tf)
    return pl.pallas_call(
        functools.partial(_ffn_kernel, final=final),
        grid=grid,
        in_specs=[
            pl.BlockSpec((tm, D_MODEL), lambda i, j: (i, 0)),
            pl.BlockSpec((1, D_MODEL), lambda i, j: (0, 0)),
            pl.BlockSpec((D_MODEL, tf), lambda i, j: (0, j)),
            pl.BlockSpec((D_MODEL, tf), lambda i, j: (0, j)),
            pl.BlockSpec((tf, D_MODEL), lambda i, j: (j, 0)),
            pl.BlockSpec((1, D_MODEL), lambda i, j: (0, 0)),
        ],
        out_specs=pl.BlockSpec((tm, D_MODEL), lambda i, j: (i, 0)),
        out_shape=jax.ShapeDtypeStruct((L, D_MODEL), F32),
        scratch_shapes=[pltpu.VMEM((tm, D_MODEL), BF16)],
        compiler_params=_cparams(("parallel", "arbitrary")),
        name="ffn",
    )(x, norm_g, wg, wu, wd, final_g)


def _norm_matmul_kernel(x_ref, g_ref, w_ref, o_ref, xn_ref):
    @pl.when(pl.program_id(1) == 0)
    def _():
        xn_ref[...] = _rms(x_ref[...], g_ref[...]).astype(BF16)

    o_ref[...] = _dot(xn_ref[...], w_ref[...])


def _norm_matmul(x, norm_g, w, n_cols, *, tm=512, tn=1024):
    L = x.shape[0]
    grid = (L // tm, n_cols // tn)
    return pl.pallas_call(
        _norm_matmul_kernel,
        grid=grid,
        in_specs=[
            pl.BlockSpec((tm, D_MODEL), lambda i, j: (i, 0)),
            pl.BlockSpec((1, D_MODEL), lambda i, j: (0, 0)),
            pl.BlockSpec((D_MODEL, tn), lambda i, j: (0, j)),
        ],
        out_specs=pl.BlockSpec((tm, tn), lambda i, j: (i, j)),
        out_shape=jax.ShapeDtypeStruct((L, n_cols), F32),
        scratch_shapes=[pltpu.VMEM((tm, D_MODEL), BF16)],
        compiler_params=_cparams(("parallel", "arbitrary")),
        name="norm_matmul",
    )(x, norm_g, w)


def _rope(x, cos, sin_lo, sin_hi):
    return (x * cos + pltpu.roll(x, AXIS_DIM // 2, axis=1) * sin_hi
            + pltpu.roll(x, HEAD_DIM - AXIS_DIM // 2, axis=1) * sin_lo)


def _attn_proj_kernel(x_ref, g_ref, w_ref, wvt_ref, wab_ref, qn_ref, kn_ref, cos_ref, slo_ref, shi_ref,
                      alog_ref, dtb_ref, q_ref, k_ref, vt_ref, gb_ref):
    xn = _rms(x_ref[...], g_ref[...]).astype(BF16)
    cos = cos_ref[...]
    slo = slo_ref[...]
    shi = shi_ref[...]
    for h in range(ATT_Q_HEADS):
        cols = slice(h * HEAD_DIM, (h + 1) * HEAD_DIM)
        qh = _rms(_dot(xn, w_ref[:, cols]), qn_ref[...])
        q_ref[:, cols] = _rope(qh, cos, slo, shi).astype(BF16)
    for h in range(ATT_KV_HEADS):
        cols = slice(h * HEAD_DIM, (h + 1) * HEAD_DIM)
        kh = _rms(_dot(xn, w_ref[:, ATT_Q_W + h * HEAD_DIM:ATT_Q_W + (h + 1) * HEAD_DIM]), kn_ref[...])
        k_ref[:, cols] = _rope(kh, cos, slo, shi).astype(BF16)
    vt_ref[...] = _dot_nt(wvt_ref[...], xn).astype(BF16)

    ab = _dot(xn, wab_ref[...])
    t = ab + dtb_ref[...]
    softplus = jnp.maximum(t, 0.0) + jnp.log1p(jnp.exp(-jnp.abs(t)))
    g = -jnp.exp(alog_ref[...]) * softplus
    lane = lax.broadcasted_iota(jnp.int32, ab.shape, 1)
    gb_ref[...] = jnp.where(lane < 2 * GDN_HEADS, g, jax.nn.sigmoid(ab))


def _attn_proj(x, norm_g, w_qk, w_vt, w_ab, q_norm, k_norm, cos, sin_lo, sin_hi, alog, dtb, *, tm=256):
    L = x.shape[0]
    row = lambda i: (i, 0)
    fixed = lambda i: (0, 0)
    return pl.pallas_call(
        _attn_proj_kernel,
        grid=(L // tm,),
        in_specs=[
            pl.BlockSpec((tm, D_MODEL), row),
            pl.BlockSpec((1, D_MODEL), fixed),
            pl.BlockSpec((D_MODEL, ATT_Q_W + ATT_KV_W), fixed),
            pl.BlockSpec((ATT_KV_W, D_MODEL), fixed),
            pl.BlockSpec((D_MODEL, LANES), fixed),
            pl.BlockSpec((1, HEAD_DIM), fixed),
            pl.BlockSpec((1, HEAD_DIM), fixed),
            pl.BlockSpec((tm, HEAD_DIM), row),
            pl.BlockSpec((tm, HEAD_DIM), row),
            pl.BlockSpec((tm, HEAD_DIM), row),
            pl.BlockSpec((1, LANES), fixed),
            pl.BlockSpec((1, LANES), fixed),
        ],
        out_specs=[
            pl.BlockSpec((tm, ATT_Q_W), row),
            pl.BlockSpec((tm, ATT_KV_W), row),
            pl.BlockSpec((ATT_KV_W, tm), lambda i: (0, i)),
            pl.BlockSpec((tm, LANES), row),
        ],
        out_shape=[
            jax.ShapeDtypeStruct((L, ATT_Q_W), BF16),
            jax.ShapeDtypeStruct((L, ATT_KV_W), BF16),
            jax.ShapeDtypeStruct((ATT_KV_W, L), BF16),
            jax.ShapeDtypeStruct((L, LANES), F32),
        ],
        compiler_params=_cparams(("parallel",)),
        name="attn_proj",
    )(x, norm_g, w_qk, w_vt, w_ab, q_norm, k_norm, cos, sin_lo, sin_hi, alog, dtb)


def _gdn_prep_kernel(x_ref, prev_ref, next_ref, cw_ref, gb_ref, q_ref, k_ref, v_ref, gcb_ref, gct_ref,
                     ext_ref, *, tm):
    i = pl.program_id(0)
    pad = CONV_K // 2
    ext_ref[0:SUBLANES, :] = jnp.where(i == 0, 0.0, prev_ref[...])
    ext_ref[SUBLANES:SUBLANES + tm, :] = x_ref[...]
    ext_ref[SUBLANES + tm:, :] = jnp.where(i == pl.num_programs(0) - 1, 0.0, next_ref[...])

    for s in range(3 * GDN_HEADS):
        cols = slice(s * HEAD_DIM, (s + 1) * HEAD_DIM)
        y = None
        for t in range(CONV_K):
            term = ext_ref[SUBLANES - pad + t:SUBLANES - pad + t + tm, cols] * cw_ref[t:t + 1, cols]
            y = term if y is None else y + term
        y = y * jax.nn.sigmoid(y)
        part, h = divmod(s, GDN_HEADS)
        hc = slice(h * HEAD_DIM, (h + 1) * HEAD_DIM)
        if part == 0:
            q_ref[:, hc] = y * lax.rsqrt(jnp.sum(y * y, axis=-1, keepdims=True) + EPS) * (HEAD_DIM ** -0.5)
        elif part == 1:
            k_ref[:, hc] = y * lax.rsqrt(jnp.sum(y * y, axis=-1, keepdims=True) + EPS)
        else:
            v_ref[:, hc] = y

    gb = gb_ref[...]
    r = lax.broadcasted_iota(jnp.int32, (tm, tm), 0)
    c = lax.broadcasted_iota(jnp.int32, (tm, tm), 1)
    same = (r // SCAN_CHUNK) == (c // SCAN_CHUNK)
    m_lo = jnp.where(same & (c <= r), 1.0, 0.0).astype(BF16)
    m_up = jnp.where(same & (c >= r), 1.0, 0.0).astype(BF16)
    parts = _split3(gb)
    gc_f = _dot(m_lo, parts[0]) + _dot(m_lo, parts[1]) + _dot(m_lo, parts[2])
    gc_b = _dot(m_up, parts[0]) + _dot(m_up, parts[1]) + _dot(m_up, parts[2])
    lane = lax.broadcasted_iota(jnp.int32, gb.shape, 1)
    gcb = jnp.where(lane < GDN_HEADS, gc_f, jnp.where(lane < 2 * GDN_HEADS, gc_b, gb))
    gcb_ref[...] = gcb
    for n in range(tm // SCAN_CHUNK):
        gct = gcb[n * SCAN_CHUNK:(n + 1) * SCAN_CHUNK, :].T
        gct_ref[n * 2 * GDN_HEADS:(n + 1) * 2 * GDN_HEADS, :] = gct[0:2 * GDN_HEADS, :]


def _gdn_prep(proj, conv_w8, gb, *, tm=256):
    L = proj.shape[0]
    n_sub = L // SUBLANES
    per = tm // SUBLANES
    row = lambda i: (i, 0)
    return pl.pallas_call(
        functools.partial(_gdn_prep_kernel, tm=tm),
        grid=(L // tm,),
        in_specs=[
            pl.BlockSpec((tm, 3 * GDN_W), row),
            pl.BlockSpec((SUBLANES, 3 * GDN_W), lambda i: (jnp.maximum(i * per - 1, 0), 0)),
            pl.BlockSpec((SUBLANES, 3 * GDN_W), lambda i: (jnp.minimum((i + 1) * per, n_sub - 1), 0)),
            pl.BlockSpec((SUBLANES, 3 * GDN_W), lambda i: (0, 0)),
            pl.BlockSpec((tm, LANES), row),
        ],
        out_specs=[
            pl.BlockSpec((tm, GDN_W), row),
            pl.BlockSpec((tm, GDN_W), row),
            pl.BlockSpec((tm, GDN_W), row),
            pl.BlockSpec((tm, LANES), row),
            pl.BlockSpec((tm // SCAN_CHUNK * 2 * GDN_HEADS, SCAN_CHUNK), row),
        ],
        out_shape=[
            jax.ShapeDtypeStruct((L, GDN_W), F32),
            jax.ShapeDtypeStruct((L, GDN_W), F32),
            jax.ShapeDtypeStruct((L, GDN_W), F32),
            jax.ShapeDtypeStruct((L, LANES), F32),
            jax.ShapeDtypeStruct((L // SCAN_CHUNK * 2 * GDN_HEADS, SCAN_CHUNK), F32),
        ],
        scratch_shapes=[pltpu.VMEM((tm + 2 * SUBLANES, 3 * GDN_W), F32)],
        compiler_params=_cparams(("parallel",)),
        name="gdn_prep",
    )(proj, proj, proj, conv_w8, gb)


def _gdn_scan_kernel(q_ref, k_ref, v_ref, gcb_ref, gct_ref, o_ref, s_ref):
    d = pl.program_id(0)
    C = SCAN_CHUNK

    @pl.when(pl.program_id(1) == 0)
    def _():
        s_ref[...] = jnp.zeros_like(s_ref)

    fwd = d == 0
    r = lax.broadcasted_iota(jnp.int32, (C, C), 0)
    c = lax.broadcasted_iota(jnp.int32, (C, C), 1)
    dif = jnp.where(fwd, r - c, c - r)
    m_incl = dif >= 0
    m_strict = dif > 0
    eye = jnp.where(dif == 0, 1.0, 0.0)

    gcb = gcb_ref[...]
    gc = jnp.where(fwd, gcb[:, 0:GDN_HEADS], gcb[:, GDN_HEADS:2 * GDN_HEADS])
    beta = jnp.where(fwd, gcb[:, 2 * GDN_HEADS:3 * GDN_HEADS], gcb[:, 3 * GDN_HEADS:4 * GDN_HEADS])
    g_end = jnp.where(fwd, gc[C - 1:C, :], gc[0:1, :])
    e_gc = jnp.exp(gc)
    e_rest = jnp.exp(g_end - gc)
    e_end = jnp.exp(g_end)
    gct = gct_ref[...]

    for h in range(GDN_HEADS):
        cols = slice(h * HEAD_DIM, (h + 1) * HEAD_DIM)
        hh = slice(h, h + 1)
        q = q_ref[:, cols]
        k = k_ref[:, cols]
        v = v_ref[:, cols]
        kb = k * beta[:, hh]
        vb = v * beta[:, hh]
        kbg = kb * e_gc[:, hh]
        q_dec = q * e_gc[:, hh]
        k_dec = k * e_rest[:, hh]
        k16 = k.astype(BF16)
        diff = gc[:, hh] - gct[hh, :]
        decay = jnp.where(m_incl, jnp.exp(jnp.where(m_incl, diff, 0.0)), 0.0)
        a = jnp.where(m_strict, _dot_nt(kb.astype(BF16), k16) * decay, 0.0)
        attn_qk = _dot_nt(q.astype(BF16), k16) * decay

        inv = eye - a
        a_pow = a
        for _ in range(NEUMANN_STEPS - 1):
            a16 = a_pow.astype(BF16)
            a_pow = _dot(a16, a16)
            inv = inv + _dot(inv.astype(BF16), a_pow.astype(BF16))
        inv16 = inv.astype(BF16)
        u = _dot(inv16, vb.astype(BF16))
        w = _dot(inv16, kbg.astype(BF16))

        s = s_ref[h]
        s16 = s.astype(BF16)
        v_new = u - _dot(w.astype(BF16), s16)
        v_new16 = v_new.astype(BF16)
        o_ref[:, cols] = _dot(q_dec.astype(BF16), s16) + _dot(attn_qk.astype(BF16), v_new16)
        s_ref[h] = s * e_end[:, hh] + _dot(k_dec.T.astype(BF16), v_new16)


def _gdn_scan(q, k, v, gcb, gct):
    L = q.shape[0]
    n = L // SCAN_CHUNK
    chunk = lambda d, i: i + d * (n - 1 - 2 * i)
    row = lambda d, i: (chunk(d, i), 0)
    return pl.pallas_call(
        _gdn_scan_kernel,
        grid=(2, n),
        in_specs=[
            pl.BlockSpec((SCAN_CHUNK, GDN_W), row),
            pl.BlockSpec((SCAN_CHUNK, GDN_W), row),
            pl.BlockSpec((SCAN_CHUNK, GDN_W), row),
            pl.BlockSpec((SCAN_CHUNK, LANES), row),
            pl.BlockSpec((GDN_HEADS, SCAN_CHUNK), lambda d, i: (2 * chunk(d, i) + d, 0)),
        ],
        out_specs=pl.BlockSpec((None, SCAN_CHUNK, GDN_W), lambda d, i: (d, chunk(d, i), 0)),
        out_shape=jax.ShapeDtypeStruct((2, L, GDN_W), F32),
        scratch_shapes=[pltpu.VMEM((GDN_HEADS, HEAD_DIM, HEAD_DIM), F32)],
        compiler_params=_cparams(("arbitrary", "arbitrary")),
        name="gdn_scan",
    )(q, k, v, gcb, gct)


def _flash_kernel(q_ref, k_ref, vt_ref, on_ref, o_ref, m_ref, l_ref, acc_ref):
    ki = pl.program_id(2)

    @pl.when(ki == 0)
    def _():
        m_ref[...] = jnp.full_like(m_ref, -jnp.inf)
        l_ref[...] = jnp.zeros_like(l_ref)
        acc_ref[...] = jnp.zeros_like(acc_ref)

    k = k_ref[...]
    vt = vt_ref[...]
    for h in range(ATT_GROUP):
        cols = slice(h * HEAD_DIM, (h + 1) * HEAD_DIM)
        st = _dot_nt(k, q_ref[:, cols])
        m_prev = m_ref[h]
        m_new = jnp.maximum(m_prev, jnp.max(st, axis=0, keepdims=True))
        alpha = jnp.exp2((m_prev - m_new) * EXP2_SCALE)
        p = jnp.exp2((st - m_new) * EXP2_SCALE)
        l_ref[h] = alpha * l_ref[h] + jnp.sum(p, axis=0, keepdims=True)
        acc_ref[h] = alpha * acc_ref[h] + _dot(vt, p.astype(BF16))
        m_ref[h] = m_new

    @pl.when(ki == pl.num_programs(2) - 1)
    def _():
        for h in range(ATT_GROUP):
            cols = slice(h * HEAD_DIM, (h + 1) * HEAD_DIM)
            o = (acc_ref[h] / l_ref[h]).T
            o_ref[:, cols] = _rms(o, on_ref[...]).astype(BF16)


def _flash_attn(q, k, vt, out_norm, *, tq=512, tk=512):
    L = q.shape[0]
    gw = ATT_GROUP * HEAD_DIM
    return pl.pallas_call(
        _flash_kernel,
        grid=(ATT_KV_HEADS, L // tq, L // tk),
        in_specs=[
            pl.BlockSpec((tq, gw), lambda g, i, j: (i, g)),
            pl.BlockSpec((tk, HEAD_DIM), lambda g, i, j: (j, g)),
            pl.BlockSpec((HEAD_DIM, tk), lambda g, i, j: (g, j)),
            pl.BlockSpec((1, HEAD_DIM), lambda g, i, j: (0, 0)),
        ],
        out_specs=pl.BlockSpec((tq, gw), lambda g, i, j: (i, g)),
        out_shape=jax.ShapeDtypeStruct((L, ATT_Q_W), BF16),
        scratch_shapes=[
            pltpu.VMEM((ATT_GROUP, 1, tq), F32),
            pltpu.VMEM((ATT_GROUP, 1, tq), F32),
            pltpu.VMEM((ATT_GROUP, HEAD_DIM, tq), F32),
        ],
        compiler_params=_cparams(("parallel", "parallel", "arbitrary")),
        name="flash_attn",
    )(q, k, vt, out_norm)


def _out_proj_kernel(og_ref, z_ref, oa_ref, h_ref, gn_ref, w_ref, o_ref, mix_ref):
    for h in range(GDN_HEADS):
        cols = slice(h * HEAD_DIM, (h + 1) * HEAD_DIM)
        o = _rms(og_ref[0, :, cols] + og_ref[1, :, cols], gn_ref[...])
        z = z_ref[:, cols]
        mix_ref[:, cols] = (o * (z * jax.nn.sigmoid(z))).astype(BF16)
    mix_ref[:, GDN_W:] = oa_ref[...]
    o_ref[...] = h_ref[...] + _dot(mix_ref[...], w_ref[...])


def _out_proj(og, proj, oa, h1, gdn_norm, w_out, *, tm=256):
    L = h1.shape[0]
    row = lambda i: (i, 0)
    return pl.pallas_call(
        _out_proj_kernel,
        grid=(L // tm,),
        in_specs=[
            pl.BlockSpec((2, tm, GDN_W), lambda i: (0, i, 0)),
            pl.BlockSpec((tm, GDN_W), lambda i: (i, 3)),
            pl.BlockSpec((tm, ATT_Q_W), row),
            pl.BlockSpec((tm, D_MODEL), row),
            pl.BlockSpec((1, HEAD_DIM), lambda i: (0, 0)),
            pl.BlockSpec((GDN_W + ATT_Q_W, D_MODEL), lambda i: (0, 0)),
        ],
        out_specs=pl.BlockSpec((tm, D_MODEL), row),
        out_shape=jax.ShapeDtypeStruct((L, D_MODEL), F32),
        scratch_shapes=[pltpu.VMEM((tm, GDN_W + ATT_Q_W), BF16)],
        compiler_params=_cparams(("parallel",)),
        name="out_proj",
    )(og, proj, oa, h1, gdn_norm, w_out)


def _rope_tables(L):
    t = jnp.arange(L, dtype=jnp.int32)
    row = (t // GRID_W).astype(F32)
    col = (t % GRID_W).astype(F32)
    freqs = ROPE_THETA ** (-jnp.arange(0, AXIS_DIM, 2, dtype=F32) / AXIS_DIM)
    ang_r = row[:, None] * freqs[None, :]
    ang_c = col[:, None] * freqs[None, :]
    zero = jnp.zeros_like(ang_r)
    cos = jnp.concatenate([jnp.cos(ang_r)] * 2 + [jnp.cos(ang_c)] * 2, axis=-1)
    sin_lo = jnp.concatenate([-jnp.sin(ang_r), zero, -jnp.sin(ang_c), zero], axis=-1)
    sin_hi = jnp.concatenate([zero, jnp.sin(ang_r), zero, jnp.sin(ang_c)], axis=-1)
    return cos, sin_lo, sin_hi


def _pad_lanes(x):
    x = x.reshape(1, -1)
    return jnp.pad(x, ((0, 0), (0, LANES - x.shape[1])))


def _encode(x, p):
    L = x.shape[0]
    h1 = _ffn(x, p["ffn1_norm"], p["ffn1_wg"], p["ffn1_wu"], p["ffn1_wd"], p["final_norm"], final=False)
    proj = _norm_matmul(h1, p["mix_norm"], p["w_in"], GDN_PROJ_W)
    cos, sin_lo, sin_hi = _rope_tables(L)
    qa, ka, va, gb = _attn_proj(h1, p["mix_norm"], p["w_qk"], p["w_vt"], p["w_ab"], p["q_norm"], p["k_norm"],
                                cos, sin_lo, sin_hi, p["alog"], p["dtb"])
    qg, kg, vg, gcb, gct = _gdn_prep(proj, p["conv_w"], gb)
    og = _gdn_scan(qg, kg, vg, gcb, gct)
    oa = _flash_attn(qa, ka, va, p["attn_out_norm"])
    h2 = _out_proj(og, proj, oa, h1, p["gdn_out_norm"], p["w_out"])
    return _ffn(h2, p["ffn2_norm"], p["ffn2_wg"], p["ffn2_wu"], p["ffn2_wd"], p["final_norm"], final=True)


def _prepare_params(ffn1_norm, ffn1_w_gate, ffn1_w_up, ffn1_w_down, mix_norm, w_in, conv_w, a_log_fwd, a_log_bwd,
                    dt_bias_fwd, dt_bias_bwd, gdn_out_norm, q_norm, k_norm, attn_out_norm, w_out, ffn2_norm,
                    ffn2_w_gate, ffn2_w_up, ffn2_w_down, final_norm):
    w_in0 = w_in[0]
    att0 = GDN_PROJ_W + GATE_COLS
    w_ab = jnp.pad(w_in0[:, GDN_PROJ_W:att0], ((0, 0), (0, LANES - GATE_COLS)))
    return dict(
        ffn1_norm=ffn1_norm[0].reshape(1, -1),
        ffn1_wg=ffn1_w_gate[0].astype(BF16), ffn1_wu=ffn1_w_up[0].astype(BF16), ffn1_wd=ffn1_w_down[0].astype(BF16),
        mix_norm=mix_norm[0].reshape(1, -1),
        w_in=w_in0.astype(BF16),
        w_qk=w_in0[:, att0:att0 + ATT_Q_W + ATT_KV_W].astype(BF16),
        w_vt=w_in0[:, att0 + ATT_Q_W + ATT_KV_W:].T.astype(BF16),
        w_ab=w_ab.astype(BF16),
        conv_w=jnp.pad(conv_w[0], ((0, SUBLANES - CONV_K), (0, 0))),
        alog=_pad_lanes(jnp.concatenate([a_log_fwd[0], a_log_bwd[0]])),
        dtb=_pad_lanes(jnp.concatenate([dt_bias_fwd[0], dt_bias_bwd[0]])),
        gdn_out_norm=gdn_out_norm[0].reshape(1, -1),
        q_norm=q_norm[0].reshape(1, -1), k_norm=k_norm[0].reshape(1, -1),
        attn_out_norm=attn_out_norm[0].reshape(1, -1),
        w_out=w_out[0].astype(BF16),
        ffn2_norm=ffn2_norm[0].reshape(1, -1),
        ffn2_wg=ffn2_w_gate[0].astype(BF16), ffn2_wu=ffn2_w_up[0].astype(BF16), ffn2_wd=ffn2_w_down[0].astype(BF16),
        final_norm=final_norm.reshape(1, -1),
    )


def kernel(x_prompt, x_sample, ffn1_norm, ffn1_w_gate, ffn1_w_up, ffn1_w_down, mix_norm, w_in, conv_w, a_log_fwd,
           a_log_bwd, dt_bias_fwd, dt_bias_bwd, gdn_out_norm, q_norm, k_norm, attn_out_norm, w_out, ffn2_norm,
           ffn2_w_gate, ffn2_w_up, ffn2_w_down, final_norm):
    assert x_prompt.shape[0] == 1 and x_sample.shape[0] == 1
    p = _prepare_params(ffn1_norm, ffn1_w_gate, ffn1_w_up, ffn1_w_down, mix_norm, w_in, conv_w, a_log_fwd,
                        a_log_bwd, dt_bias_fwd, dt_bias_bwd, gdn_out_norm, q_norm, k_norm, attn_out_norm, w_out,
                        ffn2_norm, ffn2_w_gate, ffn2_w_up, ffn2_w_down, final_norm)
    y_prompt = _encode(x_prompt[0], p)
    y_sample = _encode(x_sample[0], p)
    return (y_prompt[None], y_sample[None])
```

```python
import functools
import math

import jax
import jax.numpy as jnp
from jax import lax
from jax.experimental import pallas as pl
from jax.experimental.pallas import tpu as pltpu

D_MODEL = 2048
HEAD_DIM = 128
GDN_HEADS = 8
GDN_W = GDN_HEADS * HEAD_DIM
ATT_Q_HEADS = 8
ATT_KV_HEADS = 2
ATT_GROUP = ATT_Q_HEADS // ATT_KV_HEADS
ATT_Q_W = ATT_Q_HEADS * HEAD_DIM
ATT_KV_W = ATT_KV_HEADS * HEAD_DIM
ATT_W = ATT_Q_W + 2 * ATT_KV_W
GDN_PROJ_W = 4 * GDN_W
GATE_COLS = 4 * GDN_HEADS
D_FF = 5632
CONV_K = 5
GRID_W = 64
AXIS_DIM = HEAD_DIM // 2
ROPE_THETA = 10000.0
EPS = 1e-6

LANES = 128
SUBLANES = 8
SCAN_CHUNK = 128
INV_BASE = 16
INV_BASE_LEVELS = 4
EXP2_SCALE = HEAD_DIM ** -0.5 * math.log2(math.e)

F32 = jnp.float32
BF16 = jnp.bfloat16

_VMEM_LIMIT = 56 * 1024 * 1024


def _cparams(semantics):
    return pltpu.CompilerParams(dimension_semantics=semantics, vmem_limit_bytes=_VMEM_LIMIT)


def _rms(x, g):
    return x * lax.rsqrt(jnp.mean(x * x, axis=-1, keepdims=True) + EPS) * g


def _dot(a, b):
    return jnp.dot(a, b, preferred_element_type=F32)


def _dot_nt(a, b):
    return lax.dot_general(a, b, (((1,), (1,)), ((), ())), preferred_element_type=F32)


def _split3(x):
    hi = x.astype(BF16)
    r = x - hi.astype(F32)
    mid = r.astype(BF16)
    lo = (r - mid.astype(F32)).astype(BF16)
    return hi, mid, lo


def _ffn_kernel(x_ref, g_ref, wg_ref, wu_ref, wd_ref, fg_ref, o_ref, xn_ref, *, final):
    j = pl.program_id(1)

    @pl.when(j == 0)
    def _():
        xn_ref[...] = _rms(x_ref[...], g_ref[...]).astype(BF16)
        o_ref[...] = jnp.zeros_like(o_ref)

    xn = xn_ref[...]
    gate = _dot(xn, wg_ref[...])
    up = _dot(xn, wu_ref[...])
    act = (gate * jax.nn.sigmoid(gate) * up).astype(BF16)
    o_ref[...] += _dot(act, wd_ref[...])

    @pl.when(j == pl.num_programs(1) - 1)
    def _():
        h = x_ref[...] + 0.5 * o_ref[...]
        if final:
            h = _rms(h, fg_ref[...])
        o_ref[...] = h


def _ffn(x, norm_g, wg, wu, wd, final_g, *, final, tm=512, tf=512):
    L = x.shape[0]
    grid = (L // tm, D_FF // tf)
    return pl.pallas_call(
        functools.partial(_ffn_kernel, final=final),
        grid=grid,
        in_specs=[
            pl.BlockSpec((tm, D_MODEL), lambda i, j: (i, 0)),
            pl.BlockSpec((1, D_MODEL), lambda i, j: (0, 0)),
            pl.BlockSpec((D_MODEL, tf), lambda i, j: (0, j)),
            pl.BlockSpec((D_MODEL, tf), lambda i, j: (0, j)),
            pl.BlockSpec((tf, D_MODEL), lambda i, j: (j, 0)),
            pl.BlockSpec((1, D_MODEL), lambda i, j: (0, 0)),
        ],
        out_specs=pl.BlockSpec((tm, D_MODEL), lambda i, j: (i, 0)),
        out_shape=jax.ShapeDtypeStruct((L, D_MODEL), F32),
        scratch_shapes=[pltpu.VMEM((tm, D_MODEL), BF16)],
        compiler_params=_cparams(("parallel", "arbitrary")),
        name="ffn",
    )(x, norm_g, wg, wu, wd, final_g)


def _norm_matmul_kernel(x_ref, g_ref, w_ref, o_ref, xn_ref):
    @pl.when(pl.program_id(1) == 0)
    def _():
        xn_ref[...] = _rms(x_ref[...], g_ref[...]).astype(BF16)

    o_ref[...] = _dot(xn_ref[...], w_ref[...])


def _norm_matmul(x, norm_g, w, n_cols, *, tm=512, tn=1024):
    L = x.shape[0]
    grid = (L // tm, n_cols // tn)
    return pl.pallas_call(
        _norm_matmul_kernel,
        grid=grid,
        in_specs=[
            pl.BlockSpec((tm, D_MODEL), lambda i, j: (i, 0)),
            pl.BlockSpec((1, D_MODEL), lambda i, j: (0, 0)),
            pl.BlockSpec((D_MODEL, tn), lambda i, j: (0, j)),
        ],
        out_specs=pl.BlockSpec((tm, tn), lambda i, j: (i, j)),
        out_shape=jax.ShapeDtypeStruct((L, n_cols), F32),
        scratch_shapes=[pltpu.VMEM((tm, D_MODEL), BF16)],
        compiler_params=_cparams(("parallel", "arbitrary")),
        name="norm_matmul",
    )(x, norm_g, w)


def _rope(x, cos, sin_lo, sin_hi):
    return (x * cos + pltpu.roll(x, AXIS_DIM // 2, axis=1) * sin_hi
            + pltpu.roll(x, HEAD_DIM - AXIS_DIM // 2, axis=1) * sin_lo)


def _attn_proj_kernel(x_ref, g_ref, w_ref, wvt_ref, wab_ref, qn_ref, kn_ref, cos_ref, slo_ref, shi_ref,
                      alog_ref, dtb_ref, q_ref, k_ref, vt_ref, gb_ref):
    xn = _rms(x_ref[...], g_ref[...]).astype(BF16)
    cos = cos_ref[...]
    slo = slo_ref[...]
    shi = shi_ref[...]
    for h in range(ATT_Q_HEADS):
        cols = slice(h * HEAD_DIM, (h + 1) * HEAD_DIM)
        qh = _rms(_dot(xn, w_ref[:, cols]), qn_ref[...])
        q_ref[:, cols] = _rope(qh, cos, slo, shi).astype(BF16)
    for h in range(ATT_KV_HEADS):
        cols = slice(h * HEAD_DIM, (h + 1) * HEAD_DIM)
        kh = _rms(_dot(xn, w_ref[:, ATT_Q_W + h * HEAD_DIM:ATT_Q_W + (h + 1) * HEAD_DIM]), kn_ref[...])
        k_ref[:, cols] = _rope(kh, cos, slo, shi).astype(BF16)
    vt_ref[...] = _dot_nt(wvt_ref[...], xn).astype(BF16)

    ab = _dot(xn, wab_ref[...])
    t = ab + dtb_ref[...]
    softplus = jnp.maximum(t, 0.0) + jnp.log1p(jnp.exp(-jnp.abs(t)))
    g = -jnp.exp(alog_ref[...]) * softplus
    lane = lax.broadcasted_iota(jnp.int32, ab.shape, 1)
    gb_ref[...] = jnp.where(lane < 2 * GDN_HEADS, g, jax.nn.sigmoid(ab))


def _attn_proj(x, norm_g, w_qk, w_vt, w_ab, q_norm, k_norm, cos, sin_lo, sin_hi, alog, dtb, *, tm=256):
    L = x.shape[0]
    row = lambda i: (i, 0)
    fixed = lambda i: (0, 0)
    return pl.pallas_call(
        _attn_proj_kernel,
        grid=(L // tm,),
        in_specs=[
            pl.BlockSpec((tm, D_MODEL), row),
            pl.BlockSpec((1, D_MODEL), fixed),
            pl.BlockSpec((D_MODEL, ATT_Q_W + ATT_KV_W), fixed),
            pl.BlockSpec((ATT_KV_W, D_MODEL), fixed),
            pl.BlockSpec((D_MODEL, LANES), fixed),
            pl.BlockSpec((1, HEAD_DIM), fixed),
            pl.BlockSpec((1, HEAD_DIM), fixed),
            pl.BlockSpec((tm, HEAD_DIM), row),
            pl.BlockSpec((tm, HEAD_DIM), row),
            pl.BlockSpec((tm, HEAD_DIM), row),
            pl.BlockSpec((1, LANES), fixed),
            pl.BlockSpec((1, LANES), fixed),
        ],
        out_specs=[
            pl.BlockSpec((tm, ATT_Q_W), row),
            pl.BlockSpec((tm, ATT_KV_W), row),
            pl.BlockSpec((ATT_KV_W, tm), lambda i: (0, i)),
            pl.BlockSpec((tm, LANES), row),
        ],
        out_shape=[
            jax.ShapeDtypeStruct((L, ATT_Q_W), BF16),
            jax.ShapeDtypeStruct((L, ATT_KV_W), BF16),
            jax.ShapeDtypeStruct((ATT_KV_W, L), BF16),
            jax.ShapeDtypeStruct((L, LANES), F32),
        ],
        compiler_params=_cparams(("parallel",)),
        name="attn_proj",
    )(x, norm_g, w_qk, w_vt, w_ab, q_norm, k_norm, cos, sin_lo, sin_hi, alog, dtb)


def _gdn_prep_kernel(x_ref, prev_ref, next_ref, cw_ref, gb_ref, q_ref, k_ref, v_ref, gcb_ref, gct_ref,
                     ext_ref, *, tm):
    i = pl.program_id(0)
    pad = CONV_K // 2
    ext_ref[0:SUBLANES, :] = jnp.where(i == 0, 0.0, prev_ref[...])
    ext_ref[SUBLANES:SUBLANES + tm, :] = x_ref[...]
    ext_ref[SUBLANES + tm:, :] = jnp.where(i == pl.num_programs(0) - 1, 0.0, next_ref[...])

    for s in range(3 * GDN_HEADS):
        cols = slice(s * HEAD_DIM, (s + 1) * HEAD_DIM)
        y = None
        for t in range(CONV_K):
            term = ext_ref[SUBLANES - pad + t:SUBLANES - pad + t + tm, cols] * cw_ref[t:t + 1, cols]
            y = term if y is None else y + term
        y = y * jax.nn.sigmoid(y)
        part, h = divmod(s, GDN_HEADS)
        hc = slice(h * HEAD_DIM, (h + 1) * HEAD_DIM)
        if part == 0:
            q_ref[:, hc] = y * lax.rsqrt(jnp.sum(y * y, axis=-1, keepdims=True) + EPS) * (HEAD_DIM ** -0.5)
        elif part == 1:
            k_ref[:, hc] = y * lax.rsqrt(jnp.sum(y * y, axis=-1, keepdims=True) + EPS)
        else:
            v_ref[:, hc] = y

    gb = gb_ref[...]
    r = lax.broadcasted_iota(jnp.int32, (tm, tm), 0)
    c = lax.broadcasted_iota(jnp.int32, (tm, tm), 1)
    same = (r // SCAN_CHUNK) == (c // SCAN_CHUNK)
    m_lo = jnp.where(same & (c <= r), 1.0, 0.0).astype(BF16)
    m_up = jnp.where(same & (c >= r), 1.0, 0.0).astype(BF16)
    parts = _split3(gb)
    gc_f = _dot(m_lo, parts[0]) + _dot(m_lo, parts[1]) + _dot(m_lo, parts[2])
    gc_b = _dot(m_up, parts[0]) + _dot(m_up, parts[1]) + _dot(m_up, parts[2])
    lane = lax.broadcasted_iota(jnp.int32, gb.shape, 1)
    gcb = jnp.where(lane < GDN_HEADS, gc_f, jnp.where(lane < 2 * GDN_HEADS, gc_b, gb))
    gcb_ref[...] = gcb
    for n in range(tm // SCAN_CHUNK):
        gct = gcb[n * SCAN_CHUNK:(n + 1) * SCAN_CHUNK, :].T
        gct_ref[n * 2 * GDN_HEADS:(n + 1) * 2 * GDN_HEADS, :] = gct[0:2 * GDN_HEADS, :]


def _gdn_prep(proj, conv_w8, gb, *, tm=256):
    L = proj.shape[0]
    n_sub = L // SUBLANES
    per = tm // SUBLANES
    row = lambda i: (i, 0)
    return pl.pallas_call(
        functools.partial(_gdn_prep_kernel, tm=tm),
        grid=(L // tm,),
        in_specs=[
            pl.BlockSpec((tm, 3 * GDN_W), row),
            pl.BlockSpec((SUBLANES, 3 * GDN_W), lambda i: (jnp.maximum(i * per - 1, 0), 0)),
            pl.BlockSpec((SUBLANES, 3 * GDN_W), lambda i: (jnp.minimum((i + 1) * per, n_sub - 1), 0)),
            pl.BlockSpec((SUBLANES, 3 * GDN_W), lambda i: (0, 0)),
            pl.BlockSpec((tm, LANES), row),
        ],
        out_specs=[
            pl.BlockSpec((tm, GDN_W), row),
            pl.BlockSpec((tm, GDN_W), row),
            pl.BlockSpec((tm, GDN_W), row),
            pl.BlockSpec((tm, LANES), row),
            pl.BlockSpec((tm // SCAN_CHUNK * 2 * GDN_HEADS, SCAN_CHUNK), row),
        ],
        out_shape=[
            jax.ShapeDtypeStruct((L, GDN_W), F32),
            jax.ShapeDtypeStruct((L, GDN_W), F32),
            jax.ShapeDtypeStruct((L, GDN_W), F32),
            jax.ShapeDtypeStruct((L, LANES), F32),
            jax.ShapeDtypeStruct((L // SCAN_CHUNK * 2 * GDN_HEADS, SCAN_CHUNK), F32),
        ],
        scratch_shapes=[pltpu.VMEM((tm + 2 * SUBLANES, 3 * GDN_W), F32)],
        compiler_params=_cparams(("parallel",)),
        name="gdn_prep",
    )(proj, proj, proj, conv_w8, gb)


def _gdn_scan_kernel(q_ref, k_ref, v_ref, gcb_ref, gct_ref, o_ref, s_ref):
    d = pl.program_id(0)
    C = SCAN_CHUNK

    @pl.when(pl.program_id(1) == 0)
    def _():
        s_ref[...] = jnp.zeros_like(s_ref)

    fwd = d == 0
    r = lax.broadcasted_iota(jnp.int32, (C, C), 0)
    c = lax.broadcasted_iota(jnp.int32, (C, C), 1)
    dif = jnp.where(fwd, r - c, c - r)
    m_incl = dif >= 0
    m_strict = dif > 0
    eye = jnp.where(dif == 0, 1.0, 0.0)

    gcb = gcb_ref[...]
    gc = jnp.where(fwd, gcb[:, 0:GDN_HEADS], gcb[:, GDN_HEADS:2 * GDN_HEADS])
    beta = jnp.where(fwd, gcb[:, 2 * GDN_HEADS:3 * GDN_HEADS], gcb[:, 3 * GDN_HEADS:4 * GDN_HEADS])
    g_end = jnp.where(fwd, gc[C - 1:C, :], gc[0:1, :])
    e_gc = jnp.exp(gc)
    e_rest = jnp.exp(g_end - gc)
    e_end = jnp.exp(g_end)
    gct = gct_ref[...]

    heads = range(GDN_HEADS)
    cols = [slice(h * HEAD_DIM, (h + 1) * HEAD_DIM) for h in heads]
    col1 = [slice(h, h + 1) for h in heads]

    k = [k_ref[:, cols[h]] for h in heads]
    k16 = [k[h].astype(BF16) for h in heads]
    kb = [k[h] * beta[:, col1[h]] for h in heads]
    q16 = [q_ref[:, cols[h]].astype(BF16) for h in heads]
    kq = [_dot_nt(jnp.concatenate([kb[h].astype(BF16), q16[h]], axis=0), k16[h]) for h in heads]
    decay = []
    for h in heads:
        diff = gc[:, col1[h]] - gct[col1[h], :]
        decay.append(jnp.where(m_incl, jnp.exp(jnp.where(m_incl, diff, 0.0)), 0.0))
    a = [jnp.where(m_strict, kq[h][0:C] * decay[h], 0.0) for h in heads]
    attn_qk16 = [(kq[h][C:2 * C] * decay[h]).astype(BF16) for h in heads]

    rb = r >> INV_BASE_LEVELS
    cb = c >> INV_BASE_LEVELS
    a_d = [jnp.where(rb == cb, a[h], 0.0) for h in heads]
    inv = [eye - a_d[h] for h in heads]
    a_d16 = [a_d[h].astype(BF16) for h in heads]
    a_pow16 = [_dot(a_d16[h], a_d16[h]).astype(BF16) for h in heads]
    for level in range(INV_BASE_LEVELS - 1):
        if level < INV_BASE_LEVELS - 2:
            both = [_dot(jnp.concatenate([inv[h].astype(BF16), a_pow16[h]], axis=0), a_pow16[h]) for h in heads]
            inv = [inv[h] + both[h][0:C] for h in heads]
            a_pow16 = [both[h][C:2 * C].astype(BF16) for h in heads]
        else:
            inv = [inv[h] + _dot(inv[h].astype(BF16), a_pow16[h]) for h in heads]
    b = INV_BASE
    while b < C:
        off = ((rb >> 1) == (cb >> 1)) & (rb != cb)
        a_off16 = [jnp.where(off, a[h], 0.0).astype(BF16) for h in heads]
        inv16 = [inv[h].astype(BF16) for h in heads]
        left = [_dot(inv16[h], a_off16[h]).astype(BF16) for h in heads]
        inv = [inv[h] - _dot(left[h], inv16[h]) for h in heads]
        rb = rb >> 1
        cb = cb >> 1
        b *= 2

    rhs16 = [jnp.concatenate([(v_ref[:, cols[h]] * beta[:, col1[h]]).astype(BF16),
                              (kb[h] * e_gc[:, col1[h]]).astype(BF16)], axis=1) for h in heads]
    uw = [_dot(inv[h].astype(BF16), rhs16[h]) for h in heads]
    q_dec16 = [(q_ref[:, cols[h]] * e_gc[:, col1[h]]).astype(BF16) for h in heads]
    k_dec_t16 = [(k[h] * e_rest[:, col1[h]]).T.astype(BF16) for h in heads]

    s = [s_ref[h] for h in heads]
    s16 = [s[h].astype(BF16) for h in heads]
    ws = [_dot(jnp.concatenate([uw[h][:, HEAD_DIM:].astype(BF16), q_dec16[h]], axis=0), s16[h]) for h in heads]
    v_new16 = [(uw[h][:, 0:HEAD_DIM] - ws[h][0:C]).astype(BF16) for h in heads]
    for h in heads:
        o_ref[:, cols[h]] = ws[h][C:2 * C] + _dot(attn_qk16[h], v_new16[h])
    for h in heads:
        s_ref[h] = s[h] * e_end[:, col1[h]] + _dot(k_dec_t16[h], v_new16[h])


def _gdn_scan(q, k, v, gcb, gct):
    L = q.shape[0]
    n = L // SCAN_CHUNK
    chunk = lambda d, i: i + d * (n - 1 - 2 * i)
    row = lambda d, i: (chunk(d, i), 0)
    return pl.pallas_call(
        _gdn_scan_kernel,
        grid=(2, n),
        in_specs=[
            pl.BlockSpec((SCAN_CHUNK, GDN_W), row),
            pl.BlockSpec((SCAN_CHUNK, GDN_W), row),
            pl.BlockSpec((SCAN_CHUNK, GDN_W), row),
            pl.BlockSpec((SCAN_CHUNK, LANES), row),
            pl.BlockSpec((GDN_HEADS, SCAN_CHUNK), lambda d, i: (2 * chunk(d, i) + d, 0)),
        ],
        out_specs=pl.BlockSpec((None, SCAN_CHUNK, GDN_W), lambda d, i: (d, chunk(d, i), 0)),
        out_shape=jax.ShapeDtypeStruct((2, L, GDN_W), F32),
        scratch_shapes=[pltpu.VMEM((GDN_HEADS, HEAD_DIM, HEAD_DIM), F32)],
        compiler_params=_cparams(("arbitrary", "arbitrary")),
        name="gdn_scan",
    )(q, k, v, gcb, gct)


def _flash_kernel(q_ref, k_ref, vt_ref, on_ref, o_ref, m_ref, l_ref, acc_ref):
    ki = pl.program_id(2)

    @pl.when(ki == 0)
    def _():
        m_ref[...] = jnp.full_like(m_ref, -jnp.inf)
        l_ref[...] = jnp.zeros_like(l_ref)
        acc_ref[...] = jnp.zeros_like(acc_ref)

    k = k_ref[...]
    vt = vt_ref[...]
    heads = range(ATT_GROUP)
    st = [_dot_nt(k, q_ref[:, h * HEAD_DIM:(h + 1) * HEAD_DIM]) for h in heads]
    m_prev = [m_ref[h] for h in heads]
    m_new = [jnp.maximum(m_prev[h], jnp.max(st[h], axis=0, keepdims=True)) for h in heads]
    alpha = [jnp.exp2((m_prev[h] - m_new[h]) * EXP2_SCALE) for h in heads]
    p = [jnp.exp2((st[h] - m_new[h]) * EXP2_SCALE) for h in heads]
    for h in heads:
        l_ref[h] = alpha[h] * l_ref[h] + jnp.sum(p[h], axis=0, keepdims=True)
        m_ref[h] = m_new[h]
    pv = [_dot(vt, p[h].astype(BF16)) for h in heads]
    for h in heads:
        acc_ref[h] = alpha[h] * acc_ref[h] + pv[h]

    @pl.when(ki == pl.num_programs(2) - 1)
    def _():
        for h in range(ATT_GROUP):
            cols = slice(h * HEAD_DIM, (h + 1) * HEAD_DIM)
            o = (acc_ref[h] / l_ref[h]).T
            o_ref[:, cols] = _rms(o, on_ref[...]).astype(BF16)


def _flash_attn(q, k, vt, out_norm, *, tq=512, tk=1024):
    L = q.shape[0]
    gw = ATT_GROUP * HEAD_DIM
    return pl.pallas_call(
        _flash_kernel,
        grid=(ATT_KV_HEADS, L // tq, L // tk),
        in_specs=[
            pl.BlockSpec((tq, gw), lambda g, i, j: (i, g)),
            pl.BlockSpec((tk, HEAD_DIM), lambda g, i, j: (j, g)),
            pl.BlockSpec((HEAD_DIM, tk), lambda g, i, j: (g, j)),
            pl.BlockSpec((1, HEAD_DIM), lambda g, i, j: (0, 0)),
        ],
        out_specs=pl.BlockSpec((tq, gw), lambda g, i, j: (i, g)),
        out_shape=jax.ShapeDtypeStruct((L, ATT_Q_W), BF16),
        scratch_shapes=[
            pltpu.VMEM((ATT_GROUP, 1, tq), F32),
            pltpu.VMEM((ATT_GROUP, 1, tq), F32),
            pltpu.VMEM((ATT_GROUP, HEAD_DIM, tq), F32),
        ],
        compiler_params=_cparams(("parallel", "parallel", "arbitrary")),
        name="flash_attn",
    )(q, k, vt, out_norm)


def _out_proj_kernel(og_ref, z_ref, oa_ref, h_ref, gn_ref, w_ref, o_ref, mix_ref):
    for h in range(GDN_HEADS):
        cols = slice(h * HEAD_DIM, (h + 1) * HEAD_DIM)
        o = _rms(og_ref[0, :, cols] + og_ref[1, :, cols], gn_ref[...])
        z = z_ref[:, cols]
        mix_ref[:, cols] = (o * (z * jax.nn.sigmoid(z))).astype(BF16)
    mix_ref[:, GDN_W:] = oa_ref[...]
    o_ref[...] = h_ref[...] + _dot(mix_ref[...], w_ref[...])


def _out_proj(og, proj, oa, h1, gdn_norm, w_out, *, tm=256):
    L = h1.shape[0]
    row = lambda i: (i, 0)
    return pl.pallas_call(
        _out_proj_kernel,
        grid=(L // tm,),
        in_specs=[
            pl.BlockSpec((2, tm, GDN_W), lambda i: (0, i, 0)),
            pl.BlockSpec((tm, GDN_W), lambda i: (i, 3)),
            pl.BlockSpec((tm, ATT_Q_W), row),
            pl.BlockSpec((tm, D_MODEL), row),
            pl.BlockSpec((1, HEAD_DIM), lambda i: (0, 0)),
            pl.BlockSpec((GDN_W + ATT_Q_W, D_MODEL), lambda i: (0, 0)),
        ],
        out_specs=pl.BlockSpec((tm, D_MODEL), row),
        out_shape=jax.ShapeDtypeStruct((L, D_MODEL), F32),
        scratch_shapes=[pltpu.VMEM((tm, GDN_W + ATT_Q_W), BF16)],
        compiler_params=_cparams(("parallel",)),
        name="out_proj",
    )(og, proj, oa, h1, gdn_norm, w_out)


def _rope_tables(L):
    t = jnp.arange(L, dtype=jnp.int32)
    row = (t // GRID_W).astype(F32)
    col = (t % GRID_W).astype(F32)
    freqs = ROPE_THETA ** (-jnp.arange(0, AXIS_DIM, 2, dtype=F32) / AXIS_DIM)
    ang_r = row[:, None] * freqs[None, :]
    ang_c = col[:, None] * freqs[None, :]
    zero = jnp.zeros_like(ang_r)
    cos = jnp.concatenate([jnp.cos(ang_r)] * 2 + [jnp.cos(ang_c)] * 2, axis=-1)
    sin_lo = jnp.concatenate([-jnp.sin(ang_r), zero, -jnp.sin(ang_c), zero], axis=-1)
    sin_hi = jnp.concatenate([zero, jnp.sin(ang_r), zero, jnp.sin(ang_c)], axis=-1)
    return cos, sin_lo, sin_hi


def _pad_lanes(x):
    x = x.reshape(1, -1)
    return jnp.pad(x, ((0, 0), (0, LANES - x.shape[1])))


def _encode(x, p):
    L = x.shape[0]
    h1 = _ffn(x, p["ffn1_norm"], p["ffn1_wg"], p["ffn1_wu"], p["ffn1_wd"], p["final_norm"], final=False)
    proj = _norm_matmul(h1, p["mix_norm"], p["w_in"], GDN_PROJ_W)
    cos, sin_lo, sin_hi = _rope_tables(L)
    qa, ka, va, gb = _attn_proj(h1, p["mix_norm"], p["w_qk"], p["w_vt"], p["w_ab"], p["q_norm"], p["k_norm"],
                                cos, sin_lo, sin_hi, p["alog"], p["dtb"])
    qg, kg, vg, gcb, gct = _gdn_prep(proj, p["conv_w"], gb)
    og = _gdn_scan(qg, kg, vg, gcb, gct)
    oa = _flash_attn(qa, ka, va, p["attn_out_norm"])
    h2 = _out_proj(og, proj, oa, h1, p["gdn_out_norm"], p["w_out"])
    return _ffn(h2, p["ffn2_norm"], p["ffn2_wg"], p["ffn2_wu"], p["ffn2_wd"], p["final_norm"], final=True)


def _prepare_params(ffn1_norm, ffn1_w_gate, ffn1_w_up, ffn1_w_down, mix_norm, w_in, conv_w, a_log_fwd, a_log_bwd,
                    dt_bias_fwd, dt_bias_bwd, gdn_out_norm, q_norm, k_norm, attn_out_norm, w_out, ffn2_norm,
                    ffn2_w_gate, ffn2_w_up, ffn2_w_down, final_norm):
    w_in0 = w_in[0]
    att0 = GDN_PROJ_W + GATE_COLS
    w_ab = jnp.pad(w_in0[:, GDN_PROJ_W:att0], ((0, 0), (0, LANES - GATE_COLS)))
    return dict(
        ffn1_norm=ffn1_norm[0].reshape(1, -1),
        ffn1_wg=ffn1_w_gate[0].astype(BF16), ffn1_wu=ffn1_w_up[0].astype(BF16), ffn1_wd=ffn1_w_down[0].astype(BF16),
        mix_norm=mix_norm[0].reshape(1, -1),
        w_in=w_in0.astype(BF16),
        w_qk=w_in0[:, att0:att0 + ATT_Q_W + ATT_KV_W].astype(BF16),
        w_vt=w_in0[:, att0 + ATT_Q_W + ATT_KV_W:].T.astype(BF16),
        w_ab=w_ab.astype(BF16),
        conv_w=jnp.pad(conv_w[0], ((0, SUBLANES - CONV_K), (0, 0))),
        alog=_pad_lanes(jnp.concatenate([a_log_fwd[0], a_log_bwd[0]])),
        dtb=_pad_lanes(jnp.concatenate([dt_bias_fwd[0], dt_bias_bwd[0]])),
        gdn_out_norm=gdn_out_norm[0].reshape(1, -1),
        q_norm=q_norm[0].reshape(1, -1), k_norm=k_norm[0].reshape(1, -1),
        attn_out_norm=attn_out_norm[0].reshape(1, -1),
        w_out=w_out[0].astype(BF16),
        ffn2_norm=ffn2_norm[0].reshape(1, -1),
        ffn2_wg=ffn2_w_gate[0].astype(BF16), ffn2_wu=ffn2_w_up[0].astype(BF16), ffn2_wd=ffn2_w_down[0].astype(BF16),
        final_norm=final_norm.reshape(1, -1),
    )


def kernel(x_prompt, x_sample, ffn1_norm, ffn1_w_gate, ffn1_w_up, ffn1_w_down, mix_norm, w_in, conv_w, a_log_fwd,
           a_log_bwd, dt_bias_fwd, dt_bias_bwd, gdn_out_norm, q_norm, k_norm, attn_out_norm, w_out, ffn2_norm,
           ffn2_w_gate, ffn2_w_up, ffn2_w_down, final_norm):
    assert x_prompt.shape[0] == 1 and x_sample.shape[0] == 1
    p = _prepare_params(ffn1_norm, ffn1_w_gate, ffn1_w_up, ffn1_w_down, mix_norm, w_in, conv_w, a_log_fwd,
                        a_log_bwd, dt_bias_fwd, dt_bias_bwd, gdn_out_norm, q_norm, k_norm, attn_out_norm, w_out,
                        ffn2_norm, ffn2_w_gate, ffn2_w_up, ffn2_w_down, final_norm)
    y_prompt = _encode(x_prompt[0], p)
    y_sample = _encode(x_sample[0], p)
    return (y_prompt[None], y_sample[None])
```

```python
import functools
import math

import jax
import jax.numpy as jnp
from jax import lax
from jax.experimental import pallas as pl
from jax.experimental.pallas import tpu as pltpu

D_MODEL = 2048
HEAD_DIM = 128
GDN_HEADS = 8
GDN_W = GDN_HEADS * HEAD_DIM
ATT_Q_HEADS = 8
ATT_KV_HEADS = 2
ATT_GROUP = ATT_Q_HEADS // ATT_KV_HEADS
ATT_Q_W = ATT_Q_HEADS * HEAD_DIM
ATT_KV_W = ATT_KV_HEADS * HEAD_DIM
ATT_W = ATT_Q_W + 2 * ATT_KV_W
GDN_PROJ_W = 4 * GDN_W
GATE_COLS = 4 * GDN_HEADS
D_FF = 5632
CONV_K = 5
GRID_W = 64
AXIS_DIM = HEAD_DIM // 2
ROPE_THETA = 10000.0
EPS = 1e-6

LANES = 128
SUBLANES = 8
SCAN_CHUNK = 128
INV_BASE = 16
INV_BASE_LEVELS = 4
EXP2_SCALE = HEAD_DIM ** -0.5 * math.log2(math.e)
FLASH_ROW_BLOCK = 64
FLASH_SUM_ROWS = 16

F32 = jnp.float32
BF16 = jnp.bfloat16

_VMEM_LIMIT = 56 * 1024 * 1024


def _cparams(semantics):
    return pltpu.CompilerParams(dimension_semantics=semantics, vmem_limit_bytes=_VMEM_LIMIT)


def _rms(x, g):
    return x * lax.rsqrt(jnp.mean(x * x, axis=-1, keepdims=True) + EPS) * g


def _dot(a, b):
    return jnp.dot(a, b, preferred_element_type=F32)


def _dot_nt(a, b):
    return lax.dot_general(a, b, (((1,), (1,)), ((), ())), preferred_element_type=F32)


def _split3(x):
    hi = x.astype(BF16)
    r = x - hi.astype(F32)
    mid = r.astype(BF16)
    lo = (r - mid.astype(F32)).astype(BF16)
    return hi, mid, lo


def _ffn_kernel(x_ref, g_ref, wg_ref, wu_ref, wd_ref, fg_ref, o_ref, xn_ref, *, final):
    j = pl.program_id(1)

    @pl.when(j == 0)
    def _():
        xn_ref[...] = _rms(x_ref[...], g_ref[...]).astype(BF16)
        o_ref[...] = jnp.zeros_like(o_ref)

    xn = xn_ref[...]
    gate = _dot(xn, wg_ref[...])
    up = _dot(xn, wu_ref[...])
    act = (gate * jax.nn.sigmoid(gate) * up).astype(BF16)
    o_ref[...] += _dot(act, wd_ref[...])

    @pl.when(j == pl.num_programs(1) - 1)
    def _():
        h = x_ref[...] + 0.5 * o_ref[...]
        if final:
            h = _rms(h, fg_ref[...])
        o_ref[...] = h


def _ffn(x, norm_g, wg, wu, wd, final_g, *, final, tm=512, tf=512):
    L = x.shape[0]
    grid = (L // tm, D_FF // tf)
    return pl.pallas_call(
        functools.partial(_ffn_kernel, final=final),
        grid=grid,
        in_specs=[
            pl.BlockSpec((tm, D_MODEL), lambda i, j: (i, 0)),
            pl.BlockSpec((1, D_MODEL), lambda i, j: (0, 0)),
            pl.BlockSpec((D_MODEL, tf), lambda i, j: (0, j)),
            pl.BlockSpec((D_MODEL, tf), lambda i, j: (0, j)),
            pl.BlockSpec((tf, D_MODEL), lambda i, j: (j, 0)),
            pl.BlockSpec((1, D_MODEL), lambda i, j: (0, 0)),
        ],
        out_specs=pl.BlockSpec((tm, D_MODEL), lambda i, j: (i, 0)),
        out_shape=jax.ShapeDtypeStruct((L, D_MODEL), F32),
        scratch_shapes=[pltpu.VMEM((tm, D_MODEL), BF16)],
        compiler_params=_cparams(("parallel", "arbitrary")),
        name="ffn",
    )(x, norm_g, wg, wu, wd, final_g)


def _norm_matmul_kernel(x_ref, g_ref, w_ref, o_ref, xn_ref):
    @pl.when(pl.program_id(1) == 0)
    def _():
        xn_ref[...] = _rms(x_ref[...], g_ref[...]).astype(BF16)

    o_ref[...] = _dot(xn_ref[...], w_ref[...])


def _norm_matmul(x, norm_g, w, n_cols, *, tm=512, tn=1024):
    L = x.shape[0]
    grid = (L // tm, n_cols // tn)
    return pl.pallas_call(
        _norm_matmul_kernel,
        grid=grid,
        in_specs=[
            pl.BlockSpec((tm, D_MODEL), lambda i, j: (i, 0)),
            pl.BlockSpec((1, D_MODEL), lambda i, j: (0, 0)),
            pl.BlockSpec((D_MODEL, tn), lambda i, j: (0, j)),
        ],
        out_specs=pl.BlockSpec((tm, tn), lambda i, j: (i, j)),
        out_shape=jax.ShapeDtypeStruct((L, n_cols), F32),
        scratch_shapes=[pltpu.VMEM((tm, D_MODEL), BF16)],
        compiler_params=_cparams(("parallel", "arbitrary")),
        name="norm_matmul",
    )(x, norm_g, w)


def _rope(x, cos, sin_lo, sin_hi):
    return (x * cos + pltpu.roll(x, AXIS_DIM // 2, axis=1) * sin_hi
            + pltpu.roll(x, HEAD_DIM - AXIS_DIM // 2, axis=1) * sin_lo)


def _attn_proj_kernel(x_ref, g_ref, w_ref, wvt_ref, wab_ref, qn_ref, kn_ref, cos_ref, slo_ref, shi_ref,
                      alog_ref, dtb_ref, q_ref, k_ref, vt_ref, gb_ref):
    xn = _rms(x_ref[...], g_ref[...]).astype(BF16)
    cos = cos_ref[...]
    slo = slo_ref[...]
    shi = shi_ref[...]
    for h in range(ATT_Q_HEADS):
        cols = slice(h * HEAD_DIM, (h + 1) * HEAD_DIM)
        qh = _rms(_dot(xn, w_ref[:, cols]), qn_ref[...])
        q_ref[:, cols] = (_rope(qh, cos, slo, shi) * EXP2_SCALE).astype(BF16)
    for h in range(ATT_KV_HEADS):
        cols = slice(h * HEAD_DIM, (h + 1) * HEAD_DIM)
        kh = _rms(_dot(xn, w_ref[:, ATT_Q_W + h * HEAD_DIM:ATT_Q_W + (h + 1) * HEAD_DIM]), kn_ref[...])
        k_ref[:, cols] = _rope(kh, cos, slo, shi).astype(BF16)
    vt_ref[...] = _dot_nt(wvt_ref[...], xn).astype(BF16)

    ab = _dot(xn, wab_ref[...])
    t = ab + dtb_ref[...]
    softplus = jnp.maximum(t, 0.0) + jnp.log1p(jnp.exp(-jnp.abs(t)))
    g = -jnp.exp(alog_ref[...]) * softplus
    lane = lax.broadcasted_iota(jnp.int32, ab.shape, 1)
    gb_ref[...] = jnp.where(lane < 2 * GDN_HEADS, g, jax.nn.sigmoid(ab))


def _attn_proj(x, norm_g, w_qk, w_vt, w_ab, q_norm, k_norm, cos, sin_lo, sin_hi, alog, dtb, *, tm=256):
    L = x.shape[0]
    row = lambda i: (i, 0)
    fixed = lambda i: (0, 0)
    return pl.pallas_call(
        _attn_proj_kernel,
        grid=(L // tm,),
        in_specs=[
            pl.BlockSpec((tm, D_MODEL), row),
            pl.BlockSpec((1, D_MODEL), fixed),
            pl.BlockSpec((D_MODEL, ATT_Q_W + ATT_KV_W), fixed),
            pl.BlockSpec((ATT_KV_W, D_MODEL), fixed),
            pl.BlockSpec((D_MODEL, LANES), fixed),
            pl.BlockSpec((1, HEAD_DIM), fixed),
            pl.BlockSpec((1, HEAD_DIM), fixed),
            pl.BlockSpec((tm, HEAD_DIM), row),
            pl.BlockSpec((tm, HEAD_DIM), row),
            pl.BlockSpec((tm, HEAD_DIM), row),
            pl.BlockSpec((1, LANES), fixed),
            pl.BlockSpec((1, LANES), fixed),
        ],
        out_specs=[
            pl.BlockSpec((tm, ATT_Q_W), row),
            pl.BlockSpec((tm, ATT_KV_W), row),
            pl.BlockSpec((ATT_KV_W, tm), lambda i: (0, i)),
            pl.BlockSpec((tm, LANES), row),
        ],
        out_shape=[
            jax.ShapeDtypeStruct((L, ATT_Q_W), BF16),
            jax.ShapeDtypeStruct((L, ATT_KV_W), BF16),
            jax.ShapeDtypeStruct((ATT_KV_W, L), BF16),
            jax.ShapeDtypeStruct((L, LANES), F32),
        ],
        compiler_params=_cparams(("parallel",)),
        name="attn_proj",
    )(x, norm_g, w_qk, w_vt, w_ab, q_norm, k_norm, cos, sin_lo, sin_hi, alog, dtb)


def _gdn_prep_kernel(x_ref, prev_ref, next_ref, cw_ref, gb_ref, q_ref, k_ref, v_ref, gcb_ref, gct_ref,
                     ext_ref, *, tm):
    i = pl.program_id(0)
    pad = CONV_K // 2
    ext_ref[0:SUBLANES, :] = jnp.where(i == 0, 0.0, prev_ref[...])
    ext_ref[SUBLANES:SUBLANES + tm, :] = x_ref[...]
    ext_ref[SUBLANES + tm:, :] = jnp.where(i == pl.num_programs(0) - 1, 0.0, next_ref[...])

    for s in range(3 * GDN_HEADS):
        cols = slice(s * HEAD_DIM, (s + 1) * HEAD_DIM)
        y = None
        for t in range(CONV_K):
            term = ext_ref[SUBLANES - pad + t:SUBLANES - pad + t + tm, cols] * cw_ref[t:t + 1, cols]
            y = term if y is None else y + term
        y = y * jax.nn.sigmoid(y)
        part, h = divmod(s, GDN_HEADS)
        hc = slice(h * HEAD_DIM, (h + 1) * HEAD_DIM)
        if part == 0:
            q_ref[:, hc] = y * lax.rsqrt(jnp.sum(y * y, axis=-1, keepdims=True) + EPS) * (HEAD_DIM ** -0.5)
        elif part == 1:
            k_ref[:, hc] = y * lax.rsqrt(jnp.sum(y * y, axis=-1, keepdims=True) + EPS)
        else:
            v_ref[:, hc] = y

    gb = gb_ref[...]
    r = lax.broadcasted_iota(jnp.int32, (tm, tm), 0)
    c = lax.broadcasted_iota(jnp.int32, (tm, tm), 1)
    same = (r // SCAN_CHUNK) == (c // SCAN_CHUNK)
    m_lo = jnp.where(same & (c <= r), 1.0, 0.0).astype(BF16)
    m_up = jnp.where(same & (c >= r), 1.0, 0.0).astype(BF16)
    parts = _split3(gb)
    gc_f = _dot(m_lo, parts[0]) + _dot(m_lo, parts[1]) + _dot(m_lo, parts[2])
    gc_b = _dot(m_up, parts[0]) + _dot(m_up, parts[1]) + _dot(m_up, parts[2])
    lane = lax.broadcasted_iota(jnp.int32, gb.shape, 1)
    gcb = jnp.where(lane < GDN_HEADS, gc_f, jnp.where(lane < 2 * GDN_HEADS, gc_b, gb))
    gcb_ref[...] = gcb
    for n in range(tm // SCAN_CHUNK):
        gct = gcb[n * SCAN_CHUNK:(n + 1) * SCAN_CHUNK, :].T
        gct_ref[n * 2 * GDN_HEADS:(n + 1) * 2 * GDN_HEADS, :] = gct[0:2 * GDN_HEADS, :]


def _gdn_prep(proj, conv_w8, gb, *, tm=256):
    L = proj.shape[0]
    n_sub = L // SUBLANES
    per = tm // SUBLANES
    row = lambda i: (i, 0)
    return pl.pallas_call(
        functools.partial(_gdn_prep_kernel, tm=tm),
        grid=(L // tm,),
        in_specs=[
            pl.BlockSpec((tm, 3 * GDN_W), row),
            pl.BlockSpec((SUBLANES, 3 * GDN_W), lambda i: (jnp.maximum(i * per - 1, 0), 0)),
            pl.BlockSpec((SUBLANES, 3 * GDN_W), lambda i: (jnp.minimum((i + 1) * per, n_sub - 1), 0)),
            pl.BlockSpec((SUBLANES, 3 * GDN_W), lambda i: (0, 0)),
            pl.BlockSpec((tm, LANES), row),
        ],
        out_specs=[
            pl.BlockSpec((tm, GDN_W), row),
            pl.BlockSpec((tm, GDN_W), row),
            pl.BlockSpec((tm, GDN_W), row),
            pl.BlockSpec((tm, LANES), row),
            pl.BlockSpec((tm // SCAN_CHUNK * 2 * GDN_HEADS, SCAN_CHUNK), row),
        ],
        out_shape=[
            jax.ShapeDtypeStruct((L, GDN_W), F32),
            jax.ShapeDtypeStruct((L, GDN_W), F32),
            jax.ShapeDtypeStruct((L, GDN_W), F32),
            jax.ShapeDtypeStruct((L, LANES), F32),
            jax.ShapeDtypeStruct((L // SCAN_CHUNK * 2 * GDN_HEADS, SCAN_CHUNK), F32),
        ],
        scratch_shapes=[pltpu.VMEM((tm + 2 * SUBLANES, 3 * GDN_W), F32)],
        compiler_params=_cparams(("parallel",)),
        name="gdn_prep",
    )(proj, proj, proj, conv_w8, gb)


def _gdn_scan_kernel(q_ref, k_ref, v_ref, gcb_ref, gct_ref, o_ref, s_ref):
    d = pl.program_id(0)
    C = SCAN_CHUNK

    @pl.when(pl.program_id(1) == 0)
    def _():
        s_ref[...] = jnp.zeros_like(s_ref)

    fwd = d == 0
    r = lax.broadcasted_iota(jnp.int32, (C, C), 0)
    c = lax.broadcasted_iota(jnp.int32, (C, C), 1)
    dif = jnp.where(fwd, r - c, c - r)
    m_incl = dif >= 0
    m_strict = dif > 0
    eye = jnp.where(dif == 0, 1.0, 0.0)

    gcb = gcb_ref[...]
    gc = jnp.where(fwd, gcb[:, 0:GDN_HEADS], gcb[:, GDN_HEADS:2 * GDN_HEADS])
    beta = jnp.where(fwd, gcb[:, 2 * GDN_HEADS:3 * GDN_HEADS], gcb[:, 3 * GDN_HEADS:4 * GDN_HEADS])
    g_end = jnp.where(fwd, gc[C - 1:C, :], gc[0:1, :])
    e_gc = jnp.exp(gc)
    e_rest = jnp.exp(g_end - gc)
    e_end = jnp.exp(g_end)
    gct = gct_ref[...]

    heads = range(GDN_HEADS)
    cols = [slice(h * HEAD_DIM, (h + 1) * HEAD_DIM) for h in heads]
    col1 = [slice(h, h + 1) for h in heads]

    k = [k_ref[:, cols[h]] for h in heads]
    k16 = [k[h].astype(BF16) for h in heads]
    kb = [k[h] * beta[:, col1[h]] for h in heads]
    q16 = [q_ref[:, cols[h]].astype(BF16) for h in heads]
    kq = [_dot_nt(jnp.concatenate([kb[h].astype(BF16), q16[h]], axis=0), k16[h]) for h in heads]
    decay = []
    for h in heads:
        diff = gc[:, col1[h]] - gct[col1[h], :]
        decay.append(jnp.where(m_incl, jnp.exp(jnp.where(m_incl, diff, 0.0)), 0.0))
    a = [jnp.where(m_strict, kq[h][0:C] * decay[h], 0.0) for h in heads]
    attn_qk16 = [(kq[h][C:2 * C] * decay[h]).astype(BF16) for h in heads]

    rb = r >> INV_BASE_LEVELS
    cb = c >> INV_BASE_LEVELS
    a_d = [jnp.where(rb == cb, a[h], 0.0) for h in heads]
    inv = [eye - a_d[h] for h in heads]
    a_d16 = [a_d[h].astype(BF16) for h in heads]
    a_pow16 = [_dot(a_d16[h], a_d16[h]).astype(BF16) for h in heads]
    for level in range(INV_BASE_LEVELS - 1):
        if level < INV_BASE_LEVELS - 2:
            both = [_dot(jnp.concatenate([inv[h].astype(BF16), a_pow16[h]], axis=0), a_pow16[h]) for h in heads]
            inv = [inv[h] + both[h][0:C] for h in heads]
            a_pow16 = [both[h][C:2 * C].astype(BF16) for h in heads]
        else:
            inv = [inv[h] + _dot(inv[h].astype(BF16), a_pow16[h]) for h in heads]
    b = INV_BASE
    while b < C:
        off = ((rb >> 1) == (cb >> 1)) & (rb != cb)
        a_off16 = [jnp.where(off, a[h], 0.0).astype(BF16) for h in heads]
        inv16 = [inv[h].astype(BF16) for h in heads]
        left = [_dot(inv16[h], a_off16[h]).astype(BF16) for h in heads]
        inv = [inv[h] - _dot(left[h], inv16[h]) for h in heads]
        rb = rb >> 1
        cb = cb >> 1
        b *= 2

    rhs16 = [jnp.concatenate([(v_ref[:, cols[h]] * beta[:, col1[h]]).astype(BF16),
                              (kb[h] * e_gc[:, col1[h]]).astype(BF16)], axis=1) for h in heads]
    uw = [_dot(inv[h].astype(BF16), rhs16[h]) for h in heads]
    q_dec16 = [(q_ref[:, cols[h]] * e_gc[:, col1[h]]).astype(BF16) for h in heads]
    k_dec_t16 = [(k[h] * e_rest[:, col1[h]]).T.astype(BF16) for h in heads]

    s = [s_ref[h] for h in heads]
    s16 = [s[h].astype(BF16) for h in heads]
    ws = [_dot(jnp.concatenate([uw[h][:, HEAD_DIM:].astype(BF16), q_dec16[h]], axis=0), s16[h]) for h in heads]
    v_new16 = [(uw[h][:, 0:HEAD_DIM] - ws[h][0:C]).astype(BF16) for h in heads]
    for h in heads:
        o_ref[:, cols[h]] = ws[h][C:2 * C] + _dot(attn_qk16[h], v_new16[h])
    for h in heads:
        s_ref[h] = s[h] * e_end[:, col1[h]] + _dot(k_dec_t16[h], v_new16[h])


def _gdn_scan(q, k, v, gcb, gct):
    L = q.shape[0]
    n = L // SCAN_CHUNK
    chunk = lambda d, i: i + d * (n - 1 - 2 * i)
    row = lambda d, i: (chunk(d, i), 0)
    return pl.pallas_call(
        _gdn_scan_kernel,
        grid=(2, n),
        in_specs=[
            pl.BlockSpec((SCAN_CHUNK, GDN_W), row),
            pl.BlockSpec((SCAN_CHUNK, GDN_W), row),
            pl.BlockSpec((SCAN_CHUNK, GDN_W), row),
            pl.BlockSpec((SCAN_CHUNK, LANES), row),
            pl.BlockSpec((GDN_HEADS, SCAN_CHUNK), lambda d, i: (2 * chunk(d, i) + d, 0)),
        ],
        out_specs=pl.BlockSpec((None, SCAN_CHUNK, GDN_W), lambda d, i: (d, chunk(d, i), 0)),
        out_shape=jax.ShapeDtypeStruct((2, L, GDN_W), F32),
        scratch_shapes=[pltpu.VMEM((GDN_HEADS, HEAD_DIM, HEAD_DIM), F32)],
        compiler_params=_cparams(("arbitrary", "arbitrary")),
        name="gdn_scan",
    )(q, k, v, gcb, gct)


def _flash_kernel(q_ref, k_ref, vt_ref, on_ref, o_ref, m_ref, acc_ref, st_ref, p_ref, *, tk):
    ki = pl.program_id(2)

    @pl.when(ki == 0)
    def _():
        m_ref[...] = jnp.full_like(m_ref, -jnp.inf)
        acc_ref[...] = jnp.zeros_like(acc_ref)

    k = k_ref[...]
    vt1 = jnp.concatenate([vt_ref[...], jnp.ones((FLASH_SUM_ROWS, tk), BF16)], axis=0)
    blocks = [slice(r, r + FLASH_ROW_BLOCK) for r in range(0, tk, FLASH_ROW_BLOCK)]

    def scores(h):
        st_ref[h] = _dot_nt(k, q_ref[:, h * HEAD_DIM:(h + 1) * HEAD_DIM])

    def softmax(h):
        part = None
        for rows in blocks:
            x = st_ref[h, rows, :]
            while x.shape[0] > SUBLANES:
                half = x.shape[0] // 2
                x = jnp.maximum(x[:half], x[half:])
            part = x if part is None else jnp.maximum(part, x)
        m_prev = m_ref[h]
        m_new = jnp.maximum(m_prev, jnp.max(part, axis=0, keepdims=True))
        m_ref[h] = m_new
        for rows in blocks:
            p_ref[h, rows, :] = jnp.exp2(st_ref[h, rows, :] - m_new).astype(BF16)
        return jnp.exp2(m_prev - m_new)

    def accumulate(h, alpha):
        acc_ref[h] = alpha * acc_ref[h] + _dot(vt1, p_ref[h])

    scores(0)
    pending = None
    for h in range(ATT_GROUP):
        if h + 1 < ATT_GROUP:
            scores(h + 1)
        alpha = softmax(h)
        if pending is not None:
            accumulate(*pending)
        pending = (h, alpha)
    accumulate(*pending)

    @pl.when(ki == pl.num_programs(2) - 1)
    def _():
        for h in range(ATT_GROUP):
            cols = slice(h * HEAD_DIM, (h + 1) * HEAD_DIM)
            o = (acc_ref[h, 0:HEAD_DIM, :] / acc_ref[h, HEAD_DIM:HEAD_DIM + 1, :]).T
            o_ref[:, cols] = _rms(o, on_ref[...]).astype(BF16)


def _flash_attn(q, k, vt, out_norm, *, tq=512, tk=1024):
    L = q.shape[0]
    gw = ATT_GROUP * HEAD_DIM
    return pl.pallas_call(
        functools.partial(_flash_kernel, tk=tk),
        grid=(ATT_KV_HEADS, L // tq, L // tk),
        in_specs=[
            pl.BlockSpec((tq, gw), lambda g, i, j: (i, g)),
            pl.BlockSpec((tk, HEAD_DIM), lambda g, i, j: (j, g)),
            pl.BlockSpec((HEAD_DIM, tk), lambda g, i, j: (g, j)),
            pl.BlockSpec((1, HEAD_DIM), lambda g, i, j: (0, 0)),
        ],
        out_specs=pl.BlockSpec((tq, gw), lambda g, i, j: (i, g)),
        out_shape=jax.ShapeDtypeStruct((L, ATT_Q_W), BF16),
        scratch_shapes=[
            pltpu.VMEM((ATT_GROUP, 1, tq), F32),
            pltpu.VMEM((ATT_GROUP, HEAD_DIM + FLASH_SUM_ROWS, tq), F32),
            pltpu.VMEM((ATT_GROUP, tk, tq), F32),
            pltpu.VMEM((ATT_GROUP, tk, tq), BF16),
        ],
        compiler_params=_cparams(("parallel", "parallel", "arbitrary")),
        name="flash_attn",
    )(q, k, vt, out_norm)


def _out_proj_kernel(og_ref, z_ref, oa_ref, h_ref, gn_ref, w_ref, o_ref, mix_ref):
    for h in range(GDN_HEADS):
        cols = slice(h * HEAD_DIM, (h + 1) * HEAD_DIM)
        o = _rms(og_ref[0, :, cols] + og_ref[1, :, cols], gn_ref[...])
        z = z_ref[:, cols]
        mix_ref[:, cols] = (o * (z * jax.nn.sigmoid(z))).astype(BF16)
    mix_ref[:, GDN_W:] = oa_ref[...]
    o_ref[...] = h_ref[...] + _dot(mix_ref[...], w_ref[...])


def _out_proj(og, proj, oa, h1, gdn_norm, w_out, *, tm=256):
    L = h1.shape[0]
    row = lambda i: (i, 0)
    return pl.pallas_call(
        _out_proj_kernel,
        grid=(L // tm,),
        in_specs=[
            pl.BlockSpec((2, tm, GDN_W), lambda i: (0, i, 0)),
            pl.BlockSpec((tm, GDN_W), lambda i: (i, 3)),
            pl.BlockSpec((tm, ATT_Q_W), row),
            pl.BlockSpec((tm, D_MODEL), row),
            pl.BlockSpec((1, HEAD_DIM), lambda i: (0, 0)),
            pl.BlockSpec((GDN_W + ATT_Q_W, D_MODEL), lambda i: (0, 0)),
        ],
        out_specs=pl.BlockSpec((tm, D_MODEL), row),
        out_shape=jax.ShapeDtypeStruct((L, D_MODEL), F32),
        scratch_shapes=[pltpu.VMEM((tm, GDN_W + ATT_Q_W), BF16)],
        compiler_params=_cparams(("parallel",)),
        name="out_proj",
    )(og, proj, oa, h1, gdn_norm, w_out)


def _rope_tables(L):
    t = jnp.arange(L, dtype=jnp.int32)
    row = (t // GRID_W).astype(F32)
    col = (t % GRID_W).astype(F32)
    freqs = ROPE_THETA ** (-jnp.arange(0, AXIS_DIM, 2, dtype=F32) / AXIS_DIM)
    ang_r = row[:, None] * freqs[None, :]
    ang_c = col[:, None] * freqs[None, :]
    zero = jnp.zeros_like(ang_r)
    cos = jnp.concatenate([jnp.cos(ang_r)] * 2 + [jnp.cos(ang_c)] * 2, axis=-1)
    sin_lo = jnp.concatenate([-jnp.sin(ang_r), zero, -jnp.sin(ang_c), zero], axis=-1)
    sin_hi = jnp.concatenate([zero, jnp.sin(ang_r), zero, jnp.sin(ang_c)], axis=-1)
    return cos, sin_lo, sin_hi


def _pad_lanes(x):
    x = x.reshape(1, -1)
    return jnp.pad(x, ((0, 0), (0, LANES - x.shape[1])))


def _encode(x, p):
    L = x.shape[0]
    h1 = _ffn(x, p["ffn1_norm"], p["ffn1_wg"], p["ffn1_wu"], p["ffn1_wd"], p["final_norm"], final=False)
    proj = _norm_matmul(h1, p["mix_norm"], p["w_in"], GDN_PROJ_W)
    cos, sin_lo, sin_hi = _rope_tables(L)
    qa, ka, va, gb = _attn_proj(h1, p["mix_norm"], p["w_qk"], p["w_vt"], p["w_ab"], p["q_norm"], p["k_norm"],
                                cos, sin_lo, sin_hi, p["alog"], p["dtb"])
    qg, kg, vg, gcb, gct = _gdn_prep(proj, p["conv_w"], gb)
    og = _gdn_scan(qg, kg, vg, gcb, gct)
    oa = _flash_attn(qa, ka, va, p["attn_out_norm"])
    h2 = _out_proj(og, proj, oa, h1, p["gdn_out_norm"], p["w_out"])
    return _ffn(h2, p["ffn2_norm"], p["ffn2_wg"], p["ffn2_wu"], p["ffn2_wd"], p["final_norm"], final=True)


def _prepare_params(ffn1_norm, ffn1_w_gate, ffn1_w_up, ffn1_w_down, mix_norm, w_in, conv_w, a_log_fwd, a_log_bwd,
                    dt_bias_fwd, dt_bias_bwd, gdn_out_norm, q_norm, k_norm, attn_out_norm, w_out, ffn2_norm,
                    ffn2_w_gate, ffn2_w_up, ffn2_w_down, final_norm):
    w_in0 = w_in[0]
    att0 = GDN_PROJ_W + GATE_COLS
    w_ab = jnp.pad(w_in0[:, GDN_PROJ_W:att0], ((0, 0), (0, LANES - GATE_COLS)))
    return dict(
        ffn1_norm=ffn1_norm[0].reshape(1, -1),
        ffn1_wg=ffn1_w_gate[0].astype(BF16), ffn1_wu=ffn1_w_up[0].astype(BF16), ffn1_wd=ffn1_w_down[0].astype(BF16),
        mix_norm=mix_norm[0].reshape(1, -1),
        w_in=w_in0.astype(BF16),
        w_qk=w_in0[:, att0:att0 + ATT_Q_W + ATT_KV_W].astype(BF16),
        w_vt=w_in0[:, att0 + ATT_Q_W + ATT_KV_W:].T.astype(BF16),
        w_ab=w_ab.astype(BF16),
        conv_w=jnp.pad(conv_w[0], ((0, SUBLANES - CONV_K), (0, 0))),
        alog=_pad_lanes(jnp.concatenate([a_log_fwd[0], a_log_bwd[0]])),
        dtb=_pad_lanes(jnp.concatenate([dt_bias_fwd[0], dt_bias_bwd[0]])),
        gdn_out_norm=gdn_out_norm[0].reshape(1, -1),
        q_norm=q_norm[0].reshape(1, -1), k_norm=k_norm[0].reshape(1, -1),
        attn_out_norm=attn_out_norm[0].reshape(1, -1),
        w_out=w_out[0].astype(BF16),
        ffn2_norm=ffn2_norm[0].reshape(1, -1),
        ffn2_wg=ffn2_w_gate[0].astype(BF16), ffn2_wu=ffn2_w_up[0].astype(BF16), ffn2_wd=ffn2_w_down[0].astype(BF16),
        final_norm=final_norm.reshape(1, -1),
    )


def kernel(x_prompt, x_sample, ffn1_norm, ffn1_w_gate, ffn1_w_up, ffn1_w_down, mix_norm, w_in, conv_w, a_log_fwd,
           a_log_bwd, dt_bias_fwd, dt_bias_bwd, gdn_out_norm, q_norm, k_norm, attn_out_norm, w_out, ffn2_norm,
           ffn2_w_gate, ffn2_w_up, ffn2_w_down, final_norm):
    assert x_prompt.shape[0] == 1 and x_sample.shape[0] == 1
    p = _prepare_params(ffn1_norm, ffn1_w_gate, ffn1_w_up, ffn1_w_down, mix_norm, w_in, conv_w, a_log_fwd,
                        a_log_bwd, dt_bias_fwd, dt_bias_bwd, gdn_out_norm, q_norm, k_norm, attn_out_norm, w_out,
                        ffn2_norm, ffn2_w_gate, ffn2_w_up, ffn2_w_down, final_norm)
    y_prompt = _encode(x_prompt[0], p)
    y_sample = _encode(x_sample[0], p)
    return (y_prompt[None], y_sample[None])
```

```python
import functools
import math

import jax
import jax.numpy as jnp
from jax import lax
from jax.experimental import pallas as pl
from jax.experimental.pallas import tpu as pltpu

D_MODEL = 2048
HEAD_DIM = 128
GDN_HEADS = 8
GDN_W = GDN_HEADS * HEAD_DIM
ATT_Q_HEADS = 8
ATT_KV_HEADS = 2
ATT_GROUP = ATT_Q_HEADS // ATT_KV_HEADS
ATT_Q_W = ATT_Q_HEADS * HEAD_DIM
ATT_KV_W = ATT_KV_HEADS * HEAD_DIM
ATT_W = ATT_Q_W + 2 * ATT_KV_W
GDN_PROJ_W = 4 * GDN_W
GATE_COLS = 4 * GDN_HEADS
D_FF = 5632
CONV_K = 5
GRID_W = 64
AXIS_DIM = HEAD_DIM // 2
ROPE_THETA = 10000.0
EPS = 1e-6

LANES = 128
SUBLANES = 8
SCAN_CHUNK = 128
PROJ_COL_TILE = 1024
INV_BASE_LEVELS = 2
INV_BASE = 2 ** INV_BASE_LEVELS
EXP2_SCALE = HEAD_DIM ** -0.5 * math.log2(math.e)
FLASH_ROW_BLOCK = 64
FLASH_SUM_ROWS = 16
FLASH_Q_BLOCK = 512

F32 = jnp.float32
BF16 = jnp.bfloat16

_VMEM_LIMIT = 56 * 1024 * 1024


def _cparams(semantics):
    return pltpu.CompilerParams(dimension_semantics=semantics, vmem_limit_bytes=_VMEM_LIMIT)


def _rms(x, g):
    return x * lax.rsqrt(jnp.mean(x * x, axis=-1, keepdims=True) + EPS) * g


def _dot(a, b):
    return jnp.dot(a, b, preferred_element_type=F32)


def _dot_nt(a, b):
    return lax.dot_general(a, b, (((1,), (1,)), ((), ())), preferred_element_type=F32)


def _split3(x):
    hi = x.astype(BF16)
    r = x - hi.astype(F32)
    mid = r.astype(BF16)
    lo = (r - mid.astype(F32)).astype(BF16)
    return hi, mid, lo


def _ffn_kernel(x_ref, g_ref, wg_ref, wu_ref, wd_ref, fg_ref, o_ref, xn_ref, *, final):
    j = pl.program_id(1)

    @pl.when(j == 0)
    def _():
        xn_ref[...] = _rms(x_ref[...], g_ref[...]).astype(BF16)
        o_ref[...] = jnp.zeros_like(o_ref)

    xn = xn_ref[...]
    gate = _dot(xn, wg_ref[...])
    up = _dot(xn, wu_ref[...])
    act = (gate * jax.nn.sigmoid(gate) * up).astype(BF16)
    o_ref[...] += _dot(act, wd_ref[...])

    @pl.when(j == pl.num_programs(1) - 1)
    def _():
        h = x_ref[...] + 0.5 * o_ref[...]
        if final:
            h = _rms(h, fg_ref[...])
        o_ref[...] = h


def _ffn(x, norm_g, wg, wu, wd, final_g, *, final, tm=512, tf=512):
    L = x.shape[0]
    assert L % tm == 0 and D_FF % tf == 0
    grid = (L // tm, D_FF // tf)
    return pl.pallas_call(
        functools.partial(_ffn_kernel, final=final),
        grid=grid,
        in_specs=[
            pl.BlockSpec((tm, D_MODEL), lambda i, j: (i, 0)),
            pl.BlockSpec((1, D_MODEL), lambda i, j: (0, 0)),
            pl.BlockSpec((D_MODEL, tf), lambda i, j: (0, j)),
            pl.BlockSpec((D_MODEL, tf), lambda i, j: (0, j)),
            pl.BlockSpec((tf, D_MODEL), lambda i, j: (j, 0)),
            pl.BlockSpec((1, D_MODEL), lambda i, j: (0, 0)),
        ],
        out_specs=pl.BlockSpec((tm, D_MODEL), lambda i, j: (i, 0)),
        out_shape=jax.ShapeDtypeStruct((L, D_MODEL), F32),
        scratch_shapes=[pltpu.VMEM((tm, D_MODEL), BF16)],
        compiler_params=_cparams(("parallel", "arbitrary")),
        name="ffn",
    )(x, norm_g, wg, wu, wd, final_g)


def _rope(x, cos, sin_lo, sin_hi):
    return (x * cos + pltpu.roll(x, AXIS_DIM // 2, axis=1) * sin_hi
            + pltpu.roll(x, HEAD_DIM - AXIS_DIM // 2, axis=1) * sin_lo)


def _mix_proj_kernel(x_ref, g_ref, wg_ref, w_ref, wvt_ref, wab_ref, qn_ref, kn_ref, cos_ref, slo_ref, shi_ref,
                     alog_ref, dtb_ref, proj_ref, q_ref, k_ref, vt_ref, gb_ref):
    xn = _rms(x_ref[...], g_ref[...]).astype(BF16)
    for c in range(0, GDN_PROJ_W, PROJ_COL_TILE):
        proj_ref[:, c:c + PROJ_COL_TILE] = _dot(xn, wg_ref[:, c:c + PROJ_COL_TILE])
    cos = cos_ref[...]
    slo = slo_ref[...]
    shi = shi_ref[...]
    for h in range(0, ATT_Q_HEADS + ATT_KV_HEADS, 2):
        pair = _dot(xn, w_ref[:, h * HEAD_DIM:(h + 2) * HEAD_DIM])
        for i in range(2):
            head = pair[:, i * HEAD_DIM:(i + 1) * HEAD_DIM]
            if h + i < ATT_Q_HEADS:
                cols = slice((h + i) * HEAD_DIM, (h + i + 1) * HEAD_DIM)
                q_ref[:, cols] = (_rope(_rms(head, qn_ref[...]), cos, slo, shi) * EXP2_SCALE).astype(BF16)
            else:
                cols = slice((h + i - ATT_Q_HEADS) * HEAD_DIM, (h + i - ATT_Q_HEADS + 1) * HEAD_DIM)
                k_ref[:, cols] = _rope(_rms(head, kn_ref[...]), cos, slo, shi).astype(BF16)
    vt_ref[...] = _dot_nt(wvt_ref[...], xn).astype(BF16)

    ab = _dot(xn, wab_ref[...])
    t = ab + dtb_ref[...]
    softplus = jnp.maximum(t, 0.0) + jnp.log1p(jnp.exp(-jnp.abs(t)))
    g = -jnp.exp(alog_ref[...]) * softplus
    lane = lax.broadcasted_iota(jnp.int32, ab.shape, 1)
    gb_ref[...] = jnp.where(lane < 2 * GDN_HEADS, g, jax.nn.sigmoid(ab))


def _mix_proj(x, norm_g, w_in, w_qk, w_vt, w_ab, q_norm, k_norm, cos, sin_lo, sin_hi, alog, dtb, *, tm=512):
    L = x.shape[0]
    assert L % tm == 0
    row = lambda i: (i, 0)
    fixed = lambda i: (0, 0)
    resident = pl.Buffered(1)
    return pl.pallas_call(
        _mix_proj_kernel,
        grid=(L // tm,),
        in_specs=[
            pl.BlockSpec((tm, D_MODEL), row),
            pl.BlockSpec((1, D_MODEL), fixed),
            pl.BlockSpec((D_MODEL, GDN_PROJ_W), fixed, pipeline_mode=resident),
            pl.BlockSpec((D_MODEL, ATT_Q_W + ATT_KV_W), fixed, pipeline_mode=resident),
            pl.BlockSpec((ATT_KV_W, D_MODEL), fixed, pipeline_mode=resident),
            pl.BlockSpec((D_MODEL, LANES), fixed),
            pl.BlockSpec((1, HEAD_DIM), fixed),
            pl.BlockSpec((1, HEAD_DIM), fixed),
            pl.BlockSpec((tm, HEAD_DIM), row),
            pl.BlockSpec((tm, HEAD_DIM), row),
            pl.BlockSpec((tm, HEAD_DIM), row),
            pl.BlockSpec((1, LANES), fixed),
            pl.BlockSpec((1, LANES), fixed),
        ],
        out_specs=[
            pl.BlockSpec((tm, GDN_PROJ_W), row),
            pl.BlockSpec((tm, ATT_Q_W), row),
            pl.BlockSpec((tm, ATT_KV_W), row),
            pl.BlockSpec((ATT_KV_W, tm), lambda i: (0, i)),
            pl.BlockSpec((tm, LANES), row),
        ],
        out_shape=[
            jax.ShapeDtypeStruct((L, GDN_PROJ_W), F32),
            jax.ShapeDtypeStruct((L, ATT_Q_W), BF16),
            jax.ShapeDtypeStruct((L, ATT_KV_W), BF16),
            jax.ShapeDtypeStruct((ATT_KV_W, L), BF16),
            jax.ShapeDtypeStruct((L, LANES), F32),
        ],
        compiler_params=_cparams(("parallel",)),
        name="mix_proj",
    )(x, norm_g, w_in, w_qk, w_vt, w_ab, q_norm, k_norm, cos, sin_lo, sin_hi, alog, dtb)


def _gdn_prep_kernel(x_ref, prev_ref, next_ref, cw_ref, gb_ref, q_ref, k_ref, v_ref, gcb_ref, gct_ref,
                     ext_ref, *, tm):
    i = pl.program_id(0)
    pad = CONV_K // 2
    ext_ref[0:SUBLANES, :] = jnp.where(i == 0, 0.0, prev_ref[...])
    ext_ref[SUBLANES:SUBLANES + tm, :] = x_ref[...]
    ext_ref[SUBLANES + tm:, :] = jnp.where(i == pl.num_programs(0) - 1, 0.0, next_ref[...])

    for s in range(3 * GDN_HEADS):
        cols = slice(s * HEAD_DIM, (s + 1) * HEAD_DIM)
        y = None
        for t in range(CONV_K):
            term = ext_ref[SUBLANES - pad + t:SUBLANES - pad + t + tm, cols] * cw_ref[t:t + 1, cols]
            y = term if y is None else y + term
        y = y * jax.nn.sigmoid(y)
        part, h = divmod(s, GDN_HEADS)
        hc = slice(h * HEAD_DIM, (h + 1) * HEAD_DIM)
        if part == 0:
            q_ref[:, hc] = y * lax.rsqrt(jnp.sum(y * y, axis=-1, keepdims=True) + EPS) * (HEAD_DIM ** -0.5)
        elif part == 1:
            k_ref[:, hc] = y * lax.rsqrt(jnp.sum(y * y, axis=-1, keepdims=True) + EPS)
        else:
            v_ref[:, hc] = y

    gb = gb_ref[...]
    r = lax.broadcasted_iota(jnp.int32, (tm, tm), 0)
    c = lax.broadcasted_iota(jnp.int32, (tm, tm), 1)
    same = (r // SCAN_CHUNK) == (c // SCAN_CHUNK)
    m_lo = jnp.where(same & (c <= r), 1.0, 0.0).astype(BF16)
    m_up = jnp.where(same & (c >= r), 1.0, 0.0).astype(BF16)
    parts = _split3(gb)
    gc_f = _dot(m_lo, parts[0]) + _dot(m_lo, parts[1]) + _dot(m_lo, parts[2])
    gc_b = _dot(m_up, parts[0]) + _dot(m_up, parts[1]) + _dot(m_up, parts[2])
    lane = lax.broadcasted_iota(jnp.int32, gb.shape, 1)
    gcb = jnp.where(lane < GDN_HEADS, gc_f, jnp.where(lane < 2 * GDN_HEADS, gc_b, gb))
    gcb_ref[...] = gcb
    for n in range(tm // SCAN_CHUNK):
        gct = gcb[n * SCAN_CHUNK:(n + 1) * SCAN_CHUNK, :].T
        gct_ref[n * 2 * GDN_HEADS:(n + 1) * 2 * GDN_HEADS, :] = gct[0:2 * GDN_HEADS, :]


def _gdn_prep(proj, conv_w8, gb, *, tm=256):
    L = proj.shape[0]
    assert L % tm == 0 and tm % SCAN_CHUNK == 0
    n_sub = L // SUBLANES
    per = tm // SUBLANES
    row = lambda i: (i, 0)
    return pl.pallas_call(
        functools.partial(_gdn_prep_kernel, tm=tm),
        grid=(L // tm,),
        in_specs=[
            pl.BlockSpec((tm, 3 * GDN_W), row),
            pl.BlockSpec((SUBLANES, 3 * GDN_W), lambda i: (jnp.maximum(i * per - 1, 0), 0)),
            pl.BlockSpec((SUBLANES, 3 * GDN_W), lambda i: (jnp.minimum((i + 1) * per, n_sub - 1), 0)),
            pl.BlockSpec((SUBLANES, 3 * GDN_W), lambda i: (0, 0)),
            pl.BlockSpec((tm, LANES), row),
        ],
        out_specs=[
            pl.BlockSpec((tm, GDN_W), row),
            pl.BlockSpec((tm, GDN_W), row),
            pl.BlockSpec((tm, GDN_W), row),
            pl.BlockSpec((tm, LANES), row),
            pl.BlockSpec((tm // SCAN_CHUNK * 2 * GDN_HEADS, SCAN_CHUNK), row),
        ],
        out_shape=[
            jax.ShapeDtypeStruct((L, GDN_W), F32),
            jax.ShapeDtypeStruct((L, GDN_W), F32),
            jax.ShapeDtypeStruct((L, GDN_W), F32),
            jax.ShapeDtypeStruct((L, LANES), F32),
            jax.ShapeDtypeStruct((L // SCAN_CHUNK * 2 * GDN_HEADS, SCAN_CHUNK), F32),
        ],
        scratch_shapes=[pltpu.VMEM((tm + 2 * SUBLANES, 3 * GDN_W), F32)],
        compiler_params=_cparams(("parallel",)),
        name="gdn_prep",
    )(proj, proj, proj, conv_w8, gb)


def _gdn_scan_kernel(q_ref, k_ref, v_ref, gcb_ref, gct_ref, o_ref, s_ref):
    d = pl.program_id(0)
    C = SCAN_CHUNK

    @pl.when(pl.program_id(1) == 0)
    def _():
        s_ref[...] = jnp.zeros_like(s_ref)

    fwd = d == 0
    r = lax.broadcasted_iota(jnp.int32, (C, C), 0)
    c = lax.broadcasted_iota(jnp.int32, (C, C), 1)
    dif = jnp.where(fwd, r - c, c - r)
    m_incl = dif >= 0
    m_strict = dif > 0
    eye = jnp.where(dif == 0, 1.0, 0.0)

    gcb = gcb_ref[...]
    gc = jnp.where(fwd, gcb[:, 0:GDN_HEADS], gcb[:, GDN_HEADS:2 * GDN_HEADS])
    beta = jnp.where(fwd, gcb[:, 2 * GDN_HEADS:3 * GDN_HEADS], gcb[:, 3 * GDN_HEADS:4 * GDN_HEADS])
    g_end = jnp.where(fwd, gc[C - 1:C, :], gc[0:1, :])
    e_gc = jnp.exp(gc)
    e_rest = jnp.exp(g_end - gc)
    e_end = jnp.exp(g_end)
    gct = gct_ref[...]

    heads = range(GDN_HEADS)
    cols = [slice(h * HEAD_DIM, (h + 1) * HEAD_DIM) for h in heads]
    col1 = [slice(h, h + 1) for h in heads]

    k = [k_ref[:, cols[h]] for h in heads]
    k16 = [k[h].astype(BF16) for h in heads]
    kb = [k[h] * beta[:, col1[h]] for h in heads]
    q16 = [q_ref[:, cols[h]].astype(BF16) for h in heads]
    kq = [_dot_nt(jnp.concatenate([kb[h].astype(BF16), q16[h]], axis=0), k16[h]) for h in heads]
    decay = []
    for h in heads:
        diff = gc[:, col1[h]] - gct[col1[h], :]
        decay.append(jnp.where(m_incl, jnp.exp(jnp.where(m_incl, diff, 0.0)), 0.0))
    a = [jnp.where(m_strict, kq[h][0:C] * decay[h], 0.0) for h in heads]
    attn_qk16 = [(kq[h][C:2 * C] * decay[h]).astype(BF16) for h in heads]

    rb = r >> INV_BASE_LEVELS
    cb = c >> INV_BASE_LEVELS
    a_d = [jnp.where(rb == cb, a[h], 0.0) for h in heads]
    inv = [eye - a_d[h] for h in heads]
    a_d16 = [a_d[h].astype(BF16) for h in heads]
    a_pow16 = [_dot(a_d16[h], a_d16[h]).astype(BF16) for h in heads]
    for level in range(INV_BASE_LEVELS - 1):
        if level < INV_BASE_LEVELS - 2:
            both = [_dot(jnp.concatenate([inv[h].astype(BF16), a_pow16[h]], axis=0), a_pow16[h]) for h in heads]
            inv = [inv[h] + both[h][0:C] for h in heads]
            a_pow16 = [both[h][C:2 * C].astype(BF16) for h in heads]
        else:
            inv = [inv[h] + _dot(inv[h].astype(BF16), a_pow16[h]) for h in heads]
    b = INV_BASE
    while b < C:
        off = ((rb >> 1) == (cb >> 1)) & (rb != cb)
        a_off16 = [jnp.where(off, a[h], 0.0).astype(BF16) for h in heads]
        inv16 = [inv[h].astype(BF16) for h in heads]
        left = [_dot(inv16[h], a_off16[h]).astype(BF16) for h in heads]
        inv = [inv[h] - _dot(left[h], inv16[h]) for h in heads]
        rb = rb >> 1
        cb = cb >> 1
        b *= 2

    rhs16 = [jnp.concatenate([(v_ref[:, cols[h]] * beta[:, col1[h]]).astype(BF16),
                              (kb[h] * e_gc[:, col1[h]]).astype(BF16)], axis=1) for h in heads]
    uw = [_dot(inv[h].astype(BF16), rhs16[h]) for h in heads]
    q_dec16 = [(q_ref[:, cols[h]] * e_gc[:, col1[h]]).astype(BF16) for h in heads]
    k_dec_t16 = [(k[h] * e_rest[:, col1[h]]).T.astype(BF16) for h in heads]

    s = [s_ref[h] for h in heads]
    s16 = [s[h].astype(BF16) for h in heads]
    ws = [_dot(jnp.concatenate([uw[h][:, HEAD_DIM:].astype(BF16), q_dec16[h]], axis=0), s16[h]) for h in heads]
    v_new16 = [(uw[h][:, 0:HEAD_DIM] - ws[h][0:C]).astype(BF16) for h in heads]
    for h in heads:
        o_ref[:, cols[h]] = ws[h][C:2 * C] + _dot(attn_qk16[h], v_new16[h])
    for h in heads:
        s_ref[h] = s[h] * e_end[:, col1[h]] + _dot(k_dec_t16[h], v_new16[h])


def _gdn_scan(q, k, v, gcb, gct):
    L = q.shape[0]
    assert L % SCAN_CHUNK == 0
    n = L // SCAN_CHUNK
    chunk = lambda d, i: i + d * (n - 1 - 2 * i)
    row = lambda d, i: (chunk(d, i), 0)
    return pl.pallas_call(
        _gdn_scan_kernel,
        grid=(2, n),
        in_specs=[
            pl.BlockSpec((SCAN_CHUNK, GDN_W), row),
            pl.BlockSpec((SCAN_CHUNK, GDN_W), row),
            pl.BlockSpec((SCAN_CHUNK, GDN_W), row),
            pl.BlockSpec((SCAN_CHUNK, LANES), row),
            pl.BlockSpec((GDN_HEADS, SCAN_CHUNK), lambda d, i: (2 * chunk(d, i) + d, 0)),
        ],
        out_specs=pl.BlockSpec((None, SCAN_CHUNK, GDN_W), lambda d, i: (d, chunk(d, i), 0)),
        out_shape=jax.ShapeDtypeStruct((2, L, GDN_W), F32),
        scratch_shapes=[pltpu.VMEM((GDN_HEADS, HEAD_DIM, HEAD_DIM), F32)],
        compiler_params=_cparams(("arbitrary", "arbitrary")),
        name="gdn_scan",
    )(q, k, v, gcb, gct)


def _flash_kernel(q_ref, k_ref, vt_ref, on_ref, o_ref, m_ref, acc_ref, st_ref, p_ref, *, tk):
    ki = pl.program_id(2)

    @pl.when(ki == 0)
    def _():
        m_ref[...] = jnp.full_like(m_ref, -jnp.inf)
        acc_ref[...] = jnp.zeros_like(acc_ref)

    k = k_ref[...]
    vt1 = jnp.concatenate([vt_ref[...], jnp.ones((FLASH_SUM_ROWS, tk), BF16)], axis=0)
    blocks = [slice(r, r + FLASH_ROW_BLOCK) for r in range(0, tk, FLASH_ROW_BLOCK)]

    tq = q_ref.shape[0]
    units = [(h, slice(c, c + FLASH_Q_BLOCK)) for h in range(ATT_GROUP) for c in range(0, tq, FLASH_Q_BLOCK)]

    def scores(h, qs):
        st_ref[h, :, qs] = _dot_nt(k, q_ref[qs, h * HEAD_DIM:(h + 1) * HEAD_DIM])

    def softmax(h, qs):
        part = None
        for rows in blocks:
            x = st_ref[h, rows, qs]
            while x.shape[0] > SUBLANES:
                half = x.shape[0] // 2
                x = jnp.maximum(x[:half], x[half:])
            part = x if part is None else jnp.maximum(part, x)
        m_prev = m_ref[h, :, qs]
        m_new = jnp.maximum(m_prev, jnp.max(part, axis=0, keepdims=True))
        m_ref[h, :, qs] = m_new
        for rows in blocks:
            p_ref[h, rows, qs] = jnp.exp2(st_ref[h, rows, qs] - m_new).astype(BF16)
        return jnp.exp2(m_prev - m_new)

    def accumulate(h, qs, alpha):
        acc_ref[h, :, qs] = alpha * acc_ref[h, :, qs] + _dot(vt1, p_ref[h, :, qs])

    scores(*units[0])
    pending = None
    for u, unit in enumerate(units):
        if u + 1 < len(units):
            scores(*units[u + 1])
        alpha = softmax(*unit)
        if pending is not None:
            accumulate(*pending)
        pending = (*unit, alpha)
    accumulate(*pending)

    @pl.when(ki == pl.num_programs(2) - 1)
    def _():
        for h in range(ATT_GROUP):
            cols = slice(h * HEAD_DIM, (h + 1) * HEAD_DIM)
            o = (acc_ref[h, 0:HEAD_DIM, :] / acc_ref[h, HEAD_DIM:HEAD_DIM + 1, :]).T
            o_ref[:, cols] = _rms(o, on_ref[...]).astype(BF16)


def _flash_attn(q, k, vt, out_norm, *, tq=512, tk=2048):
    L = q.shape[0]
    assert L % tq == 0 and L % tk == 0 and tq % FLASH_Q_BLOCK == 0 and tk % FLASH_ROW_BLOCK == 0
    gw = ATT_GROUP * HEAD_DIM
    return pl.pallas_call(
        functools.partial(_flash_kernel, tk=tk),
        grid=(ATT_KV_HEADS, L // tq, L // tk),
        in_specs=[
            pl.BlockSpec((tq, gw), lambda g, i, j: (i, g)),
            pl.BlockSpec((tk, HEAD_DIM), lambda g, i, j: (j, g)),
            pl.BlockSpec((HEAD_DIM, tk), lambda g, i, j: (g, j)),
            pl.BlockSpec((1, HEAD_DIM), lambda g, i, j: (0, 0)),
        ],
        out_specs=pl.BlockSpec((tq, gw), lambda g, i, j: (i, g)),
        out_shape=jax.ShapeDtypeStruct((L, ATT_Q_W), BF16),
        scratch_shapes=[
            pltpu.VMEM((ATT_GROUP, 1, tq), F32),
            pltpu.VMEM((ATT_GROUP, HEAD_DIM + FLASH_SUM_ROWS, tq), F32),
            pltpu.VMEM((ATT_GROUP, tk, tq), F32),
            pltpu.VMEM((ATT_GROUP, tk, tq), BF16),
        ],
        compiler_params=_cparams(("parallel", "parallel", "arbitrary")),
        name="flash_attn",
    )(q, k, vt, out_norm)


def _out_proj_kernel(og_ref, z_ref, oa_ref, h_ref, gn_ref, w_ref, o_ref, mix_ref):
    att = _dot(oa_ref[...], w_ref[GDN_W:, :])
    for h in range(GDN_HEADS):
        cols = slice(h * HEAD_DIM, (h + 1) * HEAD_DIM)
        o = _rms(og_ref[0, :, cols] + og_ref[1, :, cols], gn_ref[...])
        z = z_ref[:, cols]
        mix_ref[:, cols] = (o * (z * jax.nn.sigmoid(z))).astype(BF16)
    o_ref[...] = h_ref[...] + (att + _dot(mix_ref[...], w_ref[0:GDN_W, :]))


def _out_proj(og, proj, oa, h1, gdn_norm, w_out, *, tm=512):
    L = h1.shape[0]
    assert L % tm == 0
    row = lambda i: (i, 0)
    return pl.pallas_call(
        _out_proj_kernel,
        grid=(L // tm,),
        in_specs=[
            pl.BlockSpec((2, tm, GDN_W), lambda i: (0, i, 0)),
            pl.BlockSpec((tm, GDN_W), lambda i: (i, 3)),
            pl.BlockSpec((tm, ATT_Q_W), row),
            pl.BlockSpec((tm, D_MODEL), row),
            pl.BlockSpec((1, HEAD_DIM), lambda i: (0, 0)),
            pl.BlockSpec((GDN_W + ATT_Q_W, D_MODEL), lambda i: (0, 0)),
        ],
        out_specs=pl.BlockSpec((tm, D_MODEL), row),
        out_shape=jax.ShapeDtypeStruct((L, D_MODEL), F32),
        scratch_shapes=[pltpu.VMEM((tm, GDN_W), BF16)],
        compiler_params=_cparams(("parallel",)),
        name="out_proj",
    )(og, proj, oa, h1, gdn_norm, w_out)


def _rope_tables(L):
    t = jnp.arange(L, dtype=jnp.int32)
    row = (t // GRID_W).astype(F32)
    col = (t % GRID_W).astype(F32)
    freqs = ROPE_THETA ** (-jnp.arange(0, AXIS_DIM, 2, dtype=F32) / AXIS_DIM)
    ang_r = row[:, None] * freqs[None, :]
    ang_c = col[:, None] * freqs[None, :]
    zero = jnp.zeros_like(ang_r)
    cos = jnp.concatenate([jnp.cos(ang_r)] * 2 + [jnp.cos(ang_c)] * 2, axis=-1)
    sin_lo = jnp.concatenate([-jnp.sin(ang_r), zero, -jnp.sin(ang_c), zero], axis=-1)
    sin_hi = jnp.concatenate([zero, jnp.sin(ang_r), zero, jnp.sin(ang_c)], axis=-1)
    return cos, sin_lo, sin_hi


def _pad_lanes(x):
    x = x.reshape(1, -1)
    return jnp.pad(x, ((0, 0), (0, LANES - x.shape[1])))


def _encode(x, p, rope):
    h1 = _ffn(x, p["ffn1_norm"], p["ffn1_wg"], p["ffn1_wu"], p["ffn1_wd"], p["final_norm"], final=False)
    cos, sin_lo, sin_hi = rope
    proj, qa, ka, va, gb = _mix_proj(h1, p["mix_norm"], p["w_in"], p["w_qk"], p["w_vt"], p["w_ab"], p["q_norm"],
                                     p["k_norm"], cos, sin_lo, sin_hi, p["alog"], p["dtb"])
    qg, kg, vg, gcb, gct = _gdn_prep(proj, p["conv_w"], gb)
    og = _gdn_scan(qg, kg, vg, gcb, gct)
    oa = _flash_attn(qa, ka, va, p["attn_out_norm"])
    h2 = _out_proj(og, proj, oa, h1, p["gdn_out_norm"], p["w_out"])
    return _ffn(h2, p["ffn2_norm"], p["ffn2_wg"], p["ffn2_wu"], p["ffn2_wd"], p["final_norm"], final=True)


def _prepare_params(ffn1_norm, ffn1_w_gate, ffn1_w_up, ffn1_w_down, mix_norm, w_in, conv_w, a_log_fwd, a_log_bwd,
                    dt_bias_fwd, dt_bias_bwd, gdn_out_norm, q_norm, k_norm, attn_out_norm, w_out, ffn2_norm,
                    ffn2_w_gate, ffn2_w_up, ffn2_w_down, final_norm):
    w_in0 = w_in[0]
    att0 = GDN_PROJ_W + GATE_COLS
    w_ab = jnp.pad(w_in0[:, GDN_PROJ_W:att0], ((0, 0), (0, LANES - GATE_COLS)))
    return dict(
        ffn1_norm=ffn1_norm[0].reshape(1, -1),
        ffn1_wg=ffn1_w_gate[0].astype(BF16), ffn1_wu=ffn1_w_up[0].astype(BF16), ffn1_wd=ffn1_w_down[0].astype(BF16),
        mix_norm=mix_norm[0].reshape(1, -1),
        w_in=w_in0.astype(BF16),
        w_qk=w_in0[:, att0:att0 + ATT_Q_W + ATT_KV_W].astype(BF16),
        w_vt=w_in0[:, att0 + ATT_Q_W + ATT_KV_W:].T.astype(BF16),
        w_ab=w_ab.astype(BF16),
        conv_w=jnp.pad(conv_w[0], ((0, SUBLANES - CONV_K), (0, 0))),
        alog=_pad_lanes(jnp.concatenate([a_log_fwd[0], a_log_bwd[0]])),
        dtb=_pad_lanes(jnp.concatenate([dt_bias_fwd[0], dt_bias_bwd[0]])),
        gdn_out_norm=gdn_out_norm[0].reshape(1, -1),
        q_norm=q_norm[0].reshape(1, -1), k_norm=k_norm[0].reshape(1, -1),
        attn_out_norm=attn_out_norm[0].reshape(1, -1),
        w_out=w_out[0].astype(BF16),
        ffn2_norm=ffn2_norm[0].reshape(1, -1),
        ffn2_wg=ffn2_w_gate[0].astype(BF16), ffn2_wu=ffn2_w_up[0].astype(BF16), ffn2_wd=ffn2_w_down[0].astype(BF16),
        final_norm=final_norm.reshape(1, -1),
    )


def kernel(x_prompt, x_sample, ffn1_norm, ffn1_w_gate, ffn1_w_up, ffn1_w_down, mix_norm, w_in, conv_w, a_log_fwd,
           a_log_bwd, dt_bias_fwd, dt_bias_bwd, gdn_out_norm, q_norm, k_norm, attn_out_norm, w_out, ffn2_norm,
           ffn2_w_gate, ffn2_w_up, ffn2_w_down, final_norm):
    assert x_prompt.shape[0] == 1 and x_sample.shape[0] == 1
    p = _prepare_params(ffn1_norm, ffn1_w_gate, ffn1_w_up, ffn1_w_down, mix_norm, w_in, conv_w, a_log_fwd,
                        a_log_bwd, dt_bias_fwd, dt_bias_bwd, gdn_out_norm, q_norm, k_norm, attn_out_norm, w_out,
                        ffn2_norm, ffn2_w_gate, ffn2_w_up, ffn2_w_down, final_norm)
    rope = _rope_tables(max(x_prompt.shape[1], x_sample.shape[1]))
    y_prompt = _encode(x_prompt[0], p, rope)
    y_sample = _encode(x_sample[0], p, rope)
    return (y_prompt[None], y_sample[None])
```

```python
import functools
import math

import jax
import jax.numpy as jnp
from jax import lax
from jax.experimental import pallas as pl
from jax.experimental.pallas import tpu as pltpu

D_MODEL = 2048
HEAD_DIM = 128
GDN_HEADS = 8
GDN_W = GDN_HEADS * HEAD_DIM
ATT_Q_HEADS = 8
ATT_KV_HEADS = 2
ATT_GROUP = ATT_Q_HEADS // ATT_KV_HEADS
ATT_Q_W = ATT_Q_HEADS * HEAD_DIM
ATT_KV_W = ATT_KV_HEADS * HEAD_DIM
ATT_W = ATT_Q_W + 2 * ATT_KV_W
GDN_PROJ_W = 4 * GDN_W
GATE_COLS = 4 * GDN_HEADS
D_FF = 5632
CONV_K = 5
GRID_W = 64
AXIS_DIM = HEAD_DIM // 2
ROPE_THETA = 10000.0
EPS = 1e-6

LANES = 128
SUBLANES = 8
SCAN_CHUNK = 128
PROJ_COL_TILE = 256
CONV_HALO = 16
INV_BASE_LEVELS = 2
INV_BASE = 2 ** INV_BASE_LEVELS
EXP2_SCALE = HEAD_DIM ** -0.5 * math.log2(math.e)
FLASH_ROW_BLOCK = 64
FLASH_SUM_ROWS = 16
FLASH_Q_BLOCK = 512

F32 = jnp.float32
BF16 = jnp.bfloat16

_VMEM_LIMIT = 56 * 1024 * 1024


def _cparams(semantics):
    return pltpu.CompilerParams(dimension_semantics=semantics, vmem_limit_bytes=_VMEM_LIMIT)


def _rms(x, g):
    return x * lax.rsqrt(jnp.mean(x * x, axis=-1, keepdims=True) + EPS) * g


def _dot(a, b):
    return jnp.dot(a, b, preferred_element_type=F32)


def _dot_nt(a, b):
    return lax.dot_general(a, b, (((1,), (1,)), ((), ())), preferred_element_type=F32)


def _split3(x):
    hi = x.astype(BF16)
    r = x - hi.astype(F32)
    mid = r.astype(BF16)
    lo = (r - mid.astype(F32)).astype(BF16)
    return hi, mid, lo


def _ffn_kernel(x_ref, g_ref, wg_ref, wu_ref, wd_ref, fg_ref, o_ref, xn_ref, *, final):
    j = pl.program_id(1)

    @pl.when(j == 0)
    def _():
        xn_ref[...] = _rms(x_ref[...], g_ref[...]).astype(BF16)
        o_ref[...] = jnp.zeros_like(o_ref)

    xn = xn_ref[...]
    gate = _dot(xn, wg_ref[...])
    up = _dot(xn, wu_ref[...])
    act = (gate * jax.nn.sigmoid(gate) * up).astype(BF16)
    o_ref[...] += _dot(act, wd_ref[...])

    @pl.when(j == pl.num_programs(1) - 1)
    def _():
        h = x_ref[...] + 0.5 * o_ref[...]
        if final:
            h = _rms(h, fg_ref[...])
        o_ref[...] = h


def _ffn(x, norm_g, wg, wu, wd, final_g, *, final, tm=512, tf=512):
    L = x.shape[0]
    assert L % tm == 0 and D_FF % tf == 0
    grid = (L // tm, D_FF // tf)
    return pl.pallas_call(
        functools.partial(_ffn_kernel, final=final),
        grid=grid,
        in_specs=[
            pl.BlockSpec((tm, D_MODEL), lambda i, j: (i, 0)),
            pl.BlockSpec((1, D_MODEL), lambda i, j: (0, 0)),
            pl.BlockSpec((D_MODEL, tf), lambda i, j: (0, j)),
            pl.BlockSpec((D_MODEL, tf), lambda i, j: (0, j)),
            pl.BlockSpec((tf, D_MODEL), lambda i, j: (j, 0)),
            pl.BlockSpec((1, D_MODEL), lambda i, j: (0, 0)),
        ],
        out_specs=pl.BlockSpec((tm, D_MODEL), lambda i, j: (i, 0)),
        out_shape=jax.ShapeDtypeStruct((L, D_MODEL), F32),
        scratch_shapes=[pltpu.VMEM((tm, D_MODEL), BF16)],
        compiler_params=_cparams(("parallel", "arbitrary")),
        name="ffn",
    )(x, norm_g, wg, wu, wd, final_g)


def _rope(x, cos, sin_lo, sin_hi):
    return (x * cos + pltpu.roll(x, AXIS_DIM // 2, axis=1) * sin_hi
            + pltpu.roll(x, HEAD_DIM - AXIS_DIM // 2, axis=1) * sin_lo)


def _mix_proj_kernel(x_ref, xprev_ref, xnext_ref, g_ref, wg_ref, w_ref, wvt_ref, wab_ref, qn_ref, kn_ref, cos_ref,
                     slo_ref, shi_ref, alog_ref, dtb_ref, cw_ref, qg_ref, kg_ref, vg_ref, z_ref, q_ref, k_ref,
                     vt_ref, gcb_ref, gct_ref, ext_ref, *, tm):
    i = pl.program_id(0)
    pad = CONV_K // 2
    g = g_ref[...]
    xn = _rms(x_ref[...], g).astype(BF16)

    xn_ext = jnp.concatenate([_rms(xprev_ref[...], g).astype(BF16), xn, _rms(xnext_ref[...], g).astype(BF16)], axis=0)
    keep_prev = jnp.where(i == 0, 0.0, 1.0)
    keep_next = jnp.where(i == pl.num_programs(0) - 1, 0.0, 1.0)
    cos = cos_ref[...]
    slo = slo_ref[...]
    shi = shi_ref[...]

    def project_ext(c):
        cs = slice(c, c + PROJ_COL_TILE)
        e = _dot(xn_ext, wg_ref[:, cs])
        ext_ref[0:CONV_HALO, cs] = e[0:CONV_HALO] * keep_prev
        ext_ref[CONV_HALO:CONV_HALO + tm, cs] = e[CONV_HALO:CONV_HALO + tm]
        ext_ref[CONV_HALO + tm:, cs] = e[CONV_HALO + tm:] * keep_next

    def project_z(c):
        z_ref[:, c:c + PROJ_COL_TILE] = _dot(xn, wg_ref[:, 3 * GDN_W + c:3 * GDN_W + c + PROJ_COL_TILE])

    def project_attn_pair(h):
        pair = _dot(xn, w_ref[:, h * HEAD_DIM:(h + 2) * HEAD_DIM])
        for half in range(2):
            head = pair[:, half * HEAD_DIM:(half + 1) * HEAD_DIM]
            n = h + half
            if n < ATT_Q_HEADS:
                q_ref[:, n * HEAD_DIM:(n + 1) * HEAD_DIM] = (
                    _rope(_rms(head, qn_ref[...]), cos, slo, shi) * EXP2_SCALE).astype(BF16)
            else:
                n -= ATT_Q_HEADS
                k_ref[:, n * HEAD_DIM:(n + 1) * HEAD_DIM] = _rope(_rms(head, kn_ref[...]), cos, slo, shi).astype(BF16)

    def project_vt():
        vt_ref[...] = _dot_nt(wvt_ref[...], xn).astype(BF16)

    def conv_slab(s):
        cols = slice(s * HEAD_DIM, (s + 1) * HEAD_DIM)
        y = None
        for t in range(CONV_K):
            term = ext_ref[CONV_HALO - pad + t:CONV_HALO - pad + t + tm, cols] * cw_ref[t:t + 1, cols]
            y = term if y is None else y + term
        y = y * jax.nn.sigmoid(y)
        part, h = divmod(s, GDN_HEADS)
        hc = slice(h * HEAD_DIM, (h + 1) * HEAD_DIM)
        if part == 0:
            qg_ref[:, hc] = y * lax.rsqrt(jnp.sum(y * y, axis=-1, keepdims=True) + EPS) * (HEAD_DIM ** -0.5)
        elif part == 1:
            kg_ref[:, hc] = y * lax.rsqrt(jnp.sum(y * y, axis=-1, keepdims=True) + EPS)
        else:
            vg_ref[:, hc] = y

    tiles = range(0, GDN_W, PROJ_COL_TILE)
    mxu_q = [functools.partial(project_ext, c) for c in tiles]
    mxu_k = [functools.partial(project_ext, GDN_W + c) for c in tiles]
    mxu_v = [functools.partial(project_ext, 2 * GDN_W + c) for c in tiles]
    mxu_z = [functools.partial(project_z, c) for c in tiles]
    mxu_att = [functools.partial(project_attn_pair, h) for h in range(0, ATT_Q_HEADS + ATT_KV_HEADS, 2)]
    mxu_att.append(project_vt)
    for piece in mxu_q:
        piece()
    n_att = len(mxu_att) // 2
    stages = [(mxu_k + mxu_att[:n_att], range(0, GDN_HEADS)),
              (mxu_v + mxu_att[n_att:] + mxu_z[:2], range(GDN_HEADS, 2 * GDN_HEADS)),
              (mxu_z[2:], range(2 * GDN_HEADS, 3 * GDN_HEADS))]
    for pieces, slabs in stages:
        for n in range(max(len(pieces), len(slabs))):
            if n < len(pieces):
                pieces[n]()
            if n < len(slabs):
                conv_slab(slabs[n])

    ab = _dot(xn, wab_ref[...])
    t = ab + dtb_ref[...]
    softplus = jnp.maximum(t, 0.0) + jnp.log1p(jnp.exp(-jnp.abs(t)))
    log_decay = -jnp.exp(alog_ref[...]) * softplus
    lane = lax.broadcasted_iota(jnp.int32, ab.shape, 1)
    gb = jnp.where(lane < 2 * GDN_HEADS, log_decay, jax.nn.sigmoid(ab))

    r = lax.broadcasted_iota(jnp.int32, (tm, tm), 0)
    c = lax.broadcasted_iota(jnp.int32, (tm, tm), 1)
    same = (r // SCAN_CHUNK) == (c // SCAN_CHUNK)
    m_lo = jnp.where(same & (c <= r), 1.0, 0.0).astype(BF16)
    m_up = jnp.where(same & (c >= r), 1.0, 0.0).astype(BF16)
    parts = _split3(gb)
    gc_f = _dot(m_lo, parts[0]) + _dot(m_lo, parts[1]) + _dot(m_lo, parts[2])
    gc_b = _dot(m_up, parts[0]) + _dot(m_up, parts[1]) + _dot(m_up, parts[2])
    gcb = jnp.where(lane < GDN_HEADS, gc_f, jnp.where(lane < 2 * GDN_HEADS, gc_b, gb))
    gcb_ref[...] = gcb
    for n in range(tm // SCAN_CHUNK):
        gct = gcb[n * SCAN_CHUNK:(n + 1) * SCAN_CHUNK, :].T
        gct_ref[n * 2 * GDN_HEADS:(n + 1) * 2 * GDN_HEADS, :] = gct[0:2 * GDN_HEADS, :]


def _mix_proj(x, norm_g, w_in, w_qk, w_vt, w_ab, q_norm, k_norm, cos, sin_lo, sin_hi, alog, dtb, conv_w8, *, tm=256):
    L = x.shape[0]
    assert L % tm == 0 and tm % SCAN_CHUNK == 0 and tm % CONV_HALO == 0
    n_halo = L // CONV_HALO
    per = tm // CONV_HALO
    row = lambda i: (i, 0)
    fixed = lambda i: (0, 0)
    resident = pl.Buffered(1)
    return pl.pallas_call(
        functools.partial(_mix_proj_kernel, tm=tm),
        grid=(L // tm,),
        in_specs=[
            pl.BlockSpec((tm, D_MODEL), row),
            pl.BlockSpec((CONV_HALO, D_MODEL), lambda i: (jnp.maximum(i * per - 1, 0), 0)),
            pl.BlockSpec((CONV_HALO, D_MODEL), lambda i: (jnp.minimum((i + 1) * per, n_halo - 1), 0)),
            pl.BlockSpec((1, D_MODEL), fixed),
            pl.BlockSpec((D_MODEL, GDN_PROJ_W), fixed, pipeline_mode=resident),
            pl.BlockSpec((D_MODEL, ATT_Q_W + ATT_KV_W), fixed, pipeline_mode=resident),
            pl.BlockSpec((ATT_KV_W, D_MODEL), fixed, pipeline_mode=resident),
            pl.BlockSpec((D_MODEL, LANES), fixed),
            pl.BlockSpec((1, HEAD_DIM), fixed),
            pl.BlockSpec((1, HEAD_DIM), fixed),
            pl.BlockSpec((tm, HEAD_DIM), row),
            pl.BlockSpec((tm, HEAD_DIM), row),
            pl.BlockSpec((tm, HEAD_DIM), row),
            pl.BlockSpec((1, LANES), fixed),
            pl.BlockSpec((1, LANES), fixed),
            pl.BlockSpec((SUBLANES, 3 * GDN_W), fixed),
        ],
        out_specs=[
            pl.BlockSpec((tm, GDN_W), row),
            pl.BlockSpec((tm, GDN_W), row),
            pl.BlockSpec((tm, GDN_W), row),
            pl.BlockSpec((tm, GDN_W), row),
            pl.BlockSpec((tm, ATT_Q_W), row),
            pl.BlockSpec((tm, ATT_KV_W), row),
            pl.BlockSpec((ATT_KV_W, tm), lambda i: (0, i)),
            pl.BlockSpec((tm, LANES), row),
            pl.BlockSpec((tm // SCAN_CHUNK * 2 * GDN_HEADS, SCAN_CHUNK), row),
        ],
        out_shape=[
            jax.ShapeDtypeStruct((L, GDN_W), F32),
            jax.ShapeDtypeStruct((L, GDN_W), F32),
            jax.ShapeDtypeStruct((L, GDN_W), F32),
            jax.ShapeDtypeStruct((L, GDN_W), F32),
            jax.ShapeDtypeStruct((L, ATT_Q_W), BF16),
            jax.ShapeDtypeStruct((L, ATT_KV_W), BF16),
            jax.ShapeDtypeStruct((ATT_KV_W, L), BF16),
            jax.ShapeDtypeStruct((L, LANES), F32),
            jax.ShapeDtypeStruct((L // SCAN_CHUNK * 2 * GDN_HEADS, SCAN_CHUNK), F32),
        ],
        scratch_shapes=[pltpu.VMEM((tm + 2 * CONV_HALO, 3 * GDN_W), F32)],
        compiler_params=_cparams(("parallel",)),
        name="mix_proj",
    )(x, x, x, norm_g, w_in, w_qk, w_vt, w_ab, q_norm, k_norm, cos, sin_lo, sin_hi, alog, dtb, conv_w8)


def _gdn_scan_kernel(q_ref, k_ref, v_ref, gcb_ref, gct_ref, o_ref, s_ref):
    d = pl.program_id(0)
    C = SCAN_CHUNK

    @pl.when(pl.program_id(1) == 0)
    def _():
        s_ref[...] = jnp.zeros_like(s_ref)

    fwd = d == 0
    r = lax.broadcasted_iota(jnp.int32, (C, C), 0)
    c = lax.broadcasted_iota(jnp.int32, (C, C), 1)
    dif = jnp.where(fwd, r - c, c - r)
    m_incl = dif >= 0
    m_strict = dif > 0
    eye = jnp.where(dif == 0, 1.0, 0.0)

    gcb = gcb_ref[...]
    gc = jnp.where(fwd, gcb[:, 0:GDN_HEADS], gcb[:, GDN_HEADS:2 * GDN_HEADS])
    beta = jnp.where(fwd, gcb[:, 2 * GDN_HEADS:3 * GDN_HEADS], gcb[:, 3 * GDN_HEADS:4 * GDN_HEADS])
    g_end = jnp.where(fwd, gc[C - 1:C, :], gc[0:1, :])
    e_gc = jnp.exp(gc)
    e_rest = jnp.exp(g_end - gc)
    e_end = jnp.exp(g_end)
    gct = gct_ref[...]

    heads = range(GDN_HEADS)
    cols = [slice(h * HEAD_DIM, (h + 1) * HEAD_DIM) for h in heads]
    col1 = [slice(h, h + 1) for h in heads]

    k = [k_ref[:, cols[h]] for h in heads]
    k16 = [k[h].astype(BF16) for h in heads]
    kb = [k[h] * beta[:, col1[h]] for h in heads]
    q16 = [q_ref[:, cols[h]].astype(BF16) for h in heads]
    kq = [_dot_nt(jnp.concatenate([kb[h].astype(BF16), q16[h]], axis=0), k16[h]) for h in heads]
    decay = []
    for h in heads:
        diff = gc[:, col1[h]] - gct[col1[h], :]
        decay.append(jnp.where(m_incl, jnp.exp(jnp.where(m_incl, diff, 0.0)), 0.0))
    a = [jnp.where(m_strict, kq[h][0:C] * decay[h], 0.0) for h in heads]
    attn_qk16 = [(kq[h][C:2 * C] * decay[h]).astype(BF16) for h in heads]

    rb = r >> INV_BASE_LEVELS
    cb = c >> INV_BASE_LEVELS
    a_d = [jnp.where(rb == cb, a[h], 0.0) for h in heads]
    inv = [eye - a_d[h] for h in heads]
    a_d16 = [a_d[h].astype(BF16) for h in heads]
    a_pow16 = [_dot(a_d16[h], a_d16[h]).astype(BF16) for h in heads]
    for level in range(INV_BASE_LEVELS - 1):
        if level < INV_BASE_LEVELS - 2:
            both = [_dot(jnp.concatenate([inv[h].astype(BF16), a_pow16[h]], axis=0), a_pow16[h]) for h in heads]
            inv = [inv[h] + both[h][0:C] for h in heads]
            a_pow16 = [both[h][C:2 * C].astype(BF16) for h in heads]
        else:
            inv = [inv[h] + _dot(inv[h].astype(BF16), a_pow16[h]) for h in heads]
    b = INV_BASE
    while b < C:
        off = ((rb >> 1) == (cb >> 1)) & (rb != cb)
        a_off16 = [jnp.where(off, a[h], 0.0).astype(BF16) for h in heads]
        inv16 = [inv[h].astype(BF16) for h in heads]
        left = [_dot(inv16[h], a_off16[h]).astype(BF16) for h in heads]
        inv = [inv[h] - _dot(left[h], inv16[h]) for h in heads]
        rb = rb >> 1
        cb = cb >> 1
        b *= 2

    rhs16 = [jnp.concatenate([(v_ref[:, cols[h]] * beta[:, col1[h]]).astype(BF16),
                              (kb[h] * e_gc[:, col1[h]]).astype(BF16)], axis=1) for h in heads]
    uw = [_dot(inv[h].astype(BF16), rhs16[h]) for h in heads]
    q_dec16 = [(q_ref[:, cols[h]] * e_gc[:, col1[h]]).astype(BF16) for h in heads]
    k_dec_t16 = [(k[h] * e_rest[:, col1[h]]).T.astype(BF16) for h in heads]

    s = [s_ref[h] for h in heads]
    s16 = [s[h].astype(BF16) for h in heads]
    ws = [_dot(jnp.concatenate([uw[h][:, HEAD_DIM:].astype(BF16), q_dec16[h]], axis=0), s16[h]) for h in heads]
    v_new16 = [(uw[h][:, 0:HEAD_DIM] - ws[h][0:C]).astype(BF16) for h in heads]
    for h in heads:
        o_ref[:, cols[h]] = ws[h][C:2 * C] + _dot(attn_qk16[h], v_new16[h])
    for h in heads:
        s_ref[h] = s[h] * e_end[:, col1[h]] + _dot(k_dec_t16[h], v_new16[h])


def _gdn_scan(q, k, v, gcb, gct):
    L = q.shape[0]
    assert L % SCAN_CHUNK == 0
    n = L // SCAN_CHUNK
    chunk = lambda d, i: i + d * (n - 1 - 2 * i)
    row = lambda d, i: (chunk(d, i), 0)
    return pl.pallas_call(
        _gdn_scan_kernel,
        grid=(2, n),
        in_specs=[
            pl.BlockSpec((SCAN_CHUNK, GDN_W), row),
            pl.BlockSpec((SCAN_CHUNK, GDN_W), row),
            pl.BlockSpec((SCAN_CHUNK, GDN_W), row),
            pl.BlockSpec((SCAN_CHUNK, LANES), row),
            pl.BlockSpec((GDN_HEADS, SCAN_CHUNK), lambda d, i: (2 * chunk(d, i) + d, 0)),
        ],
        out_specs=pl.BlockSpec((None, SCAN_CHUNK, GDN_W), lambda d, i: (d, chunk(d, i), 0)),
        out_shape=jax.ShapeDtypeStruct((2, L, GDN_W), F32),
        scratch_shapes=[pltpu.VMEM((GDN_HEADS, HEAD_DIM, HEAD_DIM), F32)],
        compiler_params=_cparams(("arbitrary", "arbitrary")),
        name="gdn_scan",
    )(q, k, v, gcb, gct)


def _flash_kernel(q_ref, k_ref, vt_ref, on_ref, o_ref, m_ref, acc_ref, st_ref, p_ref, *, tk):
    ki = pl.program_id(2)

    @pl.when(ki == 0)
    def _():
        m_ref[...] = jnp.full_like(m_ref, -jnp.inf)
        acc_ref[...] = jnp.zeros_like(acc_ref)

    k = k_ref[...]
    vt1 = jnp.concatenate([vt_ref[...], jnp.ones((FLASH_SUM_ROWS, tk), BF16)], axis=0)
    blocks = [slice(r, r + FLASH_ROW_BLOCK) for r in range(0, tk, FLASH_ROW_BLOCK)]

    tq = q_ref.shape[0]
    units = [(h, slice(c, c + FLASH_Q_BLOCK)) for h in range(ATT_GROUP) for c in range(0, tq, FLASH_Q_BLOCK)]

    def scores(h, qs):
        st_ref[h, :, qs] = _dot_nt(k, q_ref[qs, h * HEAD_DIM:(h + 1) * HEAD_DIM])

    def softmax(h, qs):
        part = None
        for rows in blocks:
            x = st_ref[h, rows, qs]
            while x.shape[0] > SUBLANES:
                half = x.shape[0] // 2
                x = jnp.maximum(x[:half], x[half:])
            part = x if part is None else jnp.maximum(part, x)
        m_prev = m_ref[h, :, qs]
        m_new = jnp.maximum(m_prev, jnp.max(part, axis=0, keepdims=True))
        m_ref[h, :, qs] = m_new
        for rows in blocks:
            p_ref[h, rows, qs] = jnp.exp2(st_ref[h, rows, qs] - m_new).astype(BF16)
        return jnp.exp2(m_prev - m_new)

    def accumulate(h, qs, alpha):
        acc_ref[h, :, qs] = alpha * acc_ref[h, :, qs] + _dot(vt1, p_ref[h, :, qs])

    scores(*units[0])
    pending = None
    for u, unit in enumerate(units):
        if u + 1 < len(units):
            scores(*units[u + 1])
        alpha = softmax(*unit)
        if pending is not None:
            accumulate(*pending)
        pending = (*unit, alpha)
    accumulate(*pending)

    @pl.when(ki == pl.num_programs(2) - 1)
    def _():
        for h in range(ATT_GROUP):
            cols = slice(h * HEAD_DIM, (h + 1) * HEAD_DIM)
            o = (acc_ref[h, 0:HEAD_DIM, :] / acc_ref[h, HEAD_DIM:HEAD_DIM + 1, :]).T
            o_ref[:, cols] = _rms(o, on_ref[...]).astype(BF16)


def _flash_attn(q, k, vt, out_norm, *, tq=512, tk=2048):
    L = q.shape[0]
    assert L % tq == 0 and L % tk == 0 and tq % FLASH_Q_BLOCK == 0 and tk % FLASH_ROW_BLOCK == 0
    gw = ATT_GROUP * HEAD_DIM
    return pl.pallas_call(
        functools.partial(_flash_kernel, tk=tk),
        grid=(ATT_KV_HEADS, L // tq, L // tk),
        in_specs=[
            pl.BlockSpec((tq, gw), lambda g, i, j: (i, g)),
            pl.BlockSpec((tk, HEAD_DIM), lambda g, i, j: (j, g)),
            pl.BlockSpec((HEAD_DIM, tk), lambda g, i, j: (g, j)),
            pl.BlockSpec((1, HEAD_DIM), lambda g, i, j: (0, 0)),
        ],
        out_specs=pl.BlockSpec((tq, gw), lambda g, i, j: (i, g)),
        out_shape=jax.ShapeDtypeStruct((L, ATT_Q_W), BF16),
        scratch_shapes=[
            pltpu.VMEM((ATT_GROUP, 1, tq), F32),
            pltpu.VMEM((ATT_GROUP, HEAD_DIM + FLASH_SUM_ROWS, tq), F32),
            pltpu.VMEM((ATT_GROUP, tk, tq), F32),
            pltpu.VMEM((ATT_GROUP, tk, tq), BF16),
        ],
        compiler_params=_cparams(("parallel", "parallel", "arbitrary")),
        name="flash_attn",
    )(q, k, vt, out_norm)


def _out_proj_kernel(og_ref, z_ref, oa_ref, h_ref, gn_ref, w_ref, o_ref, mix_ref):
    att = _dot(oa_ref[...], w_ref[GDN_W:, :])
    for h in range(GDN_HEADS):
        cols = slice(h * HEAD_DIM, (h + 1) * HEAD_DIM)
        o = _rms(og_ref[0, :, cols] + og_ref[1, :, cols], gn_ref[...])
        z = z_ref[:, cols]
        mix_ref[:, cols] = (o * (z * jax.nn.sigmoid(z))).astype(BF16)
    o_ref[...] = h_ref[...] + (att + _dot(mix_ref[...], w_ref[0:GDN_W, :]))


def _out_proj(og, z, oa, h1, gdn_norm, w_out, *, tm=512):
    L = h1.shape[0]
    assert L % tm == 0
    row = lambda i: (i, 0)
    return pl.pallas_call(
        _out_proj_kernel,
        grid=(L // tm,),
        in_specs=[
            pl.BlockSpec((2, tm, GDN_W), lambda i: (0, i, 0)),
            pl.BlockSpec((tm, GDN_W), row),
            pl.BlockSpec((tm, ATT_Q_W), row),
            pl.BlockSpec((tm, D_MODEL), row),
            pl.BlockSpec((1, HEAD_DIM), lambda i: (0, 0)),
            pl.BlockSpec((GDN_W + ATT_Q_W, D_MODEL), lambda i: (0, 0)),
        ],
        out_specs=pl.BlockSpec((tm, D_MODEL), row),
        out_shape=jax.ShapeDtypeStruct((L, D_MODEL), F32),
        scratch_shapes=[pltpu.VMEM((tm, GDN_W), BF16)],
        compiler_params=_cparams(("parallel",)),
        name="out_proj",
    )(og, z, oa, h1, gdn_norm, w_out)


def _rope_tables(L):
    assert L % GRID_W == 0
    rows = L // GRID_W
    freqs = ROPE_THETA ** (-jnp.arange(0, AXIS_DIM, 2, dtype=F32) / AXIS_DIM)
    ang_r = jnp.arange(rows, dtype=F32)[:, None] * freqs[None, :]
    ang_c = jnp.arange(GRID_W, dtype=F32)[:, None] * freqs[None, :]
    per_row = lambda tab: jnp.repeat(tab, GRID_W, axis=0)
    per_col = lambda tab: jnp.tile(tab, (rows, 1))
    cos_r, sin_r = per_row(jnp.cos(ang_r)), per_row(jnp.sin(ang_r))
    cos_c, sin_c = per_col(jnp.cos(ang_c)), per_col(jnp.sin(ang_c))
    zero = jnp.zeros_like(cos_r)
    cos = jnp.concatenate([cos_r, cos_r, cos_c, cos_c], axis=-1)
    sin_lo = jnp.concatenate([-sin_r, zero, -sin_c, zero], axis=-1)
    sin_hi = jnp.concatenate([zero, sin_r, zero, sin_c], axis=-1)
    return cos, sin_lo, sin_hi


def _pad_lanes(x):
    x = x.reshape(1, -1)
    return jnp.pad(x, ((0, 0), (0, LANES - x.shape[1])))


def _encode(x, p, rope):
    h1 = _ffn(x, p["ffn1_norm"], p["ffn1_wg"], p["ffn1_wu"], p["ffn1_wd"], p["final_norm"], final=False)
    cos, sin_lo, sin_hi = rope
    qg, kg, vg, z, qa, ka, va, gcb, gct = _mix_proj(h1, p["mix_norm"], p["w_in"], p["w_qk"], p["w_vt"], p["w_ab"],
                                                    p["q_norm"], p["k_norm"], cos, sin_lo, sin_hi, p["alog"],
                                                    p["dtb"], p["conv_w"])
    og = _gdn_scan(qg, kg, vg, gcb, gct)
    oa = _flash_attn(qa, ka, va, p["attn_out_norm"])
    h2 = _out_proj(og, z, oa, h1, p["gdn_out_norm"], p["w_out"])
    return _ffn(h2, p["ffn2_norm"], p["ffn2_wg"], p["ffn2_wu"], p["ffn2_wd"], p["final_norm"], final=True)


def _prepare_params(ffn1_norm, ffn1_w_gate, ffn1_w_up, ffn1_w_down, mix_norm, w_in, conv_w, a_log_fwd, a_log_bwd,
                    dt_bias_fwd, dt_bias_bwd, gdn_out_norm, q_norm, k_norm, attn_out_norm, w_out, ffn2_norm,
                    ffn2_w_gate, ffn2_w_up, ffn2_w_down, final_norm):
    w_in0 = w_in[0]
    att0 = GDN_PROJ_W + GATE_COLS
    w_ab = jnp.pad(w_in0[:, GDN_PROJ_W:att0], ((0, 0), (0, LANES - GATE_COLS)))
    return dict(
        ffn1_norm=ffn1_norm[0].reshape(1, -1),
        ffn1_wg=ffn1_w_gate[0].astype(BF16), ffn1_wu=ffn1_w_up[0].astype(BF16), ffn1_wd=ffn1_w_down[0].astype(BF16),
        mix_norm=mix_norm[0].reshape(1, -1),
        w_in=w_in0[:, :GDN_PROJ_W].astype(BF16),
        w_qk=w_in0[:, att0:att0 + ATT_Q_W + ATT_KV_W].astype(BF16),
        w_vt=w_in0[:, att0 + ATT_Q_W + ATT_KV_W:].T.astype(BF16),
        w_ab=w_ab.astype(BF16),
        conv_w=jnp.pad(conv_w[0], ((0, SUBLANES - CONV_K), (0, 0))),
        alog=_pad_lanes(jnp.concatenate([a_log_fwd[0], a_log_bwd[0]])),
        dtb=_pad_lanes(jnp.concatenate([dt_bias_fwd[0], dt_bias_bwd[0]])),
        gdn_out_norm=gdn_out_norm[0].reshape(1, -1),
        q_norm=q_norm[0].reshape(1, -1), k_norm=k_norm[0].reshape(1, -1),
        attn_out_norm=attn_out_norm[0].reshape(1, -1),
        w_out=w_out[0].astype(BF16),
        ffn2_norm=ffn2_norm[0].reshape(1, -1),
        ffn2_wg=ffn2_w_gate[0].astype(BF16), ffn2_wu=ffn2_w_up[0].astype(BF16), ffn2_wd=ffn2_w_down[0].astype(BF16),
        final_norm=final_norm.reshape(1, -1),
    )


def kernel(x_prompt, x_sample, ffn1_norm, ffn1_w_gate, ffn1_w_up, ffn1_w_down, mix_norm, w_in, conv_w, a_log_fwd,
           a_log_bwd, dt_bias_fwd, dt_bias_bwd, gdn_out_norm, q_norm, k_norm, attn_out_norm, w_out, ffn2_norm,
           ffn2_w_gate, ffn2_w_up, ffn2_w_down, final_norm):
    assert x_prompt.shape[0] == 1 and x_sample.shape[0] == 1
    p = _prepare_params(ffn1_norm, ffn1_w_gate, ffn1_w_up, ffn1_w_down, mix_norm, w_in, conv_w, a_log_fwd,
                        a_log_bwd, dt_bias_fwd, dt_bias_bwd, gdn_out_norm, q_norm, k_norm, attn_out_norm, w_out,
                        ffn2_norm, ffn2_w_gate, ffn2_w_up, ffn2_w_down, final_norm)
    rope = _rope_tables(max(x_prompt.shape[1], x_sample.shape[1]))
    y_prompt = _encode(x_prompt[0], p, rope)
    y_sample = _encode(x_sample[0], p, rope)
    return (y_prompt[None], y_sample[None])
```

```python
import functools
import math

import jax
import jax.numpy as jnp
from jax import lax
from jax.experimental import pallas as pl
from jax.experimental.pallas import tpu as pltpu

D_MODEL = 2048
HEAD_DIM = 128
GDN_HEADS = 8
GDN_W = GDN_HEADS * HEAD_DIM
ATT_Q_HEADS = 8
ATT_KV_HEADS = 2
ATT_GROUP = ATT_Q_HEADS // ATT_KV_HEADS
ATT_Q_W = ATT_Q_HEADS * HEAD_DIM
ATT_KV_W = ATT_KV_HEADS * HEAD_DIM
ATT_W = ATT_Q_W + 2 * ATT_KV_W
GDN_PROJ_W = 4 * GDN_W
GATE_COLS = 4 * GDN_HEADS
D_FF = 5632
CONV_K = 5
GRID_W = 64
AXIS_DIM = HEAD_DIM // 2
ROPE_THETA = 10000.0
EPS = 1e-6

LANES = 128
SUBLANES = 8
SCAN_CHUNK = 128
PROJ_COL_TILE = 256
CONV_HALO = 16
INV_BASE_LEVELS = 2
INV_BASE = 2 ** INV_BASE_LEVELS
EXP2_SCALE = HEAD_DIM ** -0.5 * math.log2(math.e)
FLASH_ROW_BLOCK = 64
FLASH_SUM_ROWS = 16
FLASH_Q_BLOCK = 512

F32 = jnp.float32
BF16 = jnp.bfloat16

_VMEM_LIMIT = 56 * 1024 * 1024


def _cparams(semantics):
    return pltpu.CompilerParams(dimension_semantics=semantics, vmem_limit_bytes=_VMEM_LIMIT)


def _rms(x, g):
    return x * lax.rsqrt(jnp.mean(x * x, axis=-1, keepdims=True) + EPS) * g


def _dot(a, b):
    return jnp.dot(a, b, preferred_element_type=F32)


def _dot_nt(a, b):
    return lax.dot_general(a, b, (((1,), (1,)), ((), ())), preferred_element_type=F32)


def _split3(x):
    hi = x.astype(BF16)
    r = x - hi.astype(F32)
    mid = r.astype(BF16)
    lo = (r - mid.astype(F32)).astype(BF16)
    return hi, mid, lo


def _ffn_kernel(x_ref, g_ref, wg_ref, wu_ref, wd_ref, fg_ref, o_ref, xn_ref, *, final):
    j = pl.program_id(1)

    @pl.when(j == 0)
    def _():
        xn_ref[...] = _rms(x_ref[...], g_ref[...]).astype(BF16)
        o_ref[...] = jnp.zeros_like(o_ref)

    xn = xn_ref[...]
    gate = _dot(xn, wg_ref[...])
    up = _dot(xn, wu_ref[...])
    act = (gate * jax.nn.sigmoid(gate) * up).astype(BF16)
    o_ref[...] += _dot(act, wd_ref[...])

    @pl.when(j == pl.num_programs(1) - 1)
    def _():
        h = x_ref[...] + 0.5 * o_ref[...]
        if final:
            h = _rms(h, fg_ref[...])
        o_ref[...] = h


def _ffn(x, norm_g, wg, wu, wd, final_g, *, final, tm=512, tf=512):
    L = x.shape[0]
    assert L % tm == 0 and D_FF % tf == 0
    grid = (L // tm, D_FF // tf)
    return pl.pallas_call(
        functools.partial(_ffn_kernel, final=final),
        grid=grid,
        in_specs=[
            pl.BlockSpec((tm, D_MODEL), lambda i, j: (i, 0)),
            pl.BlockSpec((1, D_MODEL), lambda i, j: (0, 0)),
            pl.BlockSpec((D_MODEL, tf), lambda i, j: (0, j)),
            pl.BlockSpec((D_MODEL, tf), lambda i, j: (0, j)),
            pl.BlockSpec((tf, D_MODEL), lambda i, j: (j, 0)),
            pl.BlockSpec((1, D_MODEL), lambda i, j: (0, 0)),
        ],
        out_specs=pl.BlockSpec((tm, D_MODEL), lambda i, j: (i, 0)),
        out_shape=jax.ShapeDtypeStruct((L, D_MODEL), F32),
        scratch_shapes=[pltpu.VMEM((tm, D_MODEL), BF16)],
        compiler_params=_cparams(("parallel", "arbitrary")),
        name="ffn",
    )(x, norm_g, wg, wu, wd, final_g)


def _rope(x, cos, sin_lo, sin_hi):
    return (x * cos + pltpu.roll(x, AXIS_DIM // 2, axis=1) * sin_hi
            + pltpu.roll(x, HEAD_DIM - AXIS_DIM // 2, axis=1) * sin_lo)


def _mix_proj_kernel(x_ref, xprev_ref, xnext_ref, g_ref, wg_ref, w_ref, wvt_ref, wab_ref, qn_ref, kn_ref, cos_ref,
                     slo_ref, shi_ref, alog_ref, dtb_ref, cw_ref, qg_ref, kg_ref, vg_ref, z_ref, q_ref, k_ref,
                     vt_ref, gcb_ref, gct_ref, ext_ref, *, tm):
    i = pl.program_id(0)
    pad = CONV_K // 2
    g = g_ref[...]
    xn = _rms(x_ref[...], g).astype(BF16)

    xn_ext = jnp.concatenate([_rms(xprev_ref[...], g).astype(BF16), xn, _rms(xnext_ref[...], g).astype(BF16)], axis=0)
    keep_prev = jnp.where(i == 0, 0.0, 1.0)
    keep_next = jnp.where(i == pl.num_programs(0) - 1, 0.0, 1.0)
    cos = cos_ref[...]
    slo = slo_ref[...]
    shi = shi_ref[...]

    def project_ext(c):
        cs = slice(c, c + PROJ_COL_TILE)
        e = _dot(xn_ext, wg_ref[:, cs])
        ext_ref[0:CONV_HALO, cs] = e[0:CONV_HALO] * keep_prev
        ext_ref[CONV_HALO:CONV_HALO + tm, cs] = e[CONV_HALO:CONV_HALO + tm]
        ext_ref[CONV_HALO + tm:, cs] = e[CONV_HALO + tm:] * keep_next

    def project_z(c):
        z_ref[:, c:c + PROJ_COL_TILE] = _dot(xn, wg_ref[:, 3 * GDN_W + c:3 * GDN_W + c + PROJ_COL_TILE])

    def project_attn_pair(h):
        pair = _dot(xn, w_ref[:, h * HEAD_DIM:(h + 2) * HEAD_DIM])
        for half in range(2):
            head = pair[:, half * HEAD_DIM:(half + 1) * HEAD_DIM]
            n = h + half
            if n < ATT_Q_HEADS:
                q_ref[:, n * HEAD_DIM:(n + 1) * HEAD_DIM] = (
                    _rope(_rms(head, qn_ref[...]), cos, slo, shi) * EXP2_SCALE).astype(BF16)
            else:
                n -= ATT_Q_HEADS
                k_ref[:, n * HEAD_DIM:(n + 1) * HEAD_DIM] = _rope(_rms(head, kn_ref[...]), cos, slo, shi).astype(BF16)

    def project_vt():
        vt_ref[...] = _dot_nt(wvt_ref[...], xn).astype(BF16)

    def conv_slab(s):
        cols = slice(s * HEAD_DIM, (s + 1) * HEAD_DIM)
        y = None
        for t in range(CONV_K):
            term = ext_ref[CONV_HALO - pad + t:CONV_HALO - pad + t + tm, cols] * cw_ref[t:t + 1, cols]
            y = term if y is None else y + term
        y = y * jax.nn.sigmoid(y)
        part, h = divmod(s, GDN_HEADS)
        hc = slice(h * HEAD_DIM, (h + 1) * HEAD_DIM)
        if part == 0:
            qg_ref[:, hc] = y * lax.rsqrt(jnp.sum(y * y, axis=-1, keepdims=True) + EPS) * (HEAD_DIM ** -0.5)
        elif part == 1:
            kg_ref[:, hc] = y * lax.rsqrt(jnp.sum(y * y, axis=-1, keepdims=True) + EPS)
        else:
            vg_ref[:, hc] = y

    tiles = range(0, GDN_W, PROJ_COL_TILE)
    mxu_q = [functools.partial(project_ext, c) for c in tiles]
    mxu_k = [functools.partial(project_ext, GDN_W + c) for c in tiles]
    mxu_v = [functools.partial(project_ext, 2 * GDN_W + c) for c in tiles]
    mxu_z = [functools.partial(project_z, c) for c in tiles]
    mxu_att = [functools.partial(project_attn_pair, h) for h in range(0, ATT_Q_HEADS + ATT_KV_HEADS, 2)]
    mxu_att.append(project_vt)
    for piece in mxu_q:
        piece()
    n_att = len(mxu_att) // 2
    stages = [(mxu_k + mxu_att[:n_att], range(0, GDN_HEADS)),
              (mxu_v + mxu_att[n_att:] + mxu_z[:2], range(GDN_HEADS, 2 * GDN_HEADS)),
              (mxu_z[2:], range(2 * GDN_HEADS, 3 * GDN_HEADS))]
    for pieces, slabs in stages:
        for n in range(max(len(pieces), len(slabs))):
            if n < len(pieces):
                pieces[n]()
            if n < len(slabs):
                conv_slab(slabs[n])

    ab = _dot(xn, wab_ref[...])
    t = ab + dtb_ref[...]
    softplus = jnp.maximum(t, 0.0) + jnp.log1p(jnp.exp(-jnp.abs(t)))
    log_decay = -jnp.exp(alog_ref[...]) * softplus
    lane = lax.broadcasted_iota(jnp.int32, ab.shape, 1)
    gb = jnp.where(lane < 2 * GDN_HEADS, log_decay, jax.nn.sigmoid(ab))

    r = lax.broadcasted_iota(jnp.int32, (tm, tm), 0)
    c = lax.broadcasted_iota(jnp.int32, (tm, tm), 1)
    same = (r // SCAN_CHUNK) == (c // SCAN_CHUNK)
    m_lo = jnp.where(same & (c <= r), 1.0, 0.0).astype(BF16)
    m_up = jnp.where(same & (c >= r), 1.0, 0.0).astype(BF16)
    parts = _split3(gb)
    gc_f = _dot(m_lo, parts[0]) + _dot(m_lo, parts[1]) + _dot(m_lo, parts[2])
    gc_b = _dot(m_up, parts[0]) + _dot(m_up, parts[1]) + _dot(m_up, parts[2])
    gcb = jnp.where(lane < GDN_HEADS, gc_f, jnp.where(lane < 2 * GDN_HEADS, gc_b, gb))
    gcb_ref[...] = gcb
    for n in range(tm // SCAN_CHUNK):
        gct = gcb[n * SCAN_CHUNK:(n + 1) * SCAN_CHUNK, :].T
        gct_ref[n * 2 * GDN_HEADS:(n + 1) * 2 * GDN_HEADS, :] = gct[0:2 * GDN_HEADS, :]


def _mix_proj(x, norm_g, w_in, w_qk, w_vt, w_ab, q_norm, k_norm, cos, sin_lo, sin_hi, alog, dtb, conv_w8, *, tm=256):
    L = x.shape[0]
    assert L % tm == 0 and tm % SCAN_CHUNK == 0 and tm % CONV_HALO == 0
    n_halo = L // CONV_HALO
    per = tm // CONV_HALO
    row = lambda i: (i, 0)
    fixed = lambda i: (0, 0)
    resident = pl.Buffered(1)
    return pl.pallas_call(
        functools.partial(_mix_proj_kernel, tm=tm),
        grid=(L // tm,),
        in_specs=[
            pl.BlockSpec((tm, D_MODEL), row),
            pl.BlockSpec((CONV_HALO, D_MODEL), lambda i: (jnp.maximum(i * per - 1, 0), 0)),
            pl.BlockSpec((CONV_HALO, D_MODEL), lambda i: (jnp.minimum((i + 1) * per, n_halo - 1), 0)),
            pl.BlockSpec((1, D_MODEL), fixed),
            pl.BlockSpec((D_MODEL, GDN_PROJ_W), fixed, pipeline_mode=resident),
            pl.BlockSpec((D_MODEL, ATT_Q_W + ATT_KV_W), fixed, pipeline_mode=resident),
            pl.BlockSpec((ATT_KV_W, D_MODEL), fixed, pipeline_mode=resident),
            pl.BlockSpec((D_MODEL, LANES), fixed),
            pl.BlockSpec((1, HEAD_DIM), fixed),
            pl.BlockSpec((1, HEAD_DIM), fixed),
            pl.BlockSpec((tm, HEAD_DIM), row),
            pl.BlockSpec((tm, HEAD_DIM), row),
            pl.BlockSpec((tm, HEAD_DIM), row),
            pl.BlockSpec((1, LANES), fixed),
            pl.BlockSpec((1, LANES), fixed),
            pl.BlockSpec((SUBLANES, 3 * GDN_W), fixed),
        ],
        out_specs=[
            pl.BlockSpec((tm, GDN_W), row),
            pl.BlockSpec((tm, GDN_W), row),
            pl.BlockSpec((tm, GDN_W), row),
            pl.BlockSpec((tm, GDN_W), row),
            pl.BlockSpec((tm, ATT_Q_W), row),
            pl.BlockSpec((tm, ATT_KV_W), row),
            pl.BlockSpec((ATT_KV_W, tm), lambda i: (0, i)),
            pl.BlockSpec((tm, LANES), row),
            pl.BlockSpec((tm // SCAN_CHUNK * 2 * GDN_HEADS, SCAN_CHUNK), row),
        ],
        out_shape=[
            jax.ShapeDtypeStruct((L, GDN_W), F32),
            jax.ShapeDtypeStruct((L, GDN_W), F32),
            jax.ShapeDtypeStruct((L, GDN_W), F32),
            jax.ShapeDtypeStruct((L, GDN_W), F32),
            jax.ShapeDtypeStruct((L, ATT_Q_W), BF16),
            jax.ShapeDtypeStruct((L, ATT_KV_W), BF16),
            jax.ShapeDtypeStruct((ATT_KV_W, L), BF16),
            jax.ShapeDtypeStruct((L, LANES), F32),
            jax.ShapeDtypeStruct((L // SCAN_CHUNK * 2 * GDN_HEADS, SCAN_CHUNK), F32),
        ],
        scratch_shapes=[pltpu.VMEM((tm + 2 * CONV_HALO, 3 * GDN_W), F32)],
        compiler_params=_cparams(("parallel",)),
        name="mix_proj",
    )(x, x, x, norm_g, w_in, w_qk, w_vt, w_ab, q_norm, k_norm, cos, sin_lo, sin_hi, alog, dtb, conv_w8)


def _gdn_scan_kernel(qf_ref, kf_ref, vf_ref, gcbf_ref, gctf_ref, qb_ref, kb_ref, vb_ref, gcbb_ref, gctb_ref,
                     of_ref, ob_ref, s_ref):
    C = SCAN_CHUNK

    @pl.when(pl.program_id(0) == 0)
    def _():
        s_ref[...] = jnp.zeros_like(s_ref)

    r = lax.broadcasted_iota(jnp.int32, (C, C), 0)
    c = lax.broadcasted_iota(jnp.int32, (C, C), 1)
    eye = jnp.where(r == c, 1.0, 0.0)

    chains = []
    for fwd, (q_ref, k_ref, v_ref, gcb_ref, gct_ref, o_ref) in (
            (True, (qf_ref, kf_ref, vf_ref, gcbf_ref, gctf_ref, of_ref)),
            (False, (qb_ref, kb_ref, vb_ref, gcbb_ref, gctb_ref, ob_ref))):
        dif = r - c if fwd else c - r
        lane0 = 0 if fwd else GDN_HEADS
        gcb = gcb_ref[...]
        gc = gcb[:, lane0:lane0 + GDN_HEADS]
        beta = gcb[:, 2 * GDN_HEADS + lane0:3 * GDN_HEADS + lane0]
        g_end = gc[C - 1:C, :] if fwd else gc[0:1, :]
        shared = dict(q_ref=q_ref, k_ref=k_ref, v_ref=v_ref, o_ref=o_ref, m_incl=dif >= 0, m_strict=dif > 0,
                      gc=gc, beta=beta, e_gc=jnp.exp(gc), e_rest=jnp.exp(g_end - gc), e_end=jnp.exp(g_end),
                      gct=gct_ref[...])
        for h in range(GDN_HEADS):
            chains.append(dict(shared, cols=slice(h * HEAD_DIM, (h + 1) * HEAD_DIM), col1=slice(h, h + 1),
                               state=lane0 + h))
    heads = range(len(chains))
    cols = [ch["cols"] for ch in chains]
    col1 = [ch["col1"] for ch in chains]
    beta = [ch["beta"] for ch in chains]
    e_gc = [ch["e_gc"] for ch in chains]

    k = [chains[h]["k_ref"][:, cols[h]] for h in heads]
    k16 = [k[h].astype(BF16) for h in heads]
    kb = [k[h] * beta[h][:, col1[h]] for h in heads]
    q16 = [chains[h]["q_ref"][:, cols[h]].astype(BF16) for h in heads]
    kq = [_dot_nt(jnp.concatenate([kb[h].astype(BF16), q16[h]], axis=0), k16[h]) for h in heads]
    decay = []
    for ch in chains:
        diff = ch["gc"][:, ch["col1"]] - ch["gct"][ch["col1"], :]
        decay.append(jnp.where(ch["m_incl"], jnp.exp(jnp.where(ch["m_incl"], diff, 0.0)), 0.0))
    a = [jnp.where(chains[h]["m_strict"], kq[h][0:C] * decay[h], 0.0) for h in heads]
    attn_qk16 = [(kq[h][C:2 * C] * decay[h]).astype(BF16) for h in heads]

    rb = r >> INV_BASE_LEVELS
    cb = c >> INV_BASE_LEVELS
    a_d = [jnp.where(rb == cb, a[h], 0.0) for h in heads]
    inv = [eye - a_d[h] for h in heads]
    a_d16 = [a_d[h].astype(BF16) for h in heads]
    a_pow16 = [_dot(a_d16[h], a_d16[h]).astype(BF16) for h in heads]
    for level in range(INV_BASE_LEVELS - 1):
        if level < INV_BASE_LEVELS - 2:
            both = [_dot(jnp.concatenate([inv[h].astype(BF16), a_pow16[h]], axis=0), a_pow16[h]) for h in heads]
            inv = [inv[h] + both[h][0:C] for h in heads]
            a_pow16 = [both[h][C:2 * C].astype(BF16) for h in heads]
        else:
            inv = [inv[h] + _dot(inv[h].astype(BF16), a_pow16[h]) for h in heads]
    b = INV_BASE
    while b < C:
        off = ((rb >> 1) == (cb >> 1)) & (rb != cb)
        a_off16 = [jnp.where(off, a[h], 0.0).astype(BF16) for h in heads]
        inv16 = [inv[h].astype(BF16) for h in heads]
        left = [_dot(inv16[h], a_off16[h]).astype(BF16) for h in heads]
        inv = [inv[h] - _dot(left[h], inv16[h]) for h in heads]
        rb = rb >> 1
        cb = cb >> 1
        b *= 2

    rhs16 = [jnp.concatenate([(chains[h]["v_ref"][:, cols[h]] * beta[h][:, col1[h]]).astype(BF16),
                              (kb[h] * e_gc[h][:, col1[h]]).astype(BF16)], axis=1) for h in heads]
    uw = [_dot(inv[h].astype(BF16), rhs16[h]) for h in heads]
    q_dec16 = [(chains[h]["q_ref"][:, cols[h]] * e_gc[h][:, col1[h]]).astype(BF16) for h in heads]
    k_dec_t16 = [(k[h] * chains[h]["e_rest"][:, col1[h]]).T.astype(BF16) for h in heads]

    s = [s_ref[ch["state"]] for ch in chains]
    s16 = [s[h].astype(BF16) for h in heads]
    ws = [_dot(jnp.concatenate([uw[h][:, HEAD_DIM:].astype(BF16), q_dec16[h]], axis=0), s16[h]) for h in heads]
    v_new16 = [(uw[h][:, 0:HEAD_DIM] - ws[h][0:C]).astype(BF16) for h in heads]
    for h, ch in enumerate(chains):
        ch["o_ref"][:, cols[h]] = ws[h][C:2 * C] + _dot(attn_qk16[h], v_new16[h])
    for h, ch in enumerate(chains):
        s_ref[ch["state"]] = s[h] * ch["e_end"][:, col1[h]] + _dot(k_dec_t16[h], v_new16[h])


def _gdn_scan(q, k, v, gcb, gct):
    L = q.shape[0]
    assert L % SCAN_CHUNK == 0
    n = L // SCAN_CHUNK
    fwd = lambda i: (i, 0)
    bwd = lambda i: (n - 1 - i, 0)
    tile_f = pl.BlockSpec((SCAN_CHUNK, GDN_W), fwd)
    tile_b = pl.BlockSpec((SCAN_CHUNK, GDN_W), bwd)
    return pl.pallas_call(
        _gdn_scan_kernel,
        grid=(n,),
        in_specs=[
            tile_f, tile_f, tile_f,
            pl.BlockSpec((SCAN_CHUNK, LANES), fwd),
            pl.BlockSpec((GDN_HEADS, SCAN_CHUNK), lambda i: (2 * i, 0)),
            tile_b, tile_b, tile_b,
            pl.BlockSpec((SCAN_CHUNK, LANES), bwd),
            pl.BlockSpec((GDN_HEADS, SCAN_CHUNK), lambda i: (2 * (n - 1 - i) + 1, 0)),
        ],
        out_specs=[tile_f, tile_b],
        out_shape=[jax.ShapeDtypeStruct((L, GDN_W), F32), jax.ShapeDtypeStruct((L, GDN_W), F32)],
        scratch_shapes=[pltpu.VMEM((2 * GDN_HEADS, HEAD_DIM, HEAD_DIM), F32)],
        compiler_params=_cparams(("arbitrary",)),
        name="gdn_scan",
    )(q, k, v, gcb, gct, q, k, v, gcb, gct)


def _flash_kernel(q_ref, k_ref, vt_ref, on_ref, o_ref, m_ref, acc_ref, st_ref, p_ref, *, tk):
    ki = pl.program_id(2)

    @pl.when(ki == 0)
    def _():
        m_ref[...] = jnp.full_like(m_ref, -jnp.inf)
        acc_ref[...] = jnp.zeros_like(acc_ref)

    k = k_ref[...]
    vt1 = jnp.concatenate([vt_ref[...], jnp.ones((FLASH_SUM_ROWS, tk), BF16)], axis=0)
    blocks = [slice(r, r + FLASH_ROW_BLOCK) for r in range(0, tk, FLASH_ROW_BLOCK)]

    tq = q_ref.shape[0]
    units = [(h, slice(c, c + FLASH_Q_BLOCK)) for h in range(ATT_GROUP) for c in range(0, tq, FLASH_Q_BLOCK)]

    def scores(h, qs):
        st_ref[h, :, qs] = _dot_nt(k, q_ref[qs, h * HEAD_DIM:(h + 1) * HEAD_DIM])

    def softmax(h, qs):
        part = None
        for rows in blocks:
            x = st_ref[h, rows, qs]
            while x.shape[0] > SUBLANES:
                half = x.shape[0] // 2
                x = jnp.maximum(x[:half], x[half:])
            part = x if part is None else jnp.maximum(part, x)
        m_prev = m_ref[h, :, qs]
        m_new = jnp.maximum(m_prev, jnp.max(part, axis=0, keepdims=True))
        m_ref[h, :, qs] = m_new
        for rows in blocks:
            p_ref[h, rows, qs] = jnp.exp2(st_ref[h, rows, qs] - m_new).astype(BF16)
        return jnp.exp2(m_prev - m_new)

    def accumulate(h, qs, alpha):
        acc_ref[h, :, qs] = alpha * acc_ref[h, :, qs] + _dot(vt1, p_ref[h, :, qs])

    scores(*units[0])
    pending = None
    for u, unit in enumerate(units):
        if u + 1 < len(units):
            scores(*units[u + 1])
        alpha = softmax(*unit)
        if pending is not None:
            accumulate(*pending)
        pending = (*unit, alpha)
    accumulate(*pending)

    @pl.when(ki == pl.num_programs(2) - 1)
    def _():
        for h in range(ATT_GROUP):
            cols = slice(h * HEAD_DIM, (h + 1) * HEAD_DIM)
            o = (acc_ref[h, 0:HEAD_DIM, :] / acc_ref[h, HEAD_DIM:HEAD_DIM + 1, :]).T
            o_ref[:, cols] = _rms(o, on_ref[...]).astype(BF16)


def _flash_attn(q, k, vt, out_norm, *, tq=512, tk=2048):
    L = q.shape[0]
    assert L % tq == 0 and L % tk == 0 and tq % FLASH_Q_BLOCK == 0 and tk % FLASH_ROW_BLOCK == 0
    gw = ATT_GROUP * HEAD_DIM
    return pl.pallas_call(
        functools.partial(_flash_kernel, tk=tk),
        grid=(ATT_KV_HEADS, L // tq, L // tk),
        in_specs=[
            pl.BlockSpec((tq, gw), lambda g, i, j: (i, g)),
            pl.BlockSpec((tk, HEAD_DIM), lambda g, i, j: (j, g)),
            pl.BlockSpec((HEAD_DIM, tk), lambda g, i, j: (g, j)),
            pl.BlockSpec((1, HEAD_DIM), lambda g, i, j: (0, 0)),
        ],
        out_specs=pl.BlockSpec((tq, gw), lambda g, i, j: (i, g)),
        out_shape=jax.ShapeDtypeStruct((L, ATT_Q_W), BF16),
        scratch_shapes=[
            pltpu.VMEM((ATT_GROUP, 1, tq), F32),
            pltpu.VMEM((ATT_GROUP, HEAD_DIM + FLASH_SUM_ROWS, tq), F32),
            pltpu.VMEM((ATT_GROUP, tk, tq), F32),
            pltpu.VMEM((ATT_GROUP, tk, tq), BF16),
        ],
        compiler_params=_cparams(("parallel", "parallel", "arbitrary")),
        name="flash_attn",
    )(q, k, vt, out_norm)


def _out_proj_kernel(of_ref, ob_ref, z_ref, oa_ref, h_ref, gn_ref, w_ref, o_ref, mix_ref):
    att = _dot(oa_ref[...], w_ref[GDN_W:, :])
    for h in range(GDN_HEADS):
        cols = slice(h * HEAD_DIM, (h + 1) * HEAD_DIM)
        o = _rms(of_ref[:, cols] + ob_ref[:, cols], gn_ref[...])
        z = z_ref[:, cols]
        mix_ref[:, cols] = (o * (z * jax.nn.sigmoid(z))).astype(BF16)
    o_ref[...] = h_ref[...] + (att + _dot(mix_ref[...], w_ref[0:GDN_W, :]))


def _out_proj(o_fwd, o_bwd, z, oa, h1, gdn_norm, w_out, *, tm=512):
    L = h1.shape[0]
    assert L % tm == 0
    row = lambda i: (i, 0)
    return pl.pallas_call(
        _out_proj_kernel,
        grid=(L // tm,),
        in_specs=[
            pl.BlockSpec((tm, GDN_W), row),
            pl.BlockSpec((tm, GDN_W), row),
            pl.BlockSpec((tm, GDN_W), row),
            pl.BlockSpec((tm, ATT_Q_W), row),
            pl.BlockSpec((tm, D_MODEL), row),
            pl.BlockSpec((1, HEAD_DIM), lambda i: (0, 0)),
            pl.BlockSpec((GDN_W + ATT_Q_W, D_MODEL), lambda i: (0, 0)),
        ],
        out_specs=pl.BlockSpec((tm, D_MODEL), row),
        out_shape=jax.ShapeDtypeStruct((L, D_MODEL), F32),
        scratch_shapes=[pltpu.VMEM((tm, GDN_W), BF16)],
        compiler_params=_cparams(("parallel",)),
        name="out_proj",
    )(o_fwd, o_bwd, z, oa, h1, gdn_norm, w_out)


def _rope_tables(L):
    assert L % GRID_W == 0
    rows = L // GRID_W
    freqs = ROPE_THETA ** (-jnp.arange(0, AXIS_DIM, 2, dtype=F32) / AXIS_DIM)
    ang_r = jnp.arange(rows, dtype=F32)[:, None] * freqs[None, :]
    ang_c = jnp.arange(GRID_W, dtype=F32)[:, None] * freqs[None, :]
    per_row = lambda tab: jnp.repeat(tab, GRID_W, axis=0)
    per_col = lambda tab: jnp.tile(tab, (rows, 1))
    cos_r, sin_r = per_row(jnp.cos(ang_r)), per_row(jnp.sin(ang_r))
    cos_c, sin_c = per_col(jnp.cos(ang_c)), per_col(jnp.sin(ang_c))
    zero = jnp.zeros_like(cos_r)
    cos = jnp.concatenate([cos_r, cos_r, cos_c, cos_c], axis=-1)
    sin_lo = jnp.concatenate([-sin_r, zero, -sin_c, zero], axis=-1)
    sin_hi = jnp.concatenate([zero, sin_r, zero, sin_c], axis=-1)
    return cos, sin_lo, sin_hi


def _pad_lanes(x):
    x = x.reshape(1, -1)
    return jnp.pad(x, ((0, 0), (0, LANES - x.shape[1])))


def _encode(x, p, rope):
    h1 = _ffn(x, p["ffn1_norm"], p["ffn1_wg"], p["ffn1_wu"], p["ffn1_wd"], p["final_norm"], final=False)
    cos, sin_lo, sin_hi = rope
    qg, kg, vg, z, qa, ka, va, gcb, gct = _mix_proj(h1, p["mix_norm"], p["w_in"], p["w_qk"], p["w_vt"], p["w_ab"],
                                                    p["q_norm"], p["k_norm"], cos, sin_lo, sin_hi, p["alog"],
                                                    p["dtb"], p["conv_w"])
    o_fwd, o_bwd = _gdn_scan(qg, kg, vg, gcb, gct)
    oa = _flash_attn(qa, ka, va, p["attn_out_norm"])
    h2 = _out_proj(o_fwd, o_bwd, z, oa, h1, p["gdn_out_norm"], p["w_out"])
    return _ffn(h2, p["ffn2_norm"], p["ffn2_wg"], p["ffn2_wu"], p["ffn2_wd"], p["final_norm"], final=True)


def _prepare_params(ffn1_norm, ffn1_w_gate, ffn1_w_up, ffn1_w_down, mix_norm, w_in, conv_w, a_log_fwd, a_log_bwd,
                    dt_bias_fwd, dt_bias_bwd, gdn_out_norm, q_norm, k_norm, attn_out_norm, w_out, ffn2_norm,
                    ffn2_w_gate, ffn2_w_up, ffn2_w_down, final_norm):
    w_in0 = w_in[0]
    att0 = GDN_PROJ_W + GATE_COLS
    w_ab = jnp.pad(w_in0[:, GDN_PROJ_W:att0], ((0, 0), (0, LANES - GATE_COLS)))
    return dict(
        ffn1_norm=ffn1_norm[0].reshape(1, -1),
        ffn1_wg=ffn1_w_gate[0].astype(BF16), ffn1_wu=ffn1_w_up[0].astype(BF16), ffn1_wd=ffn1_w_down[0].astype(BF16),
        mix_norm=mix_norm[0].reshape(1, -1),
        w_in=w_in0[:, :GDN_PROJ_W].astype(BF16),
        w_qk=w_in0[:, att0:att0 + ATT_Q_W + ATT_KV_W].astype(BF16),
        w_vt=w_in0[:, att0 + ATT_Q_W + ATT_KV_W:].T.astype(BF16),
        w_ab=w_ab.astype(BF16),
        conv_w=jnp.pad(conv_w[0], ((0, SUBLANES - CONV_K), (0, 0))),
        alog=_pad_lanes(jnp.concatenate([a_log_fwd[0], a_log_bwd[0]])),
        dtb=_pad_lanes(jnp.concatenate([dt_bias_fwd[0], dt_bias_bwd[0]])),
        gdn_out_norm=gdn_out_norm[0].reshape(1, -1),
        q_norm=q_norm[0].reshape(1, -1), k_norm=k_norm[0].reshape(1, -1),
        attn_out_norm=attn_out_norm[0].reshape(1, -1),
        w_out=w_out[0].astype(BF16),
        ffn2_norm=ffn2_norm[0].reshape(1, -1),
        ffn2_wg=ffn2_w_gate[0].astype(BF16), ffn2_wu=ffn2_w_up[0].astype(BF16), ffn2_wd=ffn2_w_down[0].astype(BF16),
        final_norm=final_norm.reshape(1, -1),
    )


def kernel(x_prompt, x_sample, ffn1_norm, ffn1_w_gate, ffn1_w_up, ffn1_w_down, mix_norm, w_in, conv_w, a_log_fwd,
           a_log_bwd, dt_bias_fwd, dt_bias_bwd, gdn_out_norm, q_norm, k_norm, attn_out_norm, w_out, ffn2_norm,
           ffn2_w_gate, ffn2_w_up, ffn2_w_down, final_norm):
    assert x_prompt.shape[0] == 1 and x_sample.shape[0] == 1
    p = _prepare_params(ffn1_norm, ffn1_w_gate, ffn1_w_up, ffn1_w_down, mix_norm, w_in, conv_w, a_log_fwd,
                        a_log_bwd, dt_bias_fwd, dt_bias_bwd, gdn_out_norm, q_norm, k_norm, attn_out_norm, w_out,
                        ffn2_norm, ffn2_w_gate, ffn2_w_up, ffn2_w_down, final_norm)
    rope = _rope_tables(max(x_prompt.shape[1], x_sample.shape[1]))
    y_prompt = _encode(x_prompt[0], p, rope)
    y_sample = _encode(x_sample[0], p, rope)
    return (y_prompt[None], y_sample[None])
```

```python
import functools
import math

import jax
import jax.numpy as jnp
from jax import lax
from jax.experimental import pallas as pl
from jax.experimental.pallas import tpu as pltpu

D_MODEL = 2048
HEAD_DIM = 128
GDN_HEADS = 8
GDN_W = GDN_HEADS * HEAD_DIM
ATT_Q_HEADS = 8
ATT_KV_HEADS = 2
ATT_GROUP = ATT_Q_HEADS // ATT_KV_HEADS
ATT_Q_W = ATT_Q_HEADS * HEAD_DIM
ATT_KV_W = ATT_KV_HEADS * HEAD_DIM
ATT_W = ATT_Q_W + 2 * ATT_KV_W
GDN_PROJ_W = 4 * GDN_W
GATE_COLS = 4 * GDN_HEADS
D_FF = 5632
CONV_K = 5
GRID_W = 64
AXIS_DIM = HEAD_DIM // 2
ROPE_THETA = 10000.0
EPS = 1e-6

LANES = 128
SUBLANES = 8
SCAN_CHUNK = 128
PROJ_COL_TILE = 256
CONV_HALO = 16
INV_BASE_LEVELS = 2
INV_BASE = 2 ** INV_BASE_LEVELS
EXP2_SCALE = HEAD_DIM ** -0.5 * math.log2(math.e)
FLASH_ROW_BLOCK = 64
FLASH_SUM_ROWS = 16
FLASH_Q_BLOCK = 512

F32 = jnp.float32
BF16 = jnp.bfloat16

_VMEM_LIMIT = 56 * 1024 * 1024


def _cparams(semantics):
    return pltpu.CompilerParams(dimension_semantics=semantics, vmem_limit_bytes=_VMEM_LIMIT)


def _rms(x, g):
    return x * lax.rsqrt(jnp.mean(x * x, axis=-1, keepdims=True) + EPS) * g


def _dot(a, b):
    return jnp.dot(a, b, preferred_element_type=F32)


def _dot_nt(a, b):
    return lax.dot_general(a, b, (((1,), (1,)), ((), ())), preferred_element_type=F32)


def _split3(x):
    hi = x.astype(BF16)
    r = x - hi.astype(F32)
    mid = r.astype(BF16)
    lo = (r - mid.astype(F32)).astype(BF16)
    return hi, mid, lo


def _ffn_kernel(x_ref, g_ref, wg_ref, wu_ref, wd_ref, fg_ref, o_ref, xn_ref, *, final):
    j = pl.program_id(1)

    @pl.when(j == 0)
    def _():
        xn_ref[...] = _rms(x_ref[...], g_ref[...]).astype(BF16)
        o_ref[...] = jnp.zeros_like(o_ref)

    xn = xn_ref[...]
    gate = _dot(xn, wg_ref[...])
    up = _dot(xn, wu_ref[...])
    act = (gate * jax.nn.sigmoid(gate) * up).astype(BF16)
    o_ref[...] += _dot(act, wd_ref[...])

    @pl.when(j == pl.num_programs(1) - 1)
    def _():
        h = x_ref[...] + 0.5 * o_ref[...]
        if final:
            h = _rms(h, fg_ref[...])
        o_ref[...] = h


def _ffn(x, norm_g, wg, wu, wd, final_g, *, final, tm=512, tf=512):
    L = x.shape[0]
    assert L % tm == 0 and D_FF % tf == 0
    grid = (L // tm, D_FF // tf)
    return pl.pallas_call(
        functools.partial(_ffn_kernel, final=final),
        grid=grid,
        in_specs=[
            pl.BlockSpec((tm, D_MODEL), lambda i, j: (i, 0)),
            pl.BlockSpec((1, D_MODEL), lambda i, j: (0, 0)),
            pl.BlockSpec((D_MODEL, tf), lambda i, j: (0, j)),
            pl.BlockSpec((D_MODEL, tf), lambda i, j: (0, j)),
            pl.BlockSpec((tf, D_MODEL), lambda i, j: (j, 0)),
            pl.BlockSpec((1, D_MODEL), lambda i, j: (0, 0)),
        ],
        out_specs=pl.BlockSpec((tm, D_MODEL), lambda i, j: (i, 0)),
        out_shape=jax.ShapeDtypeStruct((L, D_MODEL), F32),
        scratch_shapes=[pltpu.VMEM((tm, D_MODEL), BF16)],
        compiler_params=_cparams(("parallel", "arbitrary")),
        name="ffn",
    )(x, norm_g, wg, wu, wd, final_g)


def _rope(x, cos, sin_lo, sin_hi):
    return (x * cos + pltpu.roll(x, AXIS_DIM // 2, axis=1) * sin_hi
            + pltpu.roll(x, HEAD_DIM - AXIS_DIM // 2, axis=1) * sin_lo)


def _mix_proj_kernel(x_ref, xprev_ref, xnext_ref, g_ref, wg_ref, w_ref, wvt_ref, wab_ref, qn_ref, kn_ref, cos_ref,
                     slo_ref, shi_ref, alog_ref, dtb_ref, cw_ref, qg_ref, kg_ref, vg_ref, z_ref, q_ref, k_ref,
                     vt_ref, gcb_ref, gct_ref, ext_ref, *, tm):
    i = pl.program_id(0)
    pad = CONV_K // 2
    g = g_ref[...]
    xn = _rms(x_ref[...], g).astype(BF16)

    xn_ext = jnp.concatenate([_rms(xprev_ref[...], g).astype(BF16), xn, _rms(xnext_ref[...], g).astype(BF16)], axis=0)
    keep_prev = jnp.where(i == 0, 0.0, 1.0)
    keep_next = jnp.where(i == pl.num_programs(0) - 1, 0.0, 1.0)
    cos = cos_ref[...]
    slo = slo_ref[...]
    shi = shi_ref[...]

    def project_ext(c):
        cs = slice(c, c + PROJ_COL_TILE)
        e = _dot(xn_ext, wg_ref[:, cs])
        ext_ref[0:CONV_HALO, cs] = e[0:CONV_HALO] * keep_prev
        ext_ref[CONV_HALO:CONV_HALO + tm, cs] = e[CONV_HALO:CONV_HALO + tm]
        ext_ref[CONV_HALO + tm:, cs] = e[CONV_HALO + tm:] * keep_next

    def project_z(c):
        z_ref[:, c:c + PROJ_COL_TILE] = _dot(xn, wg_ref[:, 3 * GDN_W + c:3 * GDN_W + c + PROJ_COL_TILE])

    def project_attn_pair(h):
        pair = _dot(xn, w_ref[:, h * HEAD_DIM:(h + 2) * HEAD_DIM])
        for half in range(2):
            head = pair[:, half * HEAD_DIM:(half + 1) * HEAD_DIM]
            n = h + half
            if n < ATT_Q_HEADS:
                q_ref[:, n * HEAD_DIM:(n + 1) * HEAD_DIM] = (
                    _rope(_rms(head, qn_ref[...]), cos, slo, shi) * EXP2_SCALE).astype(BF16)
            else:
                n -= ATT_Q_HEADS
                k_ref[:, n * HEAD_DIM:(n + 1) * HEAD_DIM] = _rope(_rms(head, kn_ref[...]), cos, slo, shi).astype(BF16)

    def project_vt():
        vt_ref[...] = _dot_nt(wvt_ref[...], xn).astype(BF16)

    def conv_slab(s):
        cols = slice(s * HEAD_DIM, (s + 1) * HEAD_DIM)
        y = None
        for t in range(CONV_K):
            term = ext_ref[CONV_HALO - pad + t:CONV_HALO - pad + t + tm, cols] * cw_ref[t:t + 1, cols]
            y = term if y is None else y + term
        y = y * jax.nn.sigmoid(y)
        part, h = divmod(s, GDN_HEADS)
        hc = slice(h * HEAD_DIM, (h + 1) * HEAD_DIM)
        if part == 0:
            qg_ref[:, hc] = y * (lax.rsqrt(jnp.sum(y * y, axis=-1, keepdims=True) + EPS) * (HEAD_DIM ** -0.5))
        elif part == 1:
            kg_ref[:, hc] = y * lax.rsqrt(jnp.sum(y * y, axis=-1, keepdims=True) + EPS)
        else:
            vg_ref[:, hc] = y

    tiles = range(0, GDN_W, PROJ_COL_TILE)
    mxu_q = [functools.partial(project_ext, c) for c in tiles]
    mxu_k = [functools.partial(project_ext, GDN_W + c) for c in tiles]
    mxu_v = [functools.partial(project_ext, 2 * GDN_W + c) for c in tiles]
    mxu_z = [functools.partial(project_z, c) for c in tiles]
    mxu_att = [functools.partial(project_attn_pair, h) for h in range(0, ATT_Q_HEADS + ATT_KV_HEADS, 2)]
    mxu_att.append(project_vt)
    for piece in mxu_q:
        piece()
    n_att = len(mxu_att) // 2
    stages = [(mxu_k + mxu_att[:n_att], range(0, GDN_HEADS)),
              (mxu_v + mxu_att[n_att:] + mxu_z[:2], range(GDN_HEADS, 2 * GDN_HEADS)),
              (mxu_z[2:], range(2 * GDN_HEADS, 3 * GDN_HEADS))]
    for pieces, slabs in stages:
        for n in range(max(len(pieces), len(slabs))):
            if n < len(pieces):
                pieces[n]()
            if n < len(slabs):
                conv_slab(slabs[n])

    ab = _dot(xn, wab_ref[...])
    t = ab + dtb_ref[...]
    softplus = jnp.maximum(t, 0.0) + jnp.log1p(jnp.exp(-jnp.abs(t)))
    log_decay = -jnp.exp(alog_ref[...]) * softplus
    lane = lax.broadcasted_iota(jnp.int32, ab.shape, 1)
    gb = jnp.where(lane < 2 * GDN_HEADS, log_decay, jax.nn.sigmoid(ab))

    r = lax.broadcasted_iota(jnp.int32, (tm, tm), 0)
    c = lax.broadcasted_iota(jnp.int32, (tm, tm), 1)
    same = (r // SCAN_CHUNK) == (c // SCAN_CHUNK)
    m_lo = jnp.where(same & (c <= r), 1.0, 0.0).astype(BF16)
    m_up = jnp.where(same & (c >= r), 1.0, 0.0).astype(BF16)
    parts = _split3(gb)
    gc_f = _dot(m_lo, parts[0]) + _dot(m_lo, parts[1]) + _dot(m_lo, parts[2])
    gc_b = _dot(m_up, parts[0]) + _dot(m_up, parts[1]) + _dot(m_up, parts[2])
    gcb = jnp.where(lane < GDN_HEADS, gc_f, jnp.where(lane < 2 * GDN_HEADS, gc_b, gb))
    gcb_ref[...] = gcb
    for n in range(tm // SCAN_CHUNK):
        gct = gcb[n * SCAN_CHUNK:(n + 1) * SCAN_CHUNK, :].T
        gct_ref[n * 2 * GDN_HEADS:(n + 1) * 2 * GDN_HEADS, :] = gct[0:2 * GDN_HEADS, :]


def _mix_proj(x, norm_g, w_in, w_qk, w_vt, w_ab, q_norm, k_norm, cos, sin_lo, sin_hi, alog, dtb, conv_w8, *, tm=256):
    L = x.shape[0]
    assert L % tm == 0 and tm % SCAN_CHUNK == 0 and tm % CONV_HALO == 0
    n_halo = L // CONV_HALO
    per = tm // CONV_HALO
    row = lambda i: (i, 0)
    fixed = lambda i: (0, 0)
    resident = pl.Buffered(1)
    return pl.pallas_call(
        functools.partial(_mix_proj_kernel, tm=tm),
        grid=(L // tm,),
        in_specs=[
            pl.BlockSpec((tm, D_MODEL), row),
            pl.BlockSpec((CONV_HALO, D_MODEL), lambda i: (jnp.maximum(i * per - 1, 0), 0)),
            pl.BlockSpec((CONV_HALO, D_MODEL), lambda i: (jnp.minimum((i + 1) * per, n_halo - 1), 0)),
            pl.BlockSpec((1, D_MODEL), fixed),
            pl.BlockSpec((D_MODEL, GDN_PROJ_W), fixed, pipeline_mode=resident),
            pl.BlockSpec((D_MODEL, ATT_Q_W + ATT_KV_W), fixed, pipeline_mode=resident),
            pl.BlockSpec((ATT_KV_W, D_MODEL), fixed, pipeline_mode=resident),
            pl.BlockSpec((D_MODEL, LANES), fixed),
            pl.BlockSpec((1, HEAD_DIM), fixed),
            pl.BlockSpec((1, HEAD_DIM), fixed),
            pl.BlockSpec((tm, HEAD_DIM), row),
            pl.BlockSpec((tm, HEAD_DIM), row),
            pl.BlockSpec((tm, HEAD_DIM), row),
            pl.BlockSpec((1, LANES), fixed),
            pl.BlockSpec((1, LANES), fixed),
            pl.BlockSpec((SUBLANES, 3 * GDN_W), fixed),
        ],
        out_specs=[
            pl.BlockSpec((tm, GDN_W), row),
            pl.BlockSpec((tm, GDN_W), row),
            pl.BlockSpec((tm, GDN_W), row),
            pl.BlockSpec((tm, GDN_W), row),
            pl.BlockSpec((tm, ATT_Q_W), row),
            pl.BlockSpec((tm, ATT_KV_W), row),
            pl.BlockSpec((ATT_KV_W, tm), lambda i: (0, i)),
            pl.BlockSpec((tm, LANES), row),
            pl.BlockSpec((tm // SCAN_CHUNK * 2 * GDN_HEADS, SCAN_CHUNK), row),
        ],
        out_shape=[
            jax.ShapeDtypeStruct((L, GDN_W), F32),
            jax.ShapeDtypeStruct((L, GDN_W), F32),
            jax.ShapeDtypeStruct((L, GDN_W), F32),
            jax.ShapeDtypeStruct((L, GDN_W), F32),
            jax.ShapeDtypeStruct((L, ATT_Q_W), BF16),
            jax.ShapeDtypeStruct((L, ATT_KV_W), BF16),
            jax.ShapeDtypeStruct((ATT_KV_W, L), BF16),
            jax.ShapeDtypeStruct((L, LANES), F32),
            jax.ShapeDtypeStruct((L // SCAN_CHUNK * 2 * GDN_HEADS, SCAN_CHUNK), F32),
        ],
        scratch_shapes=[pltpu.VMEM((tm + 2 * CONV_HALO, 3 * GDN_W), F32)],
        compiler_params=_cparams(("parallel",)),
        name="mix_proj",
    )(x, x, x, norm_g, w_in, w_qk, w_vt, w_ab, q_norm, k_norm, cos, sin_lo, sin_hi, alog, dtb, conv_w8)


def _gdn_scan_kernel(qf_ref, kf_ref, vf_ref, gcbf_ref, gctf_ref, qb_ref, kb_ref, vb_ref, gcbb_ref, gctb_ref,
                     of_ref, ob_ref, s_ref):
    C = SCAN_CHUNK

    @pl.when(pl.program_id(0) == 0)
    def _():
        s_ref[...] = jnp.zeros_like(s_ref)

    r = lax.broadcasted_iota(jnp.int32, (C, C), 0)
    c = lax.broadcasted_iota(jnp.int32, (C, C), 1)
    eye = jnp.where(r == c, 1.0, 0.0)

    chains = []
    for fwd, (q_ref, k_ref, v_ref, gcb_ref, gct_ref, o_ref) in (
            (True, (qf_ref, kf_ref, vf_ref, gcbf_ref, gctf_ref, of_ref)),
            (False, (qb_ref, kb_ref, vb_ref, gcbb_ref, gctb_ref, ob_ref))):
        dif = r - c if fwd else c - r
        lane0 = 0 if fwd else GDN_HEADS
        gcb = gcb_ref[...]
        gc = gcb[:, lane0:lane0 + GDN_HEADS]
        beta = gcb[:, 2 * GDN_HEADS + lane0:3 * GDN_HEADS + lane0]
        g_end = gc[C - 1:C, :] if fwd else gc[0:1, :]
        shared = dict(q_ref=q_ref, k_ref=k_ref, v_ref=v_ref, o_ref=o_ref, m_incl=dif >= 0, m_strict=dif > 0,
                      gc=gc, beta=beta, e_gc=jnp.exp(gc), e_rest=jnp.exp(g_end - gc), e_end=jnp.exp(g_end),
                      gct=gct_ref[...])
        for h in range(GDN_HEADS):
            chains.append(dict(shared, cols=slice(h * HEAD_DIM, (h + 1) * HEAD_DIM), col1=slice(h, h + 1),
                               state=lane0 + h))
    heads = range(len(chains))
    cols = [ch["cols"] for ch in chains]
    col1 = [ch["col1"] for ch in chains]
    beta = [ch["beta"] for ch in chains]
    e_gc = [ch["e_gc"] for ch in chains]

    k = [chains[h]["k_ref"][:, cols[h]] for h in heads]
    k16 = [k[h].astype(BF16) for h in heads]
    kb = [k[h] * beta[h][:, col1[h]] for h in heads]
    q16 = [chains[h]["q_ref"][:, cols[h]].astype(BF16) for h in heads]
    kq = [_dot_nt(jnp.concatenate([kb[h].astype(BF16), q16[h]], axis=0), k16[h]) for h in heads]
    decay = []
    for ch in chains:
        diff = ch["gc"][:, ch["col1"]] - ch["gct"][ch["col1"], :]
        decay.append(jnp.where(ch["m_incl"], jnp.exp(jnp.where(ch["m_incl"], diff, 0.0)), 0.0))
    a = [jnp.where(chains[h]["m_strict"], kq[h][0:C] * decay[h], 0.0) for h in heads]
    attn_qk16 = [(kq[h][C:2 * C] * decay[h]).astype(BF16) for h in heads]

    rb = r >> INV_BASE_LEVELS
    cb = c >> INV_BASE_LEVELS
    a_d = [jnp.where(rb == cb, a[h], 0.0) for h in heads]
    inv = [eye - a_d[h] for h in heads]
    a_d16 = [a_d[h].astype(BF16) for h in heads]
    a_pow16 = [_dot(a_d16[h], a_d16[h]).astype(BF16) for h in heads]
    for level in range(INV_BASE_LEVELS - 1):
        if level < INV_BASE_LEVELS - 2:
            both = [_dot(jnp.concatenate([inv[h].astype(BF16), a_pow16[h]], axis=0), a_pow16[h]) for h in heads]
            inv = [inv[h] + both[h][0:C] for h in heads]
            a_pow16 = [both[h][C:2 * C].astype(BF16) for h in heads]
        else:
            inv = [inv[h] + _dot(inv[h].astype(BF16), a_pow16[h]) for h in heads]
    b = INV_BASE
    while b < C:
        off = ((rb >> 1) == (cb >> 1)) & (rb != cb)
        a_off16 = [jnp.where(off, a[h], 0.0).astype(BF16) for h in heads]
        inv16 = [inv[h].astype(BF16) for h in heads]
        left = [_dot(inv16[h], a_off16[h]).astype(BF16) for h in heads]
        inv = [inv[h] - _dot(left[h], inv16[h]) for h in heads]
        rb = rb >> 1
        cb = cb >> 1
        b *= 2

    rhs16 = [jnp.concatenate([(chains[h]["v_ref"][:, cols[h]] * beta[h][:, col1[h]]).astype(BF16),
                              (kb[h] * e_gc[h][:, col1[h]]).astype(BF16)], axis=1) for h in heads]
    uw = [_dot(inv[h].astype(BF16), rhs16[h]) for h in heads]
    q_dec16 = [(chains[h]["q_ref"][:, cols[h]] * e_gc[h][:, col1[h]]).astype(BF16) for h in heads]
    k_dec_t16 = [(k[h] * chains[h]["e_rest"][:, col1[h]]).T.astype(BF16) for h in heads]

    s = [s_ref[ch["state"]] for ch in chains]
    s16 = [s[h].astype(BF16) for h in heads]
    ws = [_dot(jnp.concatenate([uw[h][:, HEAD_DIM:].astype(BF16), q_dec16[h]], axis=0), s16[h]) for h in heads]
    v_new16 = [(uw[h][:, 0:HEAD_DIM] - ws[h][0:C]).astype(BF16) for h in heads]
    for h, ch in enumerate(chains):
        ch["o_ref"][:, cols[h]] = ws[h][C:2 * C] + _dot(attn_qk16[h], v_new16[h])
    for h, ch in enumerate(chains):
        s_ref[ch["state"]] = s[h] * ch["e_end"][:, col1[h]] + _dot(k_dec_t16[h], v_new16[h])


def _gdn_scan(q, k, v, gcb, gct):
    L = q.shape[0]
    assert L % SCAN_CHUNK == 0
    n = L // SCAN_CHUNK
    fwd = lambda i: (i, 0)
    bwd = lambda i: (n - 1 - i, 0)
    tile_f = pl.BlockSpec((SCAN_CHUNK, GDN_W), fwd)
    tile_b = pl.BlockSpec((SCAN_CHUNK, GDN_W), bwd)
    return pl.pallas_call(
        _gdn_scan_kernel,
        grid=(n,),
        in_specs=[
            tile_f, tile_f, tile_f,
            pl.BlockSpec((SCAN_CHUNK, LANES), fwd),
            pl.BlockSpec((GDN_HEADS, SCAN_CHUNK), lambda i: (2 * i, 0)),
            tile_b, tile_b, tile_b,
            pl.BlockSpec((SCAN_CHUNK, LANES), bwd),
            pl.BlockSpec((GDN_HEADS, SCAN_CHUNK), lambda i: (2 * (n - 1 - i) + 1, 0)),
        ],
        out_specs=[tile_f, tile_b],
        out_shape=[jax.ShapeDtypeStruct((L, GDN_W), F32), jax.ShapeDtypeStruct((L, GDN_W), F32)],
        scratch_shapes=[pltpu.VMEM((2 * GDN_HEADS, HEAD_DIM, HEAD_DIM), F32)],
        compiler_params=_cparams(("arbitrary",)),
        name="gdn_scan",
    )(q, k, v, gcb, gct, q, k, v, gcb, gct)


def _flash_kernel(q_ref, k_ref, vt_ref, on_ref, o_ref, m_ref, acc_ref, st_ref, p_ref, *, tk):
    ki = pl.program_id(1)

    @pl.when(ki == 0)
    def _():
        m_ref[...] = jnp.full_like(m_ref, -jnp.inf)
        acc_ref[...] = jnp.zeros_like(acc_ref)

    ones = jnp.ones((FLASH_SUM_ROWS, tk), BF16)
    k = [k_ref[:, g * HEAD_DIM:(g + 1) * HEAD_DIM] for g in range(ATT_KV_HEADS)]
    vt1 = [jnp.concatenate([vt_ref[g * HEAD_DIM:(g + 1) * HEAD_DIM, :], ones], axis=0) for g in range(ATT_KV_HEADS)]
    blocks = [slice(r, r + FLASH_ROW_BLOCK) for r in range(0, tk, FLASH_ROW_BLOCK)]

    tq = q_ref.shape[0]
    units = [(h, slice(c, c + FLASH_Q_BLOCK)) for h in range(ATT_Q_HEADS) for c in range(0, tq, FLASH_Q_BLOCK)]

    def scores(h, qs):
        st_ref[h, :, qs] = _dot_nt(k[h // ATT_GROUP], q_ref[qs, h * HEAD_DIM:(h + 1) * HEAD_DIM])

    def softmax(h, qs):
        part = None
        for rows in blocks:
            x = st_ref[h, rows, qs]
            while x.shape[0] > SUBLANES:
                half = x.shape[0] // 2
                x = jnp.maximum(x[:half], x[half:])
            part = x if part is None else jnp.maximum(part, x)
        m_prev = m_ref[h, :, qs]
        m_new = jnp.maximum(m_prev, jnp.max(part, axis=0, keepdims=True))
        m_ref[h, :, qs] = m_new
        for rows in blocks:
            p_ref[h, rows, qs] = jnp.exp2(st_ref[h, rows, qs] - m_new).astype(BF16)
        return jnp.exp2(m_prev - m_new)

    def accumulate(h, qs, alpha):
        acc_ref[h, :, qs] = alpha * acc_ref[h, :, qs] + _dot(vt1[h // ATT_GROUP], p_ref[h, :, qs])

    scores(*units[0])
    pending = None
    for u, unit in enumerate(units):
        if u + 1 < len(units):
            scores(*units[u + 1])
        alpha = softmax(*unit)
        if pending is not None:
            accumulate(*pending)
        pending = (*unit, alpha)
    accumulate(*pending)

    @pl.when(ki == pl.num_programs(1) - 1)
    def _():
        for h in range(ATT_Q_HEADS):
            cols = slice(h * HEAD_DIM, (h + 1) * HEAD_DIM)
            o = (acc_ref[h, 0:HEAD_DIM, :] / acc_ref[h, HEAD_DIM:HEAD_DIM + 1, :]).T
            o_ref[:, cols] = _rms(o, on_ref[...]).astype(BF16)


def _flash_attn(q, k, vt, out_norm, *, tq=512, tk=1024):
    L = q.shape[0]
    assert L % tq == 0 and L % tk == 0 and tq % FLASH_Q_BLOCK == 0 and tk % FLASH_ROW_BLOCK == 0
    return pl.pallas_call(
        functools.partial(_flash_kernel, tk=tk),
        grid=(L // tq, L // tk),
        in_specs=[
            pl.BlockSpec((tq, ATT_Q_W), lambda i, j: (i, 0)),
            pl.BlockSpec((tk, ATT_KV_W), lambda i, j: (j, 0)),
            pl.BlockSpec((ATT_KV_W, tk), lambda i, j: (0, j)),
            pl.BlockSpec((1, HEAD_DIM), lambda i, j: (0, 0)),
        ],
        out_specs=pl.BlockSpec((tq, ATT_Q_W), lambda i, j: (i, 0)),
        out_shape=jax.ShapeDtypeStruct((L, ATT_Q_W), BF16),
        scratch_shapes=[
            pltpu.VMEM((ATT_Q_HEADS, 1, tq), F32),
            pltpu.VMEM((ATT_Q_HEADS, HEAD_DIM + FLASH_SUM_ROWS, tq), F32),
            pltpu.VMEM((ATT_Q_HEADS, tk, tq), F32),
            pltpu.VMEM((ATT_Q_HEADS, tk, tq), BF16),
        ],
        compiler_params=_cparams(("parallel", "arbitrary")),
        name="flash_attn",
    )(q, k, vt, out_norm)


def _out_proj_kernel(of_ref, ob_ref, z_ref, oa_ref, h_ref, gn_ref, w_ref, o_ref, mix_ref):
    att = _dot(oa_ref[...], w_ref[GDN_W:, :])
    for h in range(GDN_HEADS):
        cols = slice(h * HEAD_DIM, (h + 1) * HEAD_DIM)
        o = _rms(of_ref[:, cols] + ob_ref[:, cols], gn_ref[...])
        z = z_ref[:, cols]
        mix_ref[:, cols] = (o * (z * jax.nn.sigmoid(z))).astype(BF16)
    o_ref[...] = h_ref[...] + (att + _dot(mix_ref[...], w_ref[0:GDN_W, :]))


def _out_proj(o_fwd, o_bwd, z, oa, h1, gdn_norm, w_out, *, tm=512):
    L = h1.shape[0]
    assert L % tm == 0
    row = lambda i: (i, 0)
    return pl.pallas_call(
        _out_proj_kernel,
        grid=(L // tm,),
        in_specs=[
            pl.BlockSpec((tm, GDN_W), row),
            pl.BlockSpec((tm, GDN_W), row),
            pl.BlockSpec((tm, GDN_W), row),
            pl.BlockSpec((tm, ATT_Q_W), row),
            pl.BlockSpec((tm, D_MODEL), row),
            pl.BlockSpec((1, HEAD_DIM), lambda i: (0, 0)),
            pl.BlockSpec((GDN_W + ATT_Q_W, D_MODEL), lambda i: (0, 0)),
        ],
        out_specs=pl.BlockSpec((tm, D_MODEL), row),
        out_shape=jax.ShapeDtypeStruct((L, D_MODEL), F32),
        scratch_shapes=[pltpu.VMEM((tm, GDN_W), BF16)],
        compiler_params=_cparams(("parallel",)),
        name="out_proj",
    )(o_fwd, o_bwd, z, oa, h1, gdn_norm, w_out)


def _rope_tables(L):
    assert L % GRID_W == 0
    rows = L // GRID_W
    freqs = ROPE_THETA ** (-jnp.arange(0, AXIS_DIM, 2, dtype=F32) / AXIS_DIM)
    ang_r = jnp.arange(rows, dtype=F32)[:, None] * freqs[None, :]
    ang_c = jnp.arange(GRID_W, dtype=F32)[:, None] * freqs[None, :]
    per_row = lambda tab: jnp.repeat(tab, GRID_W, axis=0)
    per_col = lambda tab: jnp.tile(tab, (rows, 1))
    cos_r, sin_r = per_row(jnp.cos(ang_r)), per_row(jnp.sin(ang_r))
    cos_c, sin_c = per_col(jnp.cos(ang_c)), per_col(jnp.sin(ang_c))
    zero = jnp.zeros_like(cos_r)
    cos = jnp.concatenate([cos_r, cos_r, cos_c, cos_c], axis=-1)
    sin_lo = jnp.concatenate([-sin_r, zero, -sin_c, zero], axis=-1)
    sin_hi = jnp.concatenate([zero, sin_r, zero, sin_c], axis=-1)
    return cos, sin_lo, sin_hi


def _pad_lanes(x):
    x = x.reshape(1, -1)
    return jnp.pad(x, ((0, 0), (0, LANES - x.shape[1])))


def _encode(x, p, rope):
    h1 = _ffn(x, p["ffn1_norm"], p["ffn1_wg"], p["ffn1_wu"], p["ffn1_wd"], p["final_norm"], final=False)
    cos, sin_lo, sin_hi = rope
    qg, kg, vg, z, qa, ka, va, gcb, gct = _mix_proj(h1, p["mix_norm"], p["w_in"], p["w_qk"], p["w_vt"], p["w_ab"],
                                                    p["q_norm"], p["k_norm"], cos, sin_lo, sin_hi, p["alog"],
                                                    p["dtb"], p["conv_w"])
    o_fwd, o_bwd = _gdn_scan(qg, kg, vg, gcb, gct)
    oa = _flash_attn(qa, ka, va, p["attn_out_norm"])
    h2 = _out_proj(o_fwd, o_bwd, z, oa, h1, p["gdn_out_norm"], p["w_out"])
    return _ffn(h2, p["ffn2_norm"], p["ffn2_wg"], p["ffn2_wu"], p["ffn2_wd"], p["final_norm"], final=True)


def _prepare_params(ffn1_norm, ffn1_w_gate, ffn1_w_up, ffn1_w_down, mix_norm, w_in, conv_w, a_log_fwd, a_log_bwd,
                    dt_bias_fwd, dt_bias_bwd, gdn_out_norm, q_norm, k_norm, attn_out_norm, w_out, ffn2_norm,
                    ffn2_w_gate, ffn2_w_up, ffn2_w_down, final_norm):
    w_in0 = w_in[0]
    att0 = GDN_PROJ_W + GATE_COLS
    w_ab = jnp.pad(w_in0[:, GDN_PROJ_W:att0], ((0, 0), (0, LANES - GATE_COLS)))
    return dict(
        ffn1_norm=ffn1_norm[0].reshape(1, -1),
        ffn1_wg=ffn1_w_gate[0].astype(BF16), ffn1_wu=ffn1_w_up[0].astype(BF16), ffn1_wd=ffn1_w_down[0].astype(BF16),
        mix_norm=mix_norm[0].reshape(1, -1),
        w_in=w_in0[:, :GDN_PROJ_W].astype(BF16),
        w_qk=w_in0[:, att0:att0 + ATT_Q_W + ATT_KV_W].astype(BF16),
        w_vt=w_in0[:, att0 + ATT_Q_W + ATT_KV_W:].T.astype(BF16),
        w_ab=w_ab.astype(BF16),
        conv_w=jnp.pad(conv_w[0], ((0, SUBLANES - CONV_K), (0, 0))),
        alog=_pad_lanes(jnp.concatenate([a_log_fwd[0], a_log_bwd[0]])),
        dtb=_pad_lanes(jnp.concatenate([dt_bias_fwd[0], dt_bias_bwd[0]])),
        gdn_out_norm=gdn_out_norm[0].reshape(1, -1),
        q_norm=q_norm[0].reshape(1, -1), k_norm=k_norm[0].reshape(1, -1),
        attn_out_norm=attn_out_norm[0].reshape(1, -1),
        w_out=w_out[0].astype(BF16),
        ffn2_norm=ffn2_norm[0].reshape(1, -1),
        ffn2_wg=ffn2_w_gate[0].astype(BF16), ffn2_wu=ffn2_w_up[0].astype(BF16), ffn2_wd=ffn2_w_down[0].astype(BF16),
        final_norm=final_norm.reshape(1, -1),
    )


def kernel(x_prompt, x_sample, ffn1_norm, ffn1_w_gate, ffn1_w_up, ffn1_w_down, mix_norm, w_in, conv_w, a_log_fwd,
           a_log_bwd, dt_bias_fwd, dt_bias_bwd, gdn_out_norm, q_norm, k_norm, attn_out_norm, w_out, ffn2_norm,
           ffn2_w_gate, ffn2_w_up, ffn2_w_down, final_norm):
    assert x_prompt.shape[0] == 1 and x_sample.shape[0] == 1
    p = _prepare_params(ffn1_norm, ffn1_w_gate, ffn1_w_up, ffn1_w_down, mix_norm, w_in, conv_w, a_log_fwd,
                        a_log_bwd, dt_bias_fwd, dt_bias_bwd, gdn_out_norm, q_norm, k_norm, attn_out_norm, w_out,
                        ffn2_norm, ffn2_w_gate, ffn2_w_up, ffn2_w_down, final_norm)
    rope = _rope_tables(max(x_prompt.shape[1], x_sample.shape[1]))
    y_prompt = _encode(x_prompt[0], p, rope)
    y_sample = _encode(x_sample[0], p, rope)
    return (y_prompt[None], y_sample[None])
```

```python
import functools
import math

import jax
import jax.numpy as jnp
from jax import lax
from jax.experimental import pallas as pl
from jax.experimental.pallas import tpu as pltpu

D_MODEL = 2048
HEAD_DIM = 128
GDN_HEADS = 8
GDN_W = GDN_HEADS * HEAD_DIM
ATT_Q_HEADS = 8
ATT_KV_HEADS = 2
ATT_GROUP = ATT_Q_HEADS // ATT_KV_HEADS
ATT_Q_W = ATT_Q_HEADS * HEAD_DIM
ATT_KV_W = ATT_KV_HEADS * HEAD_DIM
GDN_PROJ_W = 4 * GDN_W
GATE_COLS = 4 * GDN_HEADS
D_FF = 5632
CONV_K = 5
GRID_W = 64
AXIS_DIM = HEAD_DIM // 2
ROPE_THETA = 10000.0
EPS = 1e-6

LANES = 128
SUBLANES = 8
SCAN_CHUNK = 128
PROJ_COL_TILE = 256
CONV_HALO = 16
INV_BASE_LEVELS = 2
INV_BASE = 2 ** INV_BASE_LEVELS
EXP2_SCALE = HEAD_DIM ** -0.5 * math.log2(math.e)
FLASH_ROW_BLOCK = 64
FLASH_SUM_ROWS = 16
FLASH_Q_BLOCK = 512

F32 = jnp.float32
BF16 = jnp.bfloat16

V7X_VMEM_BYTES = 64 * 1024 * 1024
_VMEM_LIMIT = V7X_VMEM_BYTES * 7 // 8


def _cparams(semantics):
    return pltpu.CompilerParams(dimension_semantics=semantics, vmem_limit_bytes=_VMEM_LIMIT)


def _rms(x, g):
    return x * lax.rsqrt(jnp.mean(x * x, axis=-1, keepdims=True) + EPS) * g


def _dot(a, b):
    return jnp.dot(a, b, preferred_element_type=F32)


def _dot_nt(a, b):
    return lax.dot_general(a, b, (((1,), (1,)), ((), ())), preferred_element_type=F32)


def _split3(x):
    hi = x.astype(BF16)
    r = x - hi.astype(F32)
    mid = r.astype(BF16)
    lo = (r - mid.astype(F32)).astype(BF16)
    return hi, mid, lo


def _ffn_kernel(x_ref, g_ref, wg_ref, wu_ref, wd_ref, fg_ref, o_ref, xn_ref, act_ref, *, final):
    j = pl.program_id(1)
    n_tiles = pl.num_programs(1) - 1

    def activations():
        xn = xn_ref[...]
        gate = _dot(xn, wg_ref[...])
        up = _dot(xn, wu_ref[...])
        return (gate * jax.nn.sigmoid(gate) * up).astype(BF16)

    @pl.when(j == 0)
    def _():
        xn_ref[...] = _rms(x_ref[...], g_ref[...]).astype(BF16)
        act_ref[...] = activations()

    @pl.when(j == 1)
    def _():
        o_ref[...] = _dot(act_ref[...], wd_ref[...])
        act_ref[...] = activations()

    @pl.when(jnp.logical_and(j > 1, j < n_tiles))
    def _():
        o_ref[...] += _dot(act_ref[...], wd_ref[...])
        act_ref[...] = activations()

    @pl.when(j == n_tiles)
    def _():
        h = x_ref[...] + 0.5 * (o_ref[...] + _dot(act_ref[...], wd_ref[...]))
        if final:
            h = _rms(h, fg_ref[...])
        o_ref[...] = h


def _ffn(x, norm_g, wg, wu, wd, final_g, *, final, tm=512, tf=512):
    L = x.shape[0]
    assert L % tm == 0 and D_FF % tf == 0
    n_tiles = D_FF // tf
    assert n_tiles >= 2
    up_tile = lambda i, j: (0, jnp.minimum(j, n_tiles - 1))
    return pl.pallas_call(
        functools.partial(_ffn_kernel, final=final),
        grid=(L // tm, n_tiles + 1),
        in_specs=[
            pl.BlockSpec((tm, D_MODEL), lambda i, j: (i, 0)),
            pl.BlockSpec((1, D_MODEL), lambda i, j: (0, 0)),
            pl.BlockSpec((D_MODEL, tf), up_tile),
            pl.BlockSpec((D_MODEL, tf), up_tile),
            pl.BlockSpec((tf, D_MODEL), lambda i, j: (jnp.maximum(j - 1, 0), 0)),
            pl.BlockSpec((1, D_MODEL), lambda i, j: (0, 0)),
        ],
        out_specs=pl.BlockSpec((tm, D_MODEL), lambda i, j: (i, 0)),
        out_shape=jax.ShapeDtypeStruct((L, D_MODEL), F32),
        scratch_shapes=[pltpu.VMEM((tm, D_MODEL), BF16), pltpu.VMEM((tm, tf), BF16)],
        compiler_params=_cparams(("parallel", "arbitrary")),
        name="ffn",
    )(x, norm_g, wg, wu, wd, final_g)


def _rope(x, cos, sin_lo, sin_hi):
    return (x * cos + pltpu.roll(x, AXIS_DIM // 2, axis=1) * sin_hi
            + pltpu.roll(x, HEAD_DIM - AXIS_DIM // 2, axis=1) * sin_lo)


def _mix_proj_kernel(x_ref, xprev_ref, xnext_ref, g_ref, wg_ref, w_ref, wvt_ref, wab_ref, qn_ref, kn_ref, cos_ref,
                     slo_ref, shi_ref, alog_ref, dtb_ref, cw_ref, qg_ref, kg_ref, vg_ref, z_ref, q_ref, k_ref,
                     vt_ref, gcb_ref, gct_ref, ext_ref, *, tm):
    i = pl.program_id(0)
    pad = CONV_K // 2
    g = g_ref[...]
    xn = _rms(x_ref[...], g).astype(BF16)

    xn_ext = jnp.concatenate([_rms(xprev_ref[...], g).astype(BF16), xn, _rms(xnext_ref[...], g).astype(BF16)], axis=0)
    keep_prev = jnp.where(i == 0, 0.0, 1.0)
    keep_next = jnp.where(i == pl.num_programs(0) - 1, 0.0, 1.0)
    cos = cos_ref[...]
    slo = slo_ref[...]
    shi = shi_ref[...]

    def project_ext(c):
        cs = slice(c, c + PROJ_COL_TILE)
        e = _dot(xn_ext, wg_ref[:, cs])
        ext_ref[0:CONV_HALO, cs] = e[0:CONV_HALO] * keep_prev
        ext_ref[CONV_HALO:CONV_HALO + tm, cs] = e[CONV_HALO:CONV_HALO + tm]
        ext_ref[CONV_HALO + tm:, cs] = e[CONV_HALO + tm:] * keep_next

    def project_z(c):
        z_ref[:, c:c + PROJ_COL_TILE] = _dot(xn, wg_ref[:, 3 * GDN_W + c:3 * GDN_W + c + PROJ_COL_TILE])

    def project_attn_pair(h):
        pair = _dot(xn, w_ref[:, h * HEAD_DIM:(h + 2) * HEAD_DIM])
        for half in range(2):
            head = pair[:, half * HEAD_DIM:(half + 1) * HEAD_DIM]
            n = h + half
            if n < ATT_Q_HEADS:
                q_ref[:, n * HEAD_DIM:(n + 1) * HEAD_DIM] = (
                    _rope(_rms(head, qn_ref[...]), cos, slo, shi) * EXP2_SCALE).astype(BF16)
            else:
                n -= ATT_Q_HEADS
                k_ref[:, n * HEAD_DIM:(n + 1) * HEAD_DIM] = _rope(_rms(head, kn_ref[...]), cos, slo, shi).astype(BF16)

    def project_vt():
        vt_ref[...] = _dot_nt(wvt_ref[...], xn).astype(BF16)

    def conv_slab(s):
        cols = slice(s * HEAD_DIM, (s + 1) * HEAD_DIM)
        y = None
        for t in range(CONV_K):
            term = ext_ref[CONV_HALO - pad + t:CONV_HALO - pad + t + tm, cols] * cw_ref[t:t + 1, cols]
            y = term if y is None else y + term
        y = y * jax.nn.sigmoid(y)
        part, h = divmod(s, GDN_HEADS)
        hc = slice(h * HEAD_DIM, (h + 1) * HEAD_DIM)
        if part == 0:
            qg_ref[:, hc] = y * (lax.rsqrt(jnp.sum(y * y, axis=-1, keepdims=True) + EPS) * (HEAD_DIM ** -0.5))
        elif part == 1:
            kg_ref[:, hc] = y * lax.rsqrt(jnp.sum(y * y, axis=-1, keepdims=True) + EPS)
        else:
            vg_ref[:, hc] = y

    tiles = range(0, GDN_W, PROJ_COL_TILE)
    mxu_q = [functools.partial(project_ext, c) for c in tiles]
    mxu_k = [functools.partial(project_ext, GDN_W + c) for c in tiles]
    mxu_v = [functools.partial(project_ext, 2 * GDN_W + c) for c in tiles]
    mxu_z = [functools.partial(project_z, c) for c in tiles]
    mxu_att = [functools.partial(project_attn_pair, h) for h in range(0, ATT_Q_HEADS + ATT_KV_HEADS, 2)]
    mxu_att.append(project_vt)
    for piece in mxu_q:
        piece()
    n_att = len(mxu_att) // 2
    stages = [(mxu_k + mxu_att[:n_att], range(0, GDN_HEADS)),
              (mxu_v + mxu_att[n_att:] + mxu_z[:2], range(GDN_HEADS, 2 * GDN_HEADS)),
              (mxu_z[2:], range(2 * GDN_HEADS, 3 * GDN_HEADS))]
    for pieces, slabs in stages:
        for n in range(max(len(pieces), len(slabs))):
            if n < len(pieces):
                pieces[n]()
            if n < len(slabs):
                conv_slab(slabs[n])

    ab = _dot(xn, wab_ref[...])
    t = ab + dtb_ref[...]
    softplus = jnp.maximum(t, 0.0) + jnp.log1p(jnp.exp(-jnp.abs(t)))
    log_decay = -jnp.exp(alog_ref[...]) * softplus
    lane = lax.broadcasted_iota(jnp.int32, ab.shape, 1)
    gb = jnp.where(lane < 2 * GDN_HEADS, log_decay, jax.nn.sigmoid(ab))

    r = lax.broadcasted_iota(jnp.int32, (tm, tm), 0)
    c = lax.broadcasted_iota(jnp.int32, (tm, tm), 1)
    same = (r // SCAN_CHUNK) == (c // SCAN_CHUNK)
    m_lo = jnp.where(same & (c <= r), 1.0, 0.0).astype(BF16)
    m_up = jnp.where(same & (c >= r), 1.0, 0.0).astype(BF16)
    parts = _split3(gb)
    gc_f = _dot(m_lo, parts[0]) + _dot(m_lo, parts[1]) + _dot(m_lo, parts[2])
    gc_b = _dot(m_up, parts[0]) + _dot(m_up, parts[1]) + _dot(m_up, parts[2])
    gcb = jnp.where(lane < GDN_HEADS, gc_f, jnp.where(lane < 2 * GDN_HEADS, gc_b, gb))
    gcb_ref[...] = gcb
    for n in range(tm // SCAN_CHUNK):
        gct = gcb[n * SCAN_CHUNK:(n + 1) * SCAN_CHUNK, :].T
        gct_ref[n * 2 * GDN_HEADS:(n + 1) * 2 * GDN_HEADS, :] = gct[0:2 * GDN_HEADS, :]


def _mix_proj(x, norm_g, w_in, w_qk, w_vt, w_ab, q_norm, k_norm, cos, sin_lo, sin_hi, alog, dtb, conv_w8, *, tm=256):
    L = x.shape[0]
    assert L % tm == 0 and tm % SCAN_CHUNK == 0 and tm % CONV_HALO == 0
    n_halo = L // CONV_HALO
    per = tm // CONV_HALO
    row = lambda i: (i, 0)
    fixed = lambda i: (0, 0)
    resident = pl.Buffered(1)
    return pl.pallas_call(
        functools.partial(_mix_proj_kernel, tm=tm),
        grid=(L // tm,),
        in_specs=[
            pl.BlockSpec((tm, D_MODEL), row),
            pl.BlockSpec((CONV_HALO, D_MODEL), lambda i: (jnp.maximum(i * per - 1, 0), 0)),
            pl.BlockSpec((CONV_HALO, D_MODEL), lambda i: (jnp.minimum((i + 1) * per, n_halo - 1), 0)),
            pl.BlockSpec((1, D_MODEL), fixed),
            pl.BlockSpec((D_MODEL, GDN_PROJ_W), fixed, pipeline_mode=resident),
            pl.BlockSpec((D_MODEL, ATT_Q_W + ATT_KV_W), fixed, pipeline_mode=resident),
            pl.BlockSpec((ATT_KV_W, D_MODEL), fixed, pipeline_mode=resident),
            pl.BlockSpec((D_MODEL, LANES), fixed),
            pl.BlockSpec((1, HEAD_DIM), fixed),
            pl.BlockSpec((1, HEAD_DIM), fixed),
            pl.BlockSpec((tm, HEAD_DIM), row),
            pl.BlockSpec((tm, HEAD_DIM), row),
            pl.BlockSpec((tm, HEAD_DIM), row),
            pl.BlockSpec((1, LANES), fixed),
            pl.BlockSpec((1, LANES), fixed),
            pl.BlockSpec((SUBLANES, 3 * GDN_W), fixed),
        ],
        out_specs=[
            pl.BlockSpec((tm, GDN_W), row),
            pl.BlockSpec((tm, GDN_W), row),
            pl.BlockSpec((tm, GDN_W), row),
            pl.BlockSpec((tm, GDN_W), row),
            pl.BlockSpec((tm, ATT_Q_W), row),
            pl.BlockSpec((tm, ATT_KV_W), row),
            pl.BlockSpec((ATT_KV_W, tm), lambda i: (0, i)),
            pl.BlockSpec((tm, LANES), row),
            pl.BlockSpec((tm // SCAN_CHUNK * 2 * GDN_HEADS, SCAN_CHUNK), row),
        ],
        out_shape=[
            jax.ShapeDtypeStruct((L, GDN_W), F32),
            jax.ShapeDtypeStruct((L, GDN_W), F32),
            jax.ShapeDtypeStruct((L, GDN_W), F32),
            jax.ShapeDtypeStruct((L, GDN_W), F32),
            jax.ShapeDtypeStruct((L, ATT_Q_W), BF16),
            jax.ShapeDtypeStruct((L, ATT_KV_W), BF16),
            jax.ShapeDtypeStruct((ATT_KV_W, L), BF16),
            jax.ShapeDtypeStruct((L, LANES), F32),
            jax.ShapeDtypeStruct((L // SCAN_CHUNK * 2 * GDN_HEADS, SCAN_CHUNK), F32),
        ],
        scratch_shapes=[pltpu.VMEM((tm + 2 * CONV_HALO, 3 * GDN_W), F32)],
        compiler_params=_cparams(("parallel",)),
        name="mix_proj",
    )(x, x, x, norm_g, w_in, w_qk, w_vt, w_ab, q_norm, k_norm, cos, sin_lo, sin_hi, alog, dtb, conv_w8)


def _gdn_scan_kernel(qf_ref, kf_ref, vf_ref, gcbf_ref, gctf_ref, qb_ref, kb_ref, vb_ref, gcbb_ref, gctb_ref,
                     of_ref, ob_ref, s_ref):
    C = SCAN_CHUNK

    @pl.when(pl.program_id(0) == 0)
    def _():
        s_ref[...] = jnp.zeros_like(s_ref)

    r = lax.broadcasted_iota(jnp.int32, (C, C), 0)
    c = lax.broadcasted_iota(jnp.int32, (C, C), 1)
    eye = jnp.where(r == c, 1.0, 0.0)

    chains = []
    for fwd, (q_ref, k_ref, v_ref, gcb_ref, gct_ref, o_ref) in (
            (True, (qf_ref, kf_ref, vf_ref, gcbf_ref, gctf_ref, of_ref)),
            (False, (qb_ref, kb_ref, vb_ref, gcbb_ref, gctb_ref, ob_ref))):
        dif = r - c if fwd else c - r
        lane0 = 0 if fwd else GDN_HEADS
        gcb = gcb_ref[...]
        gc = gcb[:, lane0:lane0 + GDN_HEADS]
        beta = gcb[:, 2 * GDN_HEADS + lane0:3 * GDN_HEADS + lane0]
        g_end = gc[C - 1:C, :] if fwd else gc[0:1, :]
        shared = dict(q_ref=q_ref, k_ref=k_ref, v_ref=v_ref, o_ref=o_ref, m_incl=dif >= 0, m_strict=dif > 0,
                      gc=gc, beta=beta, e_gc=jnp.exp(gc), e_rest=jnp.exp(g_end - gc), e_end=jnp.exp(g_end),
                      gct=gct_ref[...])
        for h in range(GDN_HEADS):
            chains.append(dict(shared, cols=slice(h * HEAD_DIM, (h + 1) * HEAD_DIM), col1=slice(h, h + 1),
                               state=lane0 + h))
    heads = range(len(chains))
    cols = [ch["cols"] for ch in chains]
    col1 = [ch["col1"] for ch in chains]
    beta = [ch["beta"] for ch in chains]
    e_gc = [ch["e_gc"] for ch in chains]

    k = [chains[h]["k_ref"][:, cols[h]] for h in heads]
    k16 = [k[h].astype(BF16) for h in heads]
    kb = [k[h] * beta[h][:, col1[h]] for h in heads]
    q16 = [chains[h]["q_ref"][:, cols[h]].astype(BF16) for h in heads]
    kq = [_dot_nt(jnp.concatenate([kb[h].astype(BF16), q16[h]], axis=0), k16[h]) for h in heads]
    decay = []
    for ch in chains:
        diff = ch["gc"][:, ch["col1"]] - ch["gct"][ch["col1"], :]
        decay.append(jnp.where(ch["m_incl"], jnp.exp(jnp.where(ch["m_incl"], diff, 0.0)), 0.0))
    a = [jnp.where(chains[h]["m_strict"], kq[h][0:C] * decay[h], 0.0) for h in heads]
    attn_qk16 = [(kq[h][C:2 * C] * decay[h]).astype(BF16) for h in heads]

    rb = r >> INV_BASE_LEVELS
    cb = c >> INV_BASE_LEVELS
    a_d = [jnp.where(rb == cb, a[h], 0.0) for h in heads]
    inv = [eye - a_d[h] for h in heads]
    a_d16 = [a_d[h].astype(BF16) for h in heads]
    a_pow16 = [_dot(a_d16[h], a_d16[h]).astype(BF16) for h in heads]
    for level in range(INV_BASE_LEVELS - 1):
        if level < INV_BASE_LEVELS - 2:
            both = [_dot(jnp.concatenate([inv[h].astype(BF16), a_pow16[h]], axis=0), a_pow16[h]) for h in heads]
            inv = [inv[h] + both[h][0:C] for h in heads]
            a_pow16 = [both[h][C:2 * C].astype(BF16) for h in heads]
        else:
            inv = [inv[h] + _dot(inv[h].astype(BF16), a_pow16[h]) for h in heads]
    b = INV_BASE
    while b < C:
        off = ((rb >> 1) == (cb >> 1)) & (rb != cb)
        a_off16 = [jnp.where(off, a[h], 0.0).astype(BF16) for h in heads]
        inv16 = [inv[h].astype(BF16) for h in heads]
        left = [_dot(inv16[h], a_off16[h]).astype(BF16) for h in heads]
        inv = [inv[h] - _dot(left[h], inv16[h]) for h in heads]
        rb = rb >> 1
        cb = cb >> 1
        b *= 2

    rhs16 = [jnp.concatenate([(chains[h]["v_ref"][:, cols[h]] * beta[h][:, col1[h]]).astype(BF16),
                              (kb[h] * e_gc[h][:, col1[h]]).astype(BF16)], axis=1) for h in heads]
    uw = [_dot(inv[h].astype(BF16), rhs16[h]) for h in heads]
    q_dec16 = [(chains[h]["q_ref"][:, cols[h]] * e_gc[h][:, col1[h]]).astype(BF16) for h in heads]
    k_dec_t16 = [(k[h] * chains[h]["e_rest"][:, col1[h]]).T.astype(BF16) for h in heads]

    s = [s_ref[ch["state"]] for ch in chains]
    s16 = [s[h].astype(BF16) for h in heads]
    ws = [_dot(jnp.concatenate([uw[h][:, HEAD_DIM:].astype(BF16), q_dec16[h]], axis=0), s16[h]) for h in heads]
    v_new16 = [(uw[h][:, 0:HEAD_DIM] - ws[h][0:C]).astype(BF16) for h in heads]
    for h, ch in enumerate(chains):
        ch["o_ref"][:, cols[h]] = ws[h][C:2 * C] + _dot(attn_qk16[h], v_new16[h])
    for h, ch in enumerate(chains):
        s_ref[ch["state"]] = s[h] * ch["e_end"][:, col1[h]] + _dot(k_dec_t16[h], v_new16[h])


def _gdn_scan(q, k, v, gcb, gct):
    L = q.shape[0]
    assert L % SCAN_CHUNK == 0
    n = L // SCAN_CHUNK
    fwd = lambda i: (i, 0)
    bwd = lambda i: (n - 1 - i, 0)
    tile_f = pl.BlockSpec((SCAN_CHUNK, GDN_W), fwd)
    tile_b = pl.BlockSpec((SCAN_CHUNK, GDN_W), bwd)
    return pl.pallas_call(
        _gdn_scan_kernel,
        grid=(n,),
        in_specs=[
            tile_f, tile_f, tile_f,
            pl.BlockSpec((SCAN_CHUNK, LANES), fwd),
            pl.BlockSpec((GDN_HEADS, SCAN_CHUNK), lambda i: (2 * i, 0)),
            tile_b, tile_b, tile_b,
            pl.BlockSpec((SCAN_CHUNK, LANES), bwd),
            pl.BlockSpec((GDN_HEADS, SCAN_CHUNK), lambda i: (2 * (n - 1 - i) + 1, 0)),
        ],
        out_specs=[tile_f, tile_b],
        out_shape=[jax.ShapeDtypeStruct((L, GDN_W), F32), jax.ShapeDtypeStruct((L, GDN_W), F32)],
        scratch_shapes=[pltpu.VMEM((2 * GDN_HEADS, HEAD_DIM, HEAD_DIM), F32)],
        compiler_params=_cparams(("arbitrary",)),
        name="gdn_scan",
    )(q, k, v, gcb, gct, q, k, v, gcb, gct)


def _flash_kernel(q_ref, k_ref, vt_ref, on_ref, o_ref, m_ref, acc_ref, st_ref, p_ref, *, tk):
    ki = pl.program_id(1)

    @pl.when(ki == 0)
    def _():
        m_ref[...] = jnp.full_like(m_ref, -jnp.inf)
        acc_ref[...] = jnp.zeros_like(acc_ref)

    ones = jnp.ones((FLASH_SUM_ROWS, tk), BF16)
    k = [k_ref[:, g * HEAD_DIM:(g + 1) * HEAD_DIM] for g in range(ATT_KV_HEADS)]
    vt1 = [jnp.concatenate([vt_ref[g * HEAD_DIM:(g + 1) * HEAD_DIM, :], ones], axis=0) for g in range(ATT_KV_HEADS)]
    blocks = [slice(r, r + FLASH_ROW_BLOCK) for r in range(0, tk, FLASH_ROW_BLOCK)]

    tq = q_ref.shape[0]
    units = [(h, slice(c, c + FLASH_Q_BLOCK)) for h in range(ATT_Q_HEADS) for c in range(0, tq, FLASH_Q_BLOCK)]

    def scores(h, qs):
        st_ref[h, :, qs] = _dot_nt(k[h // ATT_GROUP], q_ref[qs, h * HEAD_DIM:(h + 1) * HEAD_DIM])

    def softmax(h, qs):
        part = None
        for rows in blocks:
            x = st_ref[h, rows, qs]
            while x.shape[0] > SUBLANES:
                half = x.shape[0] // 2
                x = jnp.maximum(x[:half], x[half:])
            part = x if part is None else jnp.maximum(part, x)
        m_prev = m_ref[h, :, qs]
        m_new = jnp.maximum(m_prev, jnp.max(part, axis=0, keepdims=True))
        m_ref[h, :, qs] = m_new
        for rows in blocks:
            p_ref[h, rows, qs] = jnp.exp2(st_ref[h, rows, qs] - m_new).astype(BF16)
        return jnp.exp2(m_prev - m_new)

    def accumulate(h, qs, alpha):
        acc_ref[h, :, qs] = alpha * acc_ref[h, :, qs] + _dot(vt1[h // ATT_GROUP], p_ref[h, :, qs])

    scores(*units[0])
    pending = None
    for u, unit in enumerate(units):
        if u + 1 < len(units):
            scores(*units[u + 1])
        alpha = softmax(*unit)
        if pending is not None:
            accumulate(*pending)
        pending = (*unit, alpha)
    accumulate(*pending)

    @pl.when(ki == pl.num_programs(1) - 1)
    def _():
        for h in range(ATT_Q_HEADS):
            cols = slice(h * HEAD_DIM, (h + 1) * HEAD_DIM)
            o = (acc_ref[h, 0:HEAD_DIM, :] / acc_ref[h, HEAD_DIM:HEAD_DIM + 1, :]).T
            o_ref[:, cols] = _rms(o, on_ref[...]).astype(BF16)


def _flash_attn(q, k, vt, out_norm, *, tq=512, tk=1024):
    L = q.shape[0]
    assert L % tq == 0 and L % tk == 0 and tq % FLASH_Q_BLOCK == 0 and tk % FLASH_ROW_BLOCK == 0
    return pl.pallas_call(
        functools.partial(_flash_kernel, tk=tk),
        grid=(L // tq, L // tk),
        in_specs=[
            pl.BlockSpec((tq, ATT_Q_W), lambda i, j: (i, 0)),
            pl.BlockSpec((tk, ATT_KV_W), lambda i, j: (j, 0)),
            pl.BlockSpec((ATT_KV_W, tk), lambda i, j: (0, j)),
            pl.BlockSpec((1, HEAD_DIM), lambda i, j: (0, 0)),
        ],
        out_specs=pl.BlockSpec((tq, ATT_Q_W), lambda i, j: (i, 0)),
        out_shape=jax.ShapeDtypeStruct((L, ATT_Q_W), BF16),
        scratch_shapes=[
            pltpu.VMEM((ATT_Q_HEADS, 1, tq), F32),
            pltpu.VMEM((ATT_Q_HEADS, HEAD_DIM + FLASH_SUM_ROWS, tq), F32),
            pltpu.VMEM((ATT_Q_HEADS, tk, tq), F32),
            pltpu.VMEM((ATT_Q_HEADS, tk, tq), BF16),
        ],
        compiler_params=_cparams(("parallel", "arbitrary")),
        name="flash_attn",
    )(q, k, vt, out_norm)


def _out_proj_kernel(of_ref, ob_ref, z_ref, oa_ref, h_ref, gn_ref, w_ref, o_ref, mix_ref):
    att = _dot(oa_ref[...], w_ref[GDN_W:, :])
    for h in range(GDN_HEADS):
        cols = slice(h * HEAD_DIM, (h + 1) * HEAD_DIM)
        o = _rms(of_ref[:, cols] + ob_ref[:, cols], gn_ref[...])
        z = z_ref[:, cols]
        mix_ref[:, cols] = (o * (z * jax.nn.sigmoid(z))).astype(BF16)
    o_ref[...] = h_ref[...] + (att + _dot(mix_ref[...], w_ref[0:GDN_W, :]))


def _out_proj(o_fwd, o_bwd, z, oa, h1, gdn_norm, w_out, *, tm=512):
    L = h1.shape[0]
    assert L % tm == 0
    row = lambda i: (i, 0)
    return pl.pallas_call(
        _out_proj_kernel,
        grid=(L // tm,),
        in_specs=[
            pl.BlockSpec((tm, GDN_W), row),
            pl.BlockSpec((tm, GDN_W), row),
            pl.BlockSpec((tm, GDN_W), row),
            pl.BlockSpec((tm, ATT_Q_W), row),
            pl.BlockSpec((tm, D_MODEL), row),
            pl.BlockSpec((1, HEAD_DIM), lambda i: (0, 0)),
            pl.BlockSpec((GDN_W + ATT_Q_W, D_MODEL), lambda i: (0, 0)),
        ],
        out_specs=pl.BlockSpec((tm, D_MODEL), row),
        out_shape=jax.ShapeDtypeStruct((L, D_MODEL), F32),
        scratch_shapes=[pltpu.VMEM((tm, GDN_W), BF16)],
        compiler_params=_cparams(("parallel",)),
        name="out_proj",
    )(o_fwd, o_bwd, z, oa, h1, gdn_norm, w_out)


def _rope_tables(L):
    assert L % GRID_W == 0
    rows = L // GRID_W
    freqs = ROPE_THETA ** (-jnp.arange(0, AXIS_DIM, 2, dtype=F32) / AXIS_DIM)
    ang_r = jnp.arange(rows, dtype=F32)[:, None] * freqs[None, :]
    ang_c = jnp.arange(GRID_W, dtype=F32)[:, None] * freqs[None, :]
    per_row = lambda tab: jnp.repeat(tab, GRID_W, axis=0)
    per_col = lambda tab: jnp.tile(tab, (rows, 1))
    cos_r, sin_r = per_row(jnp.cos(ang_r)), per_row(jnp.sin(ang_r))
    cos_c, sin_c = per_col(jnp.cos(ang_c)), per_col(jnp.sin(ang_c))
    zero = jnp.zeros_like(cos_r)
    cos = jnp.concatenate([cos_r, cos_r, cos_c, cos_c], axis=-1)
    sin_lo = jnp.concatenate([-sin_r, zero, -sin_c, zero], axis=-1)
    sin_hi = jnp.concatenate([zero, sin_r, zero, sin_c], axis=-1)
    return cos, sin_lo, sin_hi


def _pad_lanes(x):
    x = x.reshape(1, -1)
    return jnp.pad(x, ((0, 0), (0, LANES - x.shape[1])))


def _encode(x, p, rope):
    h1 = _ffn(x, p["ffn1_norm"], p["ffn1_wg"], p["ffn1_wu"], p["ffn1_wd"], p["final_norm"], final=False)
    cos, sin_lo, sin_hi = rope
    qg, kg, vg, z, qa, ka, va, gcb, gct = _mix_proj(h1, p["mix_norm"], p["w_in"], p["w_qk"], p["w_vt"], p["w_ab"],
                                                    p["q_norm"], p["k_norm"], cos, sin_lo, sin_hi, p["alog"],
                                                    p["dtb"], p["conv_w"])
    o_fwd, o_bwd = _gdn_scan(qg, kg, vg, gcb, gct)
    oa = _flash_attn(qa, ka, va, p["attn_out_norm"])
    h2 = _out_proj(o_fwd, o_bwd, z, oa, h1, p["gdn_out_norm"], p["w_out"])
    return _ffn(h2, p["ffn2_norm"], p["ffn2_wg"], p["ffn2_wu"], p["ffn2_wd"], p["final_norm"], final=True)


def _prepare_params(ffn1_norm, ffn1_w_gate, ffn1_w_up, ffn1_w_down, mix_norm, w_in, conv_w, a_log_fwd, a_log_bwd,
                    dt_bias_fwd, dt_bias_bwd, gdn_out_norm, q_norm, k_norm, attn_out_norm, w_out, ffn2_norm,
                    ffn2_w_gate, ffn2_w_up, ffn2_w_down, final_norm):
    w_in0 = w_in[0]
    att0 = GDN_PROJ_W + GATE_COLS
    w_ab = jnp.pad(w_in0[:, GDN_PROJ_W:att0], ((0, 0), (0, LANES - GATE_COLS)))
    return dict(
        ffn1_norm=ffn1_norm[0].reshape(1, -1),
        ffn1_wg=ffn1_w_gate[0].astype(BF16), ffn1_wu=ffn1_w_up[0].astype(BF16), ffn1_wd=ffn1_w_down[0].astype(BF16),
        mix_norm=mix_norm[0].reshape(1, -1),
        w_in=w_in0[:, :GDN_PROJ_W].astype(BF16),
        w_qk=w_in0[:, att0:att0 + ATT_Q_W + ATT_KV_W].astype(BF16),
        w_vt=w_in0[:, att0 + ATT_Q_W + ATT_KV_W:].T.astype(BF16),
        w_ab=w_ab.astype(BF16),
        conv_w=jnp.pad(conv_w[0], ((0, SUBLANES - CONV_K), (0, 0))),
        alog=_pad_lanes(jnp.concatenate([a_log_fwd[0], a_log_bwd[0]])),
        dtb=_pad_lanes(jnp.concatenate([dt_bias_fwd[0], dt_bias_bwd[0]])),
        gdn_out_norm=gdn_out_norm[0].reshape(1, -1),
        q_norm=q_norm[0].reshape(1, -1), k_norm=k_norm[0].reshape(1, -1),
        attn_out_norm=attn_out_norm[0].reshape(1, -1),
        w_out=w_out[0].astype(BF16),
        ffn2_norm=ffn2_norm[0].reshape(1, -1),
        ffn2_wg=ffn2_w_gate[0].astype(BF16), ffn2_wu=ffn2_w_up[0].astype(BF16), ffn2_wd=ffn2_w_down[0].astype(BF16),
        final_norm=final_norm.reshape(1, -1),
    )


def kernel(x_prompt, x_sample, ffn1_norm, ffn1_w_gate, ffn1_w_up, ffn1_w_down, mix_norm, w_in, conv_w, a_log_fwd,
           a_log_bwd, dt_bias_fwd, dt_bias_bwd, gdn_out_norm, q_norm, k_norm, attn_out_norm, w_out, ffn2_norm,
           ffn2_w_gate, ffn2_w_up, ffn2_w_down, final_norm):
    assert x_prompt.shape[0] == 1 and x_sample.shape[0] == 1
    p = _prepare_params(ffn1_norm, ffn1_w_gate, ffn1_w_up, ffn1_w_down, mix_norm, w_in, conv_w, a_log_fwd,
                        a_log_bwd, dt_bias_fwd, dt_bias_bwd, gdn_out_norm, q_norm, k_norm, attn_out_norm, w_out,
                        ffn2_norm, ffn2_w_gate, ffn2_w_up, ffn2_w_down, final_norm)
    rope = _rope_tables(max(x_prompt.shape[1], x_sample.shape[1]))
    y_prompt = _encode(x_prompt[0], p, rope)
    y_sample = _encode(x_sample[0], p, rope)
    return (y_prompt[None], y_sample[None])
```

```python
import functools
import math

import jax
import jax.numpy as jnp
from jax import lax
from jax.experimental import pallas as pl
from jax.experimental.pallas import tpu as pltpu

D_MODEL = 2048
HEAD_DIM = 128
GDN_HEADS = 8
GDN_W = GDN_HEADS * HEAD_DIM
ATT_Q_HEADS = 8
ATT_KV_HEADS = 2
ATT_GROUP = ATT_Q_HEADS // ATT_KV_HEADS
ATT_Q_W = ATT_Q_HEADS * HEAD_DIM
ATT_KV_W = ATT_KV_HEADS * HEAD_DIM
GDN_PROJ_W = 4 * GDN_W
GATE_COLS = 4 * GDN_HEADS
D_FF = 5632
CONV_K = 5
GRID_W = 64
AXIS_DIM = HEAD_DIM // 2
ROPE_THETA = 10000.0
EPS = 1e-6

LANES = 128
SUBLANES = 8
SCAN_CHUNK = 128
PROJ_COL_TILE = 256
CONV_HALO = 16
INV_BASE_LEVELS = 2
INV_BASE = 2 ** INV_BASE_LEVELS
EXP2_SCALE = HEAD_DIM ** -0.5 * math.log2(math.e)
FLASH_ROW_BLOCK = 64
FLASH_SUM_ROWS = 16
FLASH_Q_BLOCK = 512
F32 = jnp.float32
BF16 = jnp.bfloat16

V7X_VMEM_BYTES = 64 * 1024 * 1024
_VMEM_LIMIT = V7X_VMEM_BYTES * 7 // 8


def _cparams(semantics):
    return pltpu.CompilerParams(dimension_semantics=semantics, vmem_limit_bytes=_VMEM_LIMIT)


def _rms(x, g):
    return x * lax.rsqrt(jnp.mean(x * x, axis=-1, keepdims=True) + EPS) * g


def _dot(a, b):
    return jnp.dot(a, b, preferred_element_type=F32)


def _dot_nt(a, b):
    return lax.dot_general(a, b, (((1,), (1,)), ((), ())), preferred_element_type=F32)


def _split3(x):
    hi = x.astype(BF16)
    r = x - hi.astype(F32)
    mid = r.astype(BF16)
    lo = (r - mid.astype(F32)).astype(BF16)
    return hi, mid, lo


def _ffn_kernel(x_ref, g_ref, wg_ref, wu_ref, wd_ref, fg_ref, o_ref, xn_ref, *, final):
    j = pl.program_id(1)

    @pl.when(j == 0)
    def _():
        xn_ref[...] = _rms(x_ref[...], g_ref[...]).astype(BF16)
        o_ref[...] = jnp.zeros_like(o_ref)

    xn = xn_ref[...]
    gate = _dot(xn, wg_ref[...])
    up = _dot(xn, wu_ref[...])
    act = (gate * jax.nn.sigmoid(gate) * up).astype(BF16)
    o_ref[...] += _dot(act, wd_ref[...])

    @pl.when(j == pl.num_programs(1) - 1)
    def _():
        h = x_ref[...] + 0.5 * o_ref[...]
        if final:
            h = _rms(h, fg_ref[...])
        o_ref[...] = h


def _ffn(x, norm_g, wg, wu, wd, final_g, *, final, tm=512, tf=512):
    L = x.shape[0]
    assert L % tm == 0 and D_FF % tf == 0
    grid = (L // tm, D_FF // tf)
    return pl.pallas_call(
        functools.partial(_ffn_kernel, final=final),
        grid=grid,
        in_specs=[
            pl.BlockSpec((tm, D_MODEL), lambda i, j: (i, 0)),
            pl.BlockSpec((1, D_MODEL), lambda i, j: (0, 0)),
            pl.BlockSpec((D_MODEL, tf), lambda i, j: (0, j)),
            pl.BlockSpec((D_MODEL, tf), lambda i, j: (0, j)),
            pl.BlockSpec((tf, D_MODEL), lambda i, j: (j, 0)),
            pl.BlockSpec((1, D_MODEL), lambda i, j: (0, 0)),
        ],
        out_specs=pl.BlockSpec((tm, D_MODEL), lambda i, j: (i, 0)),
        out_shape=jax.ShapeDtypeStruct((L, D_MODEL), F32),
        scratch_shapes=[pltpu.VMEM((tm, D_MODEL), BF16)],
        compiler_params=_cparams(("parallel", "arbitrary")),
        name="ffn",
    )(x, norm_g, wg, wu, wd, final_g)


def _rope(x, cos, sin_lo, sin_hi):
    return (x * cos + pltpu.roll(x, AXIS_DIM // 2, axis=1) * sin_hi
            + pltpu.roll(x, HEAD_DIM - AXIS_DIM // 2, axis=1) * sin_lo)


def _mix_proj_kernel(x_ref, xprev_ref, xnext_ref, g_ref, wg_ref, w_ref, wvt_ref, wab_ref, qn_ref, kn_ref, cos_ref,
                     slo_ref, shi_ref, alog_ref, dtb_ref, cw_ref, qg_ref, kg_ref, vg_ref, z_ref, q_ref, k_ref,
                     vt_ref, gcb_ref, gct_ref, ext_ref, *, tm):
    i = pl.program_id(0)
    pad = CONV_K // 2
    g = g_ref[...]
    xn = _rms(x_ref[...], g).astype(BF16)

    xn_ext = jnp.concatenate([_rms(xprev_ref[...], g).astype(BF16), xn, _rms(xnext_ref[...], g).astype(BF16)], axis=0)
    keep_prev = jnp.where(i == 0, 0.0, 1.0)
    keep_next = jnp.where(i == pl.num_programs(0) - 1, 0.0, 1.0)
    cos = cos_ref[...]
    slo = slo_ref[...]
    shi = shi_ref[...]

    def project_ext(c):
        cs = slice(c, c + PROJ_COL_TILE)
        e = _dot(xn_ext, wg_ref[:, cs])
        ext_ref[0:CONV_HALO, cs] = e[0:CONV_HALO] * keep_prev
        ext_ref[CONV_HALO:CONV_HALO + tm, cs] = e[CONV_HALO:CONV_HALO + tm]
        ext_ref[CONV_HALO + tm:, cs] = e[CONV_HALO + tm:] * keep_next

    def project_z(c):
        z_ref[:, c:c + PROJ_COL_TILE] = _dot(xn, wg_ref[:, 3 * GDN_W + c:3 * GDN_W + c + PROJ_COL_TILE])

    def project_attn_pair(h):
        pair = _dot(xn, w_ref[:, h * HEAD_DIM:(h + 2) * HEAD_DIM])
        for half in range(2):
            head = pair[:, half * HEAD_DIM:(half + 1) * HEAD_DIM]
            n = h + half
            if n < ATT_Q_HEADS:
                q_ref[:, n * HEAD_DIM:(n + 1) * HEAD_DIM] = (
                    _rope(_rms(head, qn_ref[...]), cos, slo, shi) * EXP2_SCALE).astype(BF16)
            else:
                n -= ATT_Q_HEADS
                k_ref[:, n * HEAD_DIM:(n + 1) * HEAD_DIM] = _rope(_rms(head, kn_ref[...]), cos, slo, shi).astype(BF16)

    def project_vt():
        vt_ref[...] = _dot_nt(wvt_ref[...], xn).astype(BF16)

    def conv_slab(s):
        cols = slice(s * HEAD_DIM, (s + 1) * HEAD_DIM)
        y = None
        for t in range(CONV_K):
            term = ext_ref[CONV_HALO - pad + t:CONV_HALO - pad + t + tm, cols] * cw_ref[t:t + 1, cols]
            y = term if y is None else y + term
        y = y * jax.nn.sigmoid(y)
        part, h = divmod(s, GDN_HEADS)
        hc = slice(h * HEAD_DIM, (h + 1) * HEAD_DIM)
        if part == 0:
            qg_ref[:, hc] = y * (lax.rsqrt(jnp.sum(y * y, axis=-1, keepdims=True) + EPS) * (HEAD_DIM ** -0.5))
        elif part == 1:
            kg_ref[:, hc] = y * lax.rsqrt(jnp.sum(y * y, axis=-1, keepdims=True) + EPS)
        else:
            vg_ref[:, hc] = y

    tiles = range(0, GDN_W, PROJ_COL_TILE)
    mxu_q = [functools.partial(project_ext, c) for c in tiles]
    mxu_k = [functools.partial(project_ext, GDN_W + c) for c in tiles]
    mxu_v = [functools.partial(project_ext, 2 * GDN_W + c) for c in tiles]
    mxu_z = [functools.partial(project_z, c) for c in tiles]
    mxu_att = [functools.partial(project_attn_pair, h) for h in range(0, ATT_Q_HEADS + ATT_KV_HEADS, 2)]
    mxu_att.append(project_vt)
    for piece in mxu_q:
        piece()
    n_att = len(mxu_att) // 2
    stages = [(mxu_k + mxu_att[:n_att], range(0, GDN_HEADS)),
              (mxu_v + mxu_att[n_att:] + mxu_z[:2], range(GDN_HEADS, 2 * GDN_HEADS)),
              (mxu_z[2:], range(2 * GDN_HEADS, 3 * GDN_HEADS))]
    for pieces, slabs in stages:
        for n in range(max(len(pieces), len(slabs))):
            if n < len(pieces):
                pieces[n]()
            if n < len(slabs):
                conv_slab(slabs[n])

    ab = _dot(xn, wab_ref[...])
    t = ab + dtb_ref[...]
    softplus = jnp.maximum(t, 0.0) + jnp.log1p(jnp.exp(-jnp.abs(t)))
    log_decay = -jnp.exp(alog_ref[...]) * softplus
    lane = lax.broadcasted_iota(jnp.int32, ab.shape, 1)
    gb = jnp.where(lane < 2 * GDN_HEADS, log_decay, jax.nn.sigmoid(ab))

    r = lax.broadcasted_iota(jnp.int32, (tm, tm), 0)
    c = lax.broadcasted_iota(jnp.int32, (tm, tm), 1)
    same = (r // SCAN_CHUNK) == (c // SCAN_CHUNK)
    m_lo = jnp.where(same & (c <= r), 1.0, 0.0).astype(BF16)
    m_up = jnp.where(same & (c >= r), 1.0, 0.0).astype(BF16)
    parts = _split3(gb)
    gc_f = _dot(m_lo, parts[0]) + _dot(m_lo, parts[1]) + _dot(m_lo, parts[2])
    gc_b = _dot(m_up, parts[0]) + _dot(m_up, parts[1]) + _dot(m_up, parts[2])
    gcb = jnp.where(lane < GDN_HEADS, gc_f, jnp.where(lane < 2 * GDN_HEADS, gc_b, gb))
    gcb_ref[...] = gcb
    for n in range(tm // SCAN_CHUNK):
        gct = gcb[n * SCAN_CHUNK:(n + 1) * SCAN_CHUNK, :].T
        gct_ref[n * 2 * GDN_HEADS:(n + 1) * 2 * GDN_HEADS, :] = gct[0:2 * GDN_HEADS, :]


def _mix_proj(x, norm_g, w_in, w_qk, w_vt, w_ab, q_norm, k_norm, cos, sin_lo, sin_hi, alog, dtb, conv_w8, *, tm=256):
    L = x.shape[0]
    assert L % tm == 0 and tm % SCAN_CHUNK == 0 and tm % CONV_HALO == 0
    n_halo = L // CONV_HALO
    per = tm // CONV_HALO
    row = lambda i: (i, 0)
    fixed = lambda i: (0, 0)
    resident = pl.Buffered(1)
    return pl.pallas_call(
        functools.partial(_mix_proj_kernel, tm=tm),
        grid=(L // tm,),
        in_specs=[
            pl.BlockSpec((tm, D_MODEL), row),
            pl.BlockSpec((CONV_HALO, D_MODEL), lambda i: (jnp.maximum(i * per - 1, 0), 0)),
            pl.BlockSpec((CONV_HALO, D_MODEL), lambda i: (jnp.minimum((i + 1) * per, n_halo - 1), 0)),
            pl.BlockSpec((1, D_MODEL), fixed),
            pl.BlockSpec((D_MODEL, GDN_PROJ_W), fixed, pipeline_mode=resident),
            pl.BlockSpec((D_MODEL, ATT_Q_W + ATT_KV_W), fixed, pipeline_mode=resident),
            pl.BlockSpec((ATT_KV_W, D_MODEL), fixed, pipeline_mode=resident),
            pl.BlockSpec((D_MODEL, LANES), fixed),
            pl.BlockSpec((1, HEAD_DIM), fixed),
            pl.BlockSpec((1, HEAD_DIM), fixed),
            pl.BlockSpec((tm, HEAD_DIM), row),
            pl.BlockSpec((tm, HEAD_DIM), row),
            pl.BlockSpec((tm, HEAD_DIM), row),
            pl.BlockSpec((1, LANES), fixed),
            pl.BlockSpec((1, LANES), fixed),
            pl.BlockSpec((SUBLANES, 3 * GDN_W), fixed),
        ],
        out_specs=[
            pl.BlockSpec((tm, GDN_W), row),
            pl.BlockSpec((tm, GDN_W), row),
            pl.BlockSpec((tm, GDN_W), row),
            pl.BlockSpec((tm, GDN_W), row),
            pl.BlockSpec((tm, ATT_Q_W), row),
            pl.BlockSpec((tm, ATT_KV_W), row),
            pl.BlockSpec((ATT_KV_W, tm), lambda i: (0, i)),
            pl.BlockSpec((tm, LANES), row),
            pl.BlockSpec((tm // SCAN_CHUNK * 2 * GDN_HEADS, SCAN_CHUNK), row),
        ],
        out_shape=[
            jax.ShapeDtypeStruct((L, GDN_W), F32),
            jax.ShapeDtypeStruct((L, GDN_W), F32),
            jax.ShapeDtypeStruct((L, GDN_W), F32),
            jax.ShapeDtypeStruct((L, GDN_W), F32),
            jax.ShapeDtypeStruct((L, ATT_Q_W), BF16),
            jax.ShapeDtypeStruct((L, ATT_KV_W), BF16),
            jax.ShapeDtypeStruct((ATT_KV_W, L), BF16),
            jax.ShapeDtypeStruct((L, LANES), F32),
            jax.ShapeDtypeStruct((L // SCAN_CHUNK * 2 * GDN_HEADS, SCAN_CHUNK), F32),
        ],
        scratch_shapes=[pltpu.VMEM((tm + 2 * CONV_HALO, 3 * GDN_W), F32)],
        compiler_params=_cparams(("parallel",)),
        name="mix_proj",
    )(x, x, x, norm_g, w_in, w_qk, w_vt, w_ab, q_norm, k_norm, cos, sin_lo, sin_hi, alog, dtb, conv_w8)


def _gdn_scan_kernel(qf_ref, kf_ref, vf_ref, gcbf_ref, gctf_ref, qb_ref, kb_ref, vb_ref, gcbb_ref, gctb_ref,
                     of_ref, ob_ref, s_ref):
    C = SCAN_CHUNK

    @pl.when(pl.program_id(0) == 0)
    def _():
        s_ref[...] = jnp.zeros_like(s_ref)

    r = lax.broadcasted_iota(jnp.int32, (C, C), 0)
    c = lax.broadcasted_iota(jnp.int32, (C, C), 1)
    eye = jnp.where(r == c, 1.0, 0.0)

    chains = []
    for fwd, (q_ref, k_ref, v_ref, gcb_ref, gct_ref, o_ref) in (
            (True, (qf_ref, kf_ref, vf_ref, gcbf_ref, gctf_ref, of_ref)),
            (False, (qb_ref, kb_ref, vb_ref, gcbb_ref, gctb_ref, ob_ref))):
        dif = r - c if fwd else c - r
        lane0 = 0 if fwd else GDN_HEADS
        gcb = gcb_ref[...]
        gc = gcb[:, lane0:lane0 + GDN_HEADS]
        beta = gcb[:, 2 * GDN_HEADS + lane0:3 * GDN_HEADS + lane0]
        g_end = gc[C - 1:C, :] if fwd else gc[0:1, :]
        shared = dict(q_ref=q_ref, k_ref=k_ref, v_ref=v_ref, o_ref=o_ref, m_incl=dif >= 0, m_strict=dif > 0,
                      gc=gc, beta=beta, e_gc=jnp.exp(gc), e_rest=jnp.exp(g_end - gc), e_end=jnp.exp(g_end),
                      gct=gct_ref[...])
        for h in range(GDN_HEADS):
            chains.append(dict(shared, cols=slice(h * HEAD_DIM, (h + 1) * HEAD_DIM), col1=slice(h, h + 1),
                               state=lane0 + h))
    heads = range(len(chains))
    cols = [ch["cols"] for ch in chains]
    col1 = [ch["col1"] for ch in chains]
    beta = [ch["beta"] for ch in chains]
    e_gc = [ch["e_gc"] for ch in chains]

    k = [chains[h]["k_ref"][:, cols[h]] for h in heads]
    k16 = [k[h].astype(BF16) for h in heads]
    kb = [k[h] * beta[h][:, col1[h]] for h in heads]
    q16 = [chains[h]["q_ref"][:, cols[h]].astype(BF16) for h in heads]
    kq = [_dot_nt(jnp.concatenate([kb[h].astype(BF16), q16[h]], axis=0), k16[h]) for h in heads]
    decay = []
    for ch in chains:
        diff = ch["gc"][:, ch["col1"]] - ch["gct"][ch["col1"], :]
        decay.append(jnp.where(ch["m_incl"], jnp.exp(jnp.where(ch["m_incl"], diff, 0.0)), 0.0))
    a = [jnp.where(chains[h]["m_strict"], kq[h][0:C] * decay[h], 0.0) for h in heads]
    attn_qk16 = [(kq[h][C:2 * C] * decay[h]).astype(BF16) for h in heads]

    rb = r >> INV_BASE_LEVELS
    cb = c >> INV_BASE_LEVELS
    a_d = [jnp.where(rb == cb, a[h], 0.0) for h in heads]
    inv = [eye - a_d[h] for h in heads]
    a_d16 = [a_d[h].astype(BF16) for h in heads]
    a_pow16 = [_dot(a_d16[h], a_d16[h]).astype(BF16) for h in heads]
    for level in range(INV_BASE_LEVELS - 1):
        if level < INV_BASE_LEVELS - 2:
            both = [_dot(jnp.concatenate([inv[h].astype(BF16), a_pow16[h]], axis=0), a_pow16[h]) for h in heads]
            inv = [inv[h] + both[h][0:C] for h in heads]
            a_pow16 = [both[h][C:2 * C].astype(BF16) for h in heads]
        else:
            inv = [inv[h] + _dot(inv[h].astype(BF16), a_pow16[h]) for h in heads]
    b = INV_BASE
    while b < C:
        off = ((rb >> 1) == (cb >> 1)) & (rb != cb)
        a_off16 = [jnp.where(off, a[h], 0.0).astype(BF16) for h in heads]
        inv16 = [inv[h].astype(BF16) for h in heads]
        left = [_dot(inv16[h], a_off16[h]).astype(BF16) for h in heads]
        inv = [inv[h] - _dot(left[h], inv16[h]) for h in heads]
        rb = rb >> 1
        cb = cb >> 1
        b *= 2

    rhs16 = [jnp.concatenate([(chains[h]["v_ref"][:, cols[h]] * beta[h][:, col1[h]]).astype(BF16),
                              (kb[h] * e_gc[h][:, col1[h]]).astype(BF16)], axis=1) for h in heads]
    uw = [_dot(inv[h].astype(BF16), rhs16[h]) for h in heads]
    q_dec16 = [(chains[h]["q_ref"][:, cols[h]] * e_gc[h][:, col1[h]]).astype(BF16) for h in heads]
    k_dec_t16 = [(k[h] * chains[h]["e_rest"][:, col1[h]]).T.astype(BF16) for h in heads]

    s = [s_ref[ch["state"]] for ch in chains]
    s16 = [s[h].astype(BF16) for h in heads]
    ws = [_dot(jnp.concatenate([uw[h][:, HEAD_DIM:].astype(BF16), q_dec16[h]], axis=0), s16[h]) for h in heads]
    v_new16 = [(uw[h][:, 0:HEAD_DIM] - ws[h][0:C]).astype(BF16) for h in heads]
    for h, ch in enumerate(chains):
        ch["o_ref"][:, cols[h]] = ws[h][C:2 * C] + _dot(attn_qk16[h], v_new16[h])
    for h, ch in enumerate(chains):
        s_ref[ch["state"]] = s[h] * ch["e_end"][:, col1[h]] + _dot(k_dec_t16[h], v_new16[h])


def _gdn_scan(q, k, v, gcb, gct):
    L = q.shape[0]
    assert L % SCAN_CHUNK == 0
    n = L // SCAN_CHUNK
    fwd = lambda i: (i, 0)
    bwd = lambda i: (n - 1 - i, 0)
    tile_f = pl.BlockSpec((SCAN_CHUNK, GDN_W), fwd)
    tile_b = pl.BlockSpec((SCAN_CHUNK, GDN_W), bwd)
    return pl.pallas_call(
        _gdn_scan_kernel,
        grid=(n,),
        in_specs=[
            tile_f, tile_f, tile_f,
            pl.BlockSpec((SCAN_CHUNK, LANES), fwd),
            pl.BlockSpec((GDN_HEADS, SCAN_CHUNK), lambda i: (2 * i, 0)),
            tile_b, tile_b, tile_b,
            pl.BlockSpec((SCAN_CHUNK, LANES), bwd),
            pl.BlockSpec((GDN_HEADS, SCAN_CHUNK), lambda i: (2 * (n - 1 - i) + 1, 0)),
        ],
        out_specs=[tile_f, tile_b],
        out_shape=[jax.ShapeDtypeStruct((L, GDN_W), F32), jax.ShapeDtypeStruct((L, GDN_W), F32)],
        scratch_shapes=[pltpu.VMEM((2 * GDN_HEADS, HEAD_DIM, HEAD_DIM), F32)],
        compiler_params=_cparams(("arbitrary",)),
        name="gdn_scan",
    )(q, k, v, gcb, gct, q, k, v, gcb, gct)


def _flash_kernel(q_ref, k_ref, vt_ref, on_ref, o_ref, m_ref, acc_ref, st_ref, p_ref, *, tk):
    ki = pl.program_id(1)

    @pl.when(ki == 0)
    def _():
        m_ref[...] = jnp.full_like(m_ref, -jnp.inf)
        acc_ref[...] = jnp.zeros_like(acc_ref)

    ones = jnp.ones((FLASH_SUM_ROWS, tk), BF16)
    k = [k_ref[:, g * HEAD_DIM:(g + 1) * HEAD_DIM] for g in range(ATT_KV_HEADS)]
    vt1 = [jnp.concatenate([vt_ref[g * HEAD_DIM:(g + 1) * HEAD_DIM, :], ones], axis=0) for g in range(ATT_KV_HEADS)]
    blocks = [slice(r, r + FLASH_ROW_BLOCK) for r in range(0, tk, FLASH_ROW_BLOCK)]

    tq = q_ref.shape[0]
    units = [(h, slice(c, c + FLASH_Q_BLOCK)) for h in range(ATT_Q_HEADS) for c in range(0, tq, FLASH_Q_BLOCK)]

    def scores(h, qs):
        st_ref[h, :, qs] = _dot_nt(k[h // ATT_GROUP], q_ref[qs, h * HEAD_DIM:(h + 1) * HEAD_DIM])

    def softmax(h, qs):
        part = None
        for rows in blocks:
            x = st_ref[h, rows, qs]
            while x.shape[0] > SUBLANES:
                half = x.shape[0] // 2
                x = jnp.maximum(x[:half], x[half:])
            part = x if part is None else jnp.maximum(part, x)
        m_prev = m_ref[h, :, qs]
        m_new = jnp.maximum(m_prev, jnp.max(part, axis=0, keepdims=True))
        m_ref[h, :, qs] = m_new
        for rows in blocks:
            p_ref[h, rows, qs] = jnp.exp2(st_ref[h, rows, qs] - m_new).astype(BF16)
        return jnp.exp2(m_prev - m_new)

    def accumulate(h, qs, alpha):
        acc_ref[h, :, qs] = alpha * acc_ref[h, :, qs] + _dot(vt1[h // ATT_GROUP], p_ref[h, :, qs])

    scores(*units[0])
    pending = None
    for u, unit in enumerate(units):
        if u + 1 < len(units):
            scores(*units[u + 1])
        alpha = softmax(*unit)
        if pending is not None:
            accumulate(*pending)
        pending = (*unit, alpha)
    accumulate(*pending)

    @pl.when(ki == pl.num_programs(1) - 1)
    def _():
        for h in range(ATT_Q_HEADS):
            cols = slice(h * HEAD_DIM, (h + 1) * HEAD_DIM)
            o = (acc_ref[h, 0:HEAD_DIM, :] / acc_ref[h, HEAD_DIM:HEAD_DIM + 1, :]).T
            o_ref[:, cols] = _rms(o, on_ref[...]).astype(BF16)


def _flash_attn(q, k, vt, out_norm, *, tq=512, tk=1024):
    L = q.shape[0]
    assert L % tq == 0 and L % tk == 0 and tq % FLASH_Q_BLOCK == 0 and tk % FLASH_ROW_BLOCK == 0
    return pl.pallas_call(
        functools.partial(_flash_kernel, tk=tk),
        grid=(L // tq, L // tk),
        in_specs=[
            pl.BlockSpec((tq, ATT_Q_W), lambda i, j: (i, 0)),
            pl.BlockSpec((tk, ATT_KV_W), lambda i, j: (j, 0)),
            pl.BlockSpec((ATT_KV_W, tk), lambda i, j: (0, j)),
            pl.BlockSpec((1, HEAD_DIM), lambda i, j: (0, 0)),
        ],
        out_specs=pl.BlockSpec((tq, ATT_Q_W), lambda i, j: (i, 0)),
        out_shape=jax.ShapeDtypeStruct((L, ATT_Q_W), BF16),
        scratch_shapes=[
            pltpu.VMEM((ATT_Q_HEADS, 1, tq), F32),
            pltpu.VMEM((ATT_Q_HEADS, HEAD_DIM + FLASH_SUM_ROWS, tq), F32),
            pltpu.VMEM((ATT_Q_HEADS, tk, tq), F32),
            pltpu.VMEM((ATT_Q_HEADS, tk, tq), BF16),
        ],
        compiler_params=_cparams(("parallel", "arbitrary")),
        name="flash_attn",
    )(q, k, vt, out_norm)


def _out_proj_kernel(of_ref, ob_ref, z_ref, oa_ref, h_ref, gn_ref, w_ref, o_ref, mix_ref):
    att = _dot(oa_ref[...], w_ref[GDN_W:, :])
    for h in range(GDN_HEADS):
        cols = slice(h * HEAD_DIM, (h + 1) * HEAD_DIM)
        o = _rms(of_ref[:, cols] + ob_ref[:, cols], gn_ref[...])
        z = z_ref[:, cols]
        mix_ref[:, cols] = (o * (z * jax.nn.sigmoid(z))).astype(BF16)
    o_ref[...] = h_ref[...] + (att + _dot(mix_ref[...], w_ref[0:GDN_W, :]))


def _out_proj(o_fwd, o_bwd, z, oa, h1, gdn_norm, w_out, *, tm=512):
    L = h1.shape[0]
    assert L % tm == 0
    row = lambda i: (i, 0)
    return pl.pallas_call(
        _out_proj_kernel,
        grid=(L // tm,),
        in_specs=[
            pl.BlockSpec((tm, GDN_W), row),
            pl.BlockSpec((tm, GDN_W), row),
            pl.BlockSpec((tm, GDN_W), row),
            pl.BlockSpec((tm, ATT_Q_W), row),
            pl.BlockSpec((tm, D_MODEL), row),
            pl.BlockSpec((1, HEAD_DIM), lambda i: (0, 0)),
            pl.BlockSpec((GDN_W + ATT_Q_W, D_MODEL), lambda i: (0, 0)),
        ],
        out_specs=pl.BlockSpec((tm, D_MODEL), row),
        out_shape=jax.ShapeDtypeStruct((L, D_MODEL), F32),
        scratch_shapes=[pltpu.VMEM((tm, GDN_W), BF16)],
        compiler_params=_cparams(("parallel",)),
        name="out_proj",
    )(o_fwd, o_bwd, z, oa, h1, gdn_norm, w_out)


def _rope_tables(L):
    assert L % GRID_W == 0
    rows = L // GRID_W
    freqs = ROPE_THETA ** (-jnp.arange(0, AXIS_DIM, 2, dtype=F32) / AXIS_DIM)
    ang_r = jnp.arange(rows, dtype=F32)[:, None] * freqs[None, :]
    ang_c = jnp.arange(GRID_W, dtype=F32)[:, None] * freqs[None, :]
    per_row = lambda tab: jnp.repeat(tab, GRID_W, axis=0)
    per_col = lambda tab: jnp.tile(tab, (rows, 1))
    cos_r, sin_r = per_row(jnp.cos(ang_r)), per_row(jnp.sin(ang_r))
    cos_c, sin_c = per_col(jnp.cos(ang_c)), per_col(jnp.sin(ang_c))
    zero = jnp.zeros_like(cos_r)
    cos = jnp.concatenate([cos_r, cos_r, cos_c, cos_c], axis=-1)
    sin_lo = jnp.concatenate([-sin_r, zero, -sin_c, zero], axis=-1)
    sin_hi = jnp.concatenate([zero, sin_r, zero, sin_c], axis=-1)
    return cos, sin_lo, sin_hi


def _pad_lanes(x):
    x = x.reshape(1, -1)
    return jnp.pad(x, ((0, 0), (0, LANES - x.shape[1])))


def _encode(x, p, rope):
    h1 = _ffn(x, p["ffn1_norm"], p["ffn1_wg"], p["ffn1_wu"], p["ffn1_wd"], p["final_norm"], final=False)
    cos, sin_lo, sin_hi = rope
    qg, kg, vg, z, qa, ka, va, gcb, gct = _mix_proj(h1, p["mix_norm"], p["w_in"], p["w_qk"], p["w_vt"], p["w_ab"],
                                                    p["q_norm"], p["k_norm"], cos, sin_lo, sin_hi, p["alog"],
                                                    p["dtb"], p["conv_w"])
    o_fwd, o_bwd = _gdn_scan(qg, kg, vg, gcb, gct)
    oa = _flash_attn(qa, ka, va, p["attn_out_norm"])
    h2 = _out_proj(o_fwd, o_bwd, z, oa, h1, p["gdn_out_norm"], p["w_out"])
    return _ffn(h2, p["ffn2_norm"], p["ffn2_wg"], p["ffn2_wu"], p["ffn2_wd"], p["final_norm"], final=True)


def _prepare_params(ffn1_norm, ffn1_w_gate, ffn1_w_up, ffn1_w_down, mix_norm, w_in, conv_w, a_log_fwd, a_log_bwd,
                    dt_bias_fwd, dt_bias_bwd, gdn_out_norm, q_norm, k_norm, attn_out_norm, w_out, ffn2_norm,
                    ffn2_w_gate, ffn2_w_up, ffn2_w_down, final_norm):
    w_in0 = w_in[0]
    att0 = GDN_PROJ_W + GATE_COLS
    w_ab = jnp.pad(w_in0[:, GDN_PROJ_W:att0], ((0, 0), (0, LANES - GATE_COLS)))
    return dict(
        ffn1_norm=ffn1_norm[0].reshape(1, -1),
        ffn1_wg=ffn1_w_gate[0].astype(BF16), ffn1_wu=ffn1_w_up[0].astype(BF16), ffn1_wd=ffn1_w_down[0].astype(BF16),
        mix_norm=mix_norm[0].reshape(1, -1),
        w_in=w_in0[:, :GDN_PROJ_W].astype(BF16),
        w_qk=w_in0[:, att0:att0 + ATT_Q_W + ATT_KV_W].astype(BF16),
        w_vt=w_in0[:, att0 + ATT_Q_W + ATT_KV_W:].T.astype(BF16),
        w_ab=w_ab.astype(BF16),
        conv_w=jnp.pad(conv_w[0], ((0, SUBLANES - CONV_K), (0, 0))),
        alog=_pad_lanes(jnp.concatenate([a_log_fwd[0], a_log_bwd[0]])),
        dtb=_pad_lanes(jnp.concatenate([dt_bias_fwd[0], dt_bias_bwd[0]])),
        gdn_out_norm=gdn_out_norm[0].reshape(1, -1),
        q_norm=q_norm[0].reshape(1, -1), k_norm=k_norm[0].reshape(1, -1),
        attn_out_norm=attn_out_norm[0].reshape(1, -1),
        w_out=w_out[0].astype(BF16),
        ffn2_norm=ffn2_norm[0].reshape(1, -1),
        ffn2_wg=ffn2_w_gate[0].astype(BF16), ffn2_wu=ffn2_w_up[0].astype(BF16), ffn2_wd=ffn2_w_down[0].astype(BF16),
        final_norm=final_norm.reshape(1, -1),
    )


def kernel(x_prompt, x_sample, ffn1_norm, ffn1_w_gate, ffn1_w_up, ffn1_w_down, mix_norm, w_in, conv_w, a_log_fwd,
           a_log_bwd, dt_bias_fwd, dt_bias_bwd, gdn_out_norm, q_norm, k_norm, attn_out_norm, w_out, ffn2_norm,
           ffn2_w_gate, ffn2_w_up, ffn2_w_down, final_norm):
    assert x_prompt.shape[0] == 1 and x_sample.shape[0] == 1
    p = _prepare_params(ffn1_norm, ffn1_w_gate, ffn1_w_up, ffn1_w_down, mix_norm, w_in, conv_w, a_log_fwd,
                        a_log_bwd, dt_bias_fwd, dt_bias_bwd, gdn_out_norm, q_norm, k_norm, attn_out_norm, w_out,
                        ffn2_norm, ffn2_w_gate, ffn2_w_up, ffn2_w_down, final_norm)
    rope = _rope_tables(max(x_prompt.shape[1], x_sample.shape[1]))
    y_prompt = _encode(x_prompt[0], p, rope)
    y_sample = _encode(x_sample[0], p, rope)
    return (y_prompt[None], y_sample[None])
```

```python
import functools
import math

import jax
import jax.numpy as jnp
from jax import lax
from jax.experimental import pallas as pl
from jax.experimental.pallas import tpu as pltpu

D_MODEL = 2048
HEAD_DIM = 128
GDN_HEADS = 8
GDN_W = GDN_HEADS * HEAD_DIM
ATT_Q_HEADS = 8
ATT_KV_HEADS = 2
ATT_GROUP = ATT_Q_HEADS // ATT_KV_HEADS
ATT_Q_W = ATT_Q_HEADS * HEAD_DIM
ATT_KV_W = ATT_KV_HEADS * HEAD_DIM
GDN_PROJ_W = 4 * GDN_W
GATE_COLS = 4 * GDN_HEADS
D_FF = 5632
CONV_K = 5
GRID_W = 64
AXIS_DIM = HEAD_DIM // 2
ROPE_THETA = 10000.0
EPS = 1e-6

LANES = 128
SUBLANES = 8
SCAN_CHUNK = 128
PROJ_COL_TILE = 256
CONV_HALO = 16
INV_BASE_LEVELS = 2
INV_BASE = 2 ** INV_BASE_LEVELS
EXP2_SCALE = HEAD_DIM ** -0.5 * math.log2(math.e)
FLASH_ROW_BLOCK = 64
FLASH_SUM_ROWS = 16
FLASH_Q_BLOCK = 512
F32 = jnp.float32
BF16 = jnp.bfloat16

V7X_VMEM_BYTES = 64 * 1024 * 1024
_VMEM_LIMIT = V7X_VMEM_BYTES * 7 // 8


def _cparams(semantics):
    return pltpu.CompilerParams(dimension_semantics=semantics, vmem_limit_bytes=_VMEM_LIMIT)


def _rms(x, g):
    return x * lax.rsqrt(jnp.mean(x * x, axis=-1, keepdims=True) + EPS) * g


def _dot(a, b):
    return jnp.dot(a, b, preferred_element_type=F32)


def _dot_nt(a, b):
    return lax.dot_general(a, b, (((1,), (1,)), ((), ())), preferred_element_type=F32)


def _split3(x):
    hi = x.astype(BF16)
    r = x - hi.astype(F32)
    mid = r.astype(BF16)
    lo = (r - mid.astype(F32)).astype(BF16)
    return hi, mid, lo


def _ffn_kernel(x_ref, g_ref, wg_ref, wu_ref, wd_ref, fg_ref, o_ref, xn_ref, *, final):
    j = pl.program_id(1)
    last = pl.num_programs(1) - 1

    def hidden_tile():
        xn = xn_ref[...]
        gate = _dot(xn, wg_ref[...])
        up = _dot(xn, wu_ref[...])
        act = (gate * jax.nn.sigmoid(gate) * up).astype(BF16)
        return _dot(act, wd_ref[...])

    @pl.when(j == 0)
    def _():
        xn_ref[...] = _rms(x_ref[...], g_ref[...]).astype(BF16)
        o_ref[...] = hidden_tile()

    @pl.when(jnp.logical_and(j > 0, j < last))
    def _():
        o_ref[...] += hidden_tile()

    @pl.when(j == last)
    def _():
        h = x_ref[...] + 0.5 * (o_ref[...] + hidden_tile())
        if final:
            h = _rms(h, fg_ref[...])
        o_ref[...] = h


def _ffn(x, norm_g, wg, wu, wd, final_g, *, final, tm=512, tf=512):
    L = x.shape[0]
    assert L % tm == 0 and D_FF % tf == 0
    grid = (L // tm, D_FF // tf)
    return pl.pallas_call(
        functools.partial(_ffn_kernel, final=final),
        grid=grid,
        in_specs=[
            pl.BlockSpec((tm, D_MODEL), lambda i, j: (i, 0)),
            pl.BlockSpec((1, D_MODEL), lambda i, j: (0, 0)),
            pl.BlockSpec((D_MODEL, tf), lambda i, j: (0, j)),
            pl.BlockSpec((D_MODEL, tf), lambda i, j: (0, j)),
            pl.BlockSpec((tf, D_MODEL), lambda i, j: (j, 0)),
            pl.BlockSpec((1, D_MODEL), lambda i, j: (0, 0)),
        ],
        out_specs=pl.BlockSpec((tm, D_MODEL), lambda i, j: (i, 0)),
        out_shape=jax.ShapeDtypeStruct((L, D_MODEL), F32),
        scratch_shapes=[pltpu.VMEM((tm, D_MODEL), BF16)],
        compiler_params=_cparams(("parallel", "arbitrary")),
        name="ffn",
    )(x, norm_g, wg, wu, wd, final_g)


def _rope(x, cos, sin_lo, sin_hi):
    return (x * cos + pltpu.roll(x, AXIS_DIM // 2, axis=1) * sin_hi
            + pltpu.roll(x, HEAD_DIM - AXIS_DIM // 2, axis=1) * sin_lo)


def _mix_proj_kernel(x_ref, xprev_ref, xnext_ref, g_ref, wg_ref, w_ref, wvt_ref, wab_ref, qn_ref, kn_ref, cos_ref,
                     slo_ref, shi_ref, alog_ref, dtb_ref, cw_ref, qg_ref, kg_ref, vg_ref, z_ref, q_ref, k_ref,
                     vt_ref, gcb_ref, gct_ref, ext_ref, *, tm):
    i = pl.program_id(0)
    pad = CONV_K // 2
    g = g_ref[...]
    xn = _rms(x_ref[...], g).astype(BF16)

    xn_ext = jnp.concatenate([_rms(xprev_ref[...], g).astype(BF16), xn, _rms(xnext_ref[...], g).astype(BF16)], axis=0)
    keep_prev = jnp.where(i == 0, 0.0, 1.0)
    keep_next = jnp.where(i == pl.num_programs(0) - 1, 0.0, 1.0)
    cos = cos_ref[...]
    slo = slo_ref[...]
    shi = shi_ref[...]

    def project_ext(c):
        cs = slice(c, c + PROJ_COL_TILE)
        e = _dot(xn_ext, wg_ref[:, cs])
        ext_ref[0:CONV_HALO, cs] = e[0:CONV_HALO] * keep_prev
        ext_ref[CONV_HALO:CONV_HALO + tm, cs] = e[CONV_HALO:CONV_HALO + tm]
        ext_ref[CONV_HALO + tm:, cs] = e[CONV_HALO + tm:] * keep_next

    def project_z(c):
        z_ref[:, c:c + PROJ_COL_TILE] = _dot(xn, wg_ref[:, 3 * GDN_W + c:3 * GDN_W + c + PROJ_COL_TILE])

    def project_attn_pair(h):
        pair = _dot(xn, w_ref[:, h * HEAD_DIM:(h + 2) * HEAD_DIM])
        for half in range(2):
            head = pair[:, half * HEAD_DIM:(half + 1) * HEAD_DIM]
            n = h + half
            if n < ATT_Q_HEADS:
                q_ref[:, n * HEAD_DIM:(n + 1) * HEAD_DIM] = (
                    _rope(_rms(head, qn_ref[...]), cos, slo, shi) * EXP2_SCALE).astype(BF16)
            else:
                n -= ATT_Q_HEADS
                k_ref[:, n * HEAD_DIM:(n + 1) * HEAD_DIM] = _rope(_rms(head, kn_ref[...]), cos, slo, shi).astype(BF16)

    def project_vt():
        vt_ref[...] = _dot_nt(wvt_ref[...], xn).astype(BF16)

    def conv_slab(s):
        cols = slice(s * HEAD_DIM, (s + 1) * HEAD_DIM)
        y = None
        for t in range(CONV_K):
            term = ext_ref[CONV_HALO - pad + t:CONV_HALO - pad + t + tm, cols] * cw_ref[t:t + 1, cols]
            y = term if y is None else y + term
        y = y * jax.nn.sigmoid(y)
        part, h = divmod(s, GDN_HEADS)
        hc = slice(h * HEAD_DIM, (h + 1) * HEAD_DIM)
        if part == 0:
            qg_ref[:, hc] = y * (lax.rsqrt(jnp.sum(y * y, axis=-1, keepdims=True) + EPS) * (HEAD_DIM ** -0.5))
        elif part == 1:
            kg_ref[:, hc] = y * lax.rsqrt(jnp.sum(y * y, axis=-1, keepdims=True) + EPS)
        else:
            vg_ref[:, hc] = y

    tiles = range(0, GDN_W, PROJ_COL_TILE)
    mxu_q = [functools.partial(project_ext, c) for c in tiles]
    mxu_k = [functools.partial(project_ext, GDN_W + c) for c in tiles]
    mxu_v = [functools.partial(project_ext, 2 * GDN_W + c) for c in tiles]
    mxu_z = [functools.partial(project_z, c) for c in tiles]
    mxu_att = [functools.partial(project_attn_pair, h) for h in range(0, ATT_Q_HEADS + ATT_KV_HEADS, 2)]
    mxu_att.append(project_vt)
    for piece in mxu_q:
        piece()
    n_att = len(mxu_att) // 2
    stages = [(mxu_k + mxu_att[:n_att], range(0, GDN_HEADS)),
              (mxu_v + mxu_att[n_att:] + mxu_z[:2], range(GDN_HEADS, 2 * GDN_HEADS)),
              (mxu_z[2:], range(2 * GDN_HEADS, 3 * GDN_HEADS))]
    for pieces, slabs in stages:
        for n in range(max(len(pieces), len(slabs))):
            if n < len(pieces):
                pieces[n]()
            if n < len(slabs):
                conv_slab(slabs[n])

    ab = _dot(xn, wab_ref[...])
    t = ab + dtb_ref[...]
    softplus = jnp.maximum(t, 0.0) + jnp.log1p(jnp.exp(-jnp.abs(t)))
    log_decay = -jnp.exp(alog_ref[...]) * softplus
    lane = lax.broadcasted_iota(jnp.int32, ab.shape, 1)
    gb = jnp.where(lane < 2 * GDN_HEADS, log_decay, jax.nn.sigmoid(ab))

    r = lax.broadcasted_iota(jnp.int32, (tm, tm), 0)
    c = lax.broadcasted_iota(jnp.int32, (tm, tm), 1)
    same = (r // SCAN_CHUNK) == (c // SCAN_CHUNK)
    m_lo = jnp.where(same & (c <= r), 1.0, 0.0).astype(BF16)
    m_up = jnp.where(same & (c >= r), 1.0, 0.0).astype(BF16)
    parts = _split3(gb)
    gc_f = _dot(m_lo, parts[0]) + _dot(m_lo, parts[1]) + _dot(m_lo, parts[2])
    gc_b = _dot(m_up, parts[0]) + _dot(m_up, parts[1]) + _dot(m_up, parts[2])
    gcb = jnp.where(lane < GDN_HEADS, gc_f, jnp.where(lane < 2 * GDN_HEADS, gc_b, gb))
    gcb_ref[...] = gcb
    for n in range(tm // SCAN_CHUNK):
        gct = gcb[n * SCAN_CHUNK:(n + 1) * SCAN_CHUNK, :].T
        gct_ref[n * 2 * GDN_HEADS:(n + 1) * 2 * GDN_HEADS, :] = gct[0:2 * GDN_HEADS, :]


def _mix_proj(x, norm_g, w_in, w_qk, w_vt, w_ab, q_norm, k_norm, cos, sin_lo, sin_hi, alog, dtb, conv_w8, *, tm=256):
    L = x.shape[0]
    assert L % tm == 0 and tm % SCAN_CHUNK == 0 and tm % CONV_HALO == 0
    n_halo = L // CONV_HALO
    per = tm // CONV_HALO
    row = lambda i: (i, 0)
    fixed = lambda i: (0, 0)
    resident = pl.Buffered(1)
    return pl.pallas_call(
        functools.partial(_mix_proj_kernel, tm=tm),
        grid=(L // tm,),
        in_specs=[
            pl.BlockSpec((tm, D_MODEL), row),
            pl.BlockSpec((CONV_HALO, D_MODEL), lambda i: (jnp.maximum(i * per - 1, 0), 0)),
            pl.BlockSpec((CONV_HALO, D_MODEL), lambda i: (jnp.minimum((i + 1) * per, n_halo - 1), 0)),
            pl.BlockSpec((1, D_MODEL), fixed),
            pl.BlockSpec((D_MODEL, GDN_PROJ_W), fixed, pipeline_mode=resident),
            pl.BlockSpec((D_MODEL, ATT_Q_W + ATT_KV_W), fixed, pipeline_mode=resident),
            pl.BlockSpec((ATT_KV_W, D_MODEL), fixed, pipeline_mode=resident),
            pl.BlockSpec((D_MODEL, LANES), fixed),
            pl.BlockSpec((1, HEAD_DIM), fixed),
            pl.BlockSpec((1, HEAD_DIM), fixed),
            pl.BlockSpec((tm, HEAD_DIM), row),
            pl.BlockSpec((tm, HEAD_DIM), row),
            pl.BlockSpec((tm, HEAD_DIM), row),
            pl.BlockSpec((1, LANES), fixed),
            pl.BlockSpec((1, LANES), fixed),
            pl.BlockSpec((SUBLANES, 3 * GDN_W), fixed),
        ],
        out_specs=[
            pl.BlockSpec((tm, GDN_W), row),
            pl.BlockSpec((tm, GDN_W), row),
            pl.BlockSpec((tm, GDN_W), row),
            pl.BlockSpec((tm, GDN_W), row),
            pl.BlockSpec((tm, ATT_Q_W), row),
            pl.BlockSpec((tm, ATT_KV_W), row),
            pl.BlockSpec((ATT_KV_W, tm), lambda i: (0, i)),
            pl.BlockSpec((tm, LANES), row),
            pl.BlockSpec((tm // SCAN_CHUNK * 2 * GDN_HEADS, SCAN_CHUNK), row),
        ],
        out_shape=[
            jax.ShapeDtypeStruct((L, GDN_W), F32),
            jax.ShapeDtypeStruct((L, GDN_W), F32),
            jax.ShapeDtypeStruct((L, GDN_W), F32),
            jax.ShapeDtypeStruct((L, GDN_W), F32),
            jax.ShapeDtypeStruct((L, ATT_Q_W), BF16),
            jax.ShapeDtypeStruct((L, ATT_KV_W), BF16),
            jax.ShapeDtypeStruct((ATT_KV_W, L), BF16),
            jax.ShapeDtypeStruct((L, LANES), F32),
            jax.ShapeDtypeStruct((L // SCAN_CHUNK * 2 * GDN_HEADS, SCAN_CHUNK), F32),
        ],
        scratch_shapes=[pltpu.VMEM((tm + 2 * CONV_HALO, 3 * GDN_W), F32)],
        compiler_params=_cparams(("parallel",)),
        name="mix_proj",
    )(x, x, x, norm_g, w_in, w_qk, w_vt, w_ab, q_norm, k_norm, cos, sin_lo, sin_hi, alog, dtb, conv_w8)


def _gdn_scan_kernel(qf_ref, kf_ref, vf_ref, gcbf_ref, gctf_ref, qb_ref, kb_ref, vb_ref, gcbb_ref, gctb_ref,
                     of_ref, ob_ref, s_ref):
    C = SCAN_CHUNK

    @pl.when(pl.program_id(0) == 0)
    def _():
        s_ref[...] = jnp.zeros_like(s_ref)

    r = lax.broadcasted_iota(jnp.int32, (C, C), 0)
    c = lax.broadcasted_iota(jnp.int32, (C, C), 1)
    eye = jnp.where(r == c, 1.0, 0.0)

    chains = []
    for fwd, (q_ref, k_ref, v_ref, gcb_ref, gct_ref, o_ref) in (
            (True, (qf_ref, kf_ref, vf_ref, gcbf_ref, gctf_ref, of_ref)),
            (False, (qb_ref, kb_ref, vb_ref, gcbb_ref, gctb_ref, ob_ref))):
        dif = r - c if fwd else c - r
        lane0 = 0 if fwd else GDN_HEADS
        gcb = gcb_ref[...]
        gc = gcb[:, lane0:lane0 + GDN_HEADS]
        beta = gcb[:, 2 * GDN_HEADS + lane0:3 * GDN_HEADS + lane0]
        g_end = gc[C - 1:C, :] if fwd else gc[0:1, :]
        shared = dict(q_ref=q_ref, k_ref=k_ref, v_ref=v_ref, o_ref=o_ref, m_incl=dif >= 0, m_strict=dif > 0,
                      gc=gc, beta=beta, e_gc=jnp.exp(gc), e_rest=jnp.exp(g_end - gc), e_end=jnp.exp(g_end),
                      gct=gct_ref[...])
        for h in range(GDN_HEADS):
            chains.append(dict(shared, cols=slice(h * HEAD_DIM, (h + 1) * HEAD_DIM), col1=slice(h, h + 1),
                               state=lane0 + h))
    heads = range(len(chains))
    cols = [ch["cols"] for ch in chains]
    col1 = [ch["col1"] for ch in chains]
    beta = [ch["beta"] for ch in chains]
    e_gc = [ch["e_gc"] for ch in chains]

    k = [chains[h]["k_ref"][:, cols[h]] for h in heads]
    k16 = [k[h].astype(BF16) for h in heads]
    kb = [k[h] * beta[h][:, col1[h]] for h in heads]
    q16 = [chains[h]["q_ref"][:, cols[h]].astype(BF16) for h in heads]
    kq = [_dot_nt(jnp.concatenate([kb[h].astype(BF16), q16[h]], axis=0), k16[h]) for h in heads]
    decay = []
    for ch in chains:
        diff = ch["gc"][:, ch["col1"]] - ch["gct"][ch["col1"], :]
        decay.append(jnp.where(ch["m_incl"], jnp.exp(jnp.where(ch["m_incl"], diff, 0.0)), 0.0))
    a = [jnp.where(chains[h]["m_strict"], kq[h][0:C] * decay[h], 0.0) for h in heads]
    attn_qk16 = [(kq[h][C:2 * C] * decay[h]).astype(BF16) for h in heads]

    rb = r >> INV_BASE_LEVELS
    cb = c >> INV_BASE_LEVELS
    a_d = [jnp.where(rb == cb, a[h], 0.0) for h in heads]
    inv = [eye - a_d[h] for h in heads]
    a_d16 = [a_d[h].astype(BF16) for h in heads]
    a_pow16 = [_dot(a_d16[h], a_d16[h]).astype(BF16) for h in heads]
    for level in range(INV_BASE_LEVELS - 1):
        if level < INV_BASE_LEVELS - 2:
            both = [_dot(jnp.concatenate([inv[h].astype(BF16), a_pow16[h]], axis=0), a_pow16[h]) for h in heads]
            inv = [inv[h] + both[h][0:C] for h in heads]
            a_pow16 = [both[h][C:2 * C].astype(BF16) for h in heads]
        else:
            inv = [inv[h] + _dot(inv[h].astype(BF16), a_pow16[h]) for h in heads]
    b = INV_BASE
    while b < C:
        off = ((rb >> 1) == (cb >> 1)) & (rb != cb)
        a_off16 = [jnp.where(off, a[h], 0.0).astype(BF16) for h in heads]
        inv16 = [inv[h].astype(BF16) for h in heads]
        left = [_dot(inv16[h], a_off16[h]).astype(BF16) for h in heads]
        inv = [inv[h] - _dot(left[h], inv16[h]) for h in heads]
        rb = rb >> 1
        cb = cb >> 1
        b *= 2

    rhs16 = [jnp.concatenate([(chains[h]["v_ref"][:, cols[h]] * beta[h][:, col1[h]]).astype(BF16),
                              (kb[h] * e_gc[h][:, col1[h]]).astype(BF16)], axis=1) for h in heads]
    uw = [_dot(inv[h].astype(BF16), rhs16[h]) for h in heads]
    q_dec16 = [(chains[h]["q_ref"][:, cols[h]] * e_gc[h][:, col1[h]]).astype(BF16) for h in heads]
    k_dec_t16 = [(k[h] * chains[h]["e_rest"][:, col1[h]]).T.astype(BF16) for h in heads]

    s = [s_ref[ch["state"]] for ch in chains]
    s16 = [s[h].astype(BF16) for h in heads]
    ws = [_dot(jnp.concatenate([uw[h][:, HEAD_DIM:].astype(BF16), q_dec16[h]], axis=0), s16[h]) for h in heads]
    v_new16 = [(uw[h][:, 0:HEAD_DIM] - ws[h][0:C]).astype(BF16) for h in heads]
    for h, ch in enumerate(chains):
        ch["o_ref"][:, cols[h]] = ws[h][C:2 * C] + _dot(attn_qk16[h], v_new16[h])
    for h, ch in enumerate(chains):
        s_ref[ch["state"]] = s[h] * ch["e_end"][:, col1[h]] + _dot(k_dec_t16[h], v_new16[h])


def _gdn_scan(q, k, v, gcb, gct):
    L = q.shape[0]
    assert L % SCAN_CHUNK == 0
    n = L // SCAN_CHUNK
    fwd = lambda i: (i, 0)
    bwd = lambda i: (n - 1 - i, 0)
    tile_f = pl.BlockSpec((SCAN_CHUNK, GDN_W), fwd)
    tile_b = pl.BlockSpec((SCAN_CHUNK, GDN_W), bwd)
    return pl.pallas_call(
        _gdn_scan_kernel,
        grid=(n,),
        in_specs=[
            tile_f, tile_f, tile_f,
            pl.BlockSpec((SCAN_CHUNK, LANES), fwd),
            pl.BlockSpec((GDN_HEADS, SCAN_CHUNK), lambda i: (2 * i, 0)),
            tile_b, tile_b, tile_b,
            pl.BlockSpec((SCAN_CHUNK, LANES), bwd),
            pl.BlockSpec((GDN_HEADS, SCAN_CHUNK), lambda i: (2 * (n - 1 - i) + 1, 0)),
        ],
        out_specs=[tile_f, tile_b],
        out_shape=[jax.ShapeDtypeStruct((L, GDN_W), F32), jax.ShapeDtypeStruct((L, GDN_W), F32)],
        scratch_shapes=[pltpu.VMEM((2 * GDN_HEADS, HEAD_DIM, HEAD_DIM), F32)],
        compiler_params=_cparams(("arbitrary",)),
        name="gdn_scan",
    )(q, k, v, gcb, gct, q, k, v, gcb, gct)


def _flash_kernel(q_ref, k_ref, vt_ref, on_ref, o_ref, m_ref, acc_ref, st_ref, p_ref, *, tk):
    ki = pl.program_id(1)

    @pl.when(ki == 0)
    def _():
        m_ref[...] = jnp.full_like(m_ref, -jnp.inf)
        acc_ref[...] = jnp.zeros_like(acc_ref)

    ones = jnp.ones((FLASH_SUM_ROWS, tk), BF16)
    k = [k_ref[:, g * HEAD_DIM:(g + 1) * HEAD_DIM] for g in range(ATT_KV_HEADS)]
    vt1 = [jnp.concatenate([vt_ref[g * HEAD_DIM:(g + 1) * HEAD_DIM, :], ones], axis=0) for g in range(ATT_KV_HEADS)]
    blocks = [slice(r, r + FLASH_ROW_BLOCK) for r in range(0, tk, FLASH_ROW_BLOCK)]

    tq = q_ref.shape[0]
    units = [(h, slice(c, c + FLASH_Q_BLOCK)) for h in range(ATT_Q_HEADS) for c in range(0, tq, FLASH_Q_BLOCK)]

    def scores(h, qs):
        st_ref[h, :, qs] = _dot_nt(k[h // ATT_GROUP], q_ref[qs, h * HEAD_DIM:(h + 1) * HEAD_DIM])

    def softmax(h, qs):
        part = None
        for rows in blocks:
            x = st_ref[h, rows, qs]
            while x.shape[0] > SUBLANES:
                half = x.shape[0] // 2
                x = jnp.maximum(x[:half], x[half:])
            part = x if part is None else jnp.maximum(part, x)
        m_prev = m_ref[h, :, qs]
        m_new = jnp.maximum(m_prev, jnp.max(part, axis=0, keepdims=True))
        m_ref[h, :, qs] = m_new
        for rows in blocks:
            p_ref[h, rows, qs] = jnp.exp2(st_ref[h, rows, qs] - m_new).astype(BF16)
        return jnp.exp2(m_prev - m_new)

    def accumulate(h, qs, alpha):
        acc_ref[h, :, qs] = alpha * acc_ref[h, :, qs] + _dot(vt1[h // ATT_GROUP], p_ref[h, :, qs])

    scores(*units[0])
    pending = None
    for u, unit in enumerate(units):
        if u + 1 < len(units):
            scores(*units[u + 1])
        alpha = softmax(*unit)
        if pending is not None:
            accumulate(*pending)
        pending = (*unit, alpha)
    accumulate(*pending)

    @pl.when(ki == pl.num_programs(1) - 1)
    def _():
        for h in range(ATT_Q_HEADS):
            cols = slice(h * HEAD_DIM, (h + 1) * HEAD_DIM)
            o = (acc_ref[h, 0:HEAD_DIM, :] / acc_ref[h, HEAD_DIM:HEAD_DIM + 1, :]).T
            o_ref[:, cols] = _rms(o, on_ref[...]).astype(BF16)


def _flash_attn(q, k, vt, out_norm, *, tq=512, tk=1024):
    L = q.shape[0]
    assert L % tq == 0 and L % tk == 0 and tq % FLASH_Q_BLOCK == 0 and tk % FLASH_ROW_BLOCK == 0
    return pl.pallas_call(
        functools.partial(_flash_kernel, tk=tk),
        grid=(L // tq, L // tk),
        in_specs=[
            pl.BlockSpec((tq, ATT_Q_W), lambda i, j: (i, 0)),
            pl.BlockSpec((tk, ATT_KV_W), lambda i, j: (j, 0)),
            pl.BlockSpec((ATT_KV_W, tk), lambda i, j: (0, j)),
            pl.BlockSpec((1, HEAD_DIM), lambda i, j: (0, 0)),
        ],
        out_specs=pl.BlockSpec((tq, ATT_Q_W), lambda i, j: (i, 0)),
        out_shape=jax.ShapeDtypeStruct((L, ATT_Q_W), BF16),
        scratch_shapes=[
            pltpu.VMEM((ATT_Q_HEADS, 1, tq), F32),
            pltpu.VMEM((ATT_Q_HEADS, HEAD_DIM + FLASH_SUM_ROWS, tq), F32),
            pltpu.VMEM((ATT_Q_HEADS, tk, tq), F32),
            pltpu.VMEM((ATT_Q_HEADS, tk, tq), BF16),
        ],
        compiler_params=_cparams(("parallel", "arbitrary")),
        name="flash_attn",
    )(q, k, vt, out_norm)


def _out_proj_kernel(of_ref, ob_ref, z_ref, oa_ref, h_ref, gn_ref, w_ref, o_ref, mix_ref):
    att = _dot(oa_ref[...], w_ref[GDN_W:, :])
    for h in range(GDN_HEADS):
        cols = slice(h * HEAD_DIM, (h + 1) * HEAD_DIM)
        o = _rms(of_ref[:, cols] + ob_ref[:, cols], gn_ref[...])
        z = z_ref[:, cols]
        mix_ref[:, cols] = (o * (z * jax.nn.sigmoid(z))).astype(BF16)
    o_ref[...] = h_ref[...] + (att + _dot(mix_ref[...], w_ref[0:GDN_W, :]))


def _out_proj(o_fwd, o_bwd, z, oa, h1, gdn_norm, w_out, *, tm=512):
    L = h1.shape[0]
    assert L % tm == 0
    row = lambda i: (i, 0)
    return pl.pallas_call(
        _out_proj_kernel,
        grid=(L // tm,),
        in_specs=[
            pl.BlockSpec((tm, GDN_W), row),
            pl.BlockSpec((tm, GDN_W), row),
            pl.BlockSpec((tm, GDN_W), row),
            pl.BlockSpec((tm, ATT_Q_W), row),
            pl.BlockSpec((tm, D_MODEL), row),
            pl.BlockSpec((1, HEAD_DIM), lambda i: (0, 0)),
            pl.BlockSpec((GDN_W + ATT_Q_W, D_MODEL), lambda i: (0, 0)),
        ],
        out_specs=pl.BlockSpec((tm, D_MODEL), row),
        out_shape=jax.ShapeDtypeStruct((L, D_MODEL), F32),
        scratch_shapes=[pltpu.VMEM((tm, GDN_W), BF16)],
        compiler_params=_cparams(("parallel",)),
        name="out_proj",
    )(o_fwd, o_bwd, z, oa, h1, gdn_norm, w_out)


def _rope_tables(L):
    assert L % GRID_W == 0
    rows = L // GRID_W
    freqs = ROPE_THETA ** (-jnp.arange(0, AXIS_DIM, 2, dtype=F32) / AXIS_DIM)
    ang_r = jnp.arange(rows, dtype=F32)[:, None] * freqs[None, :]
    ang_c = jnp.arange(GRID_W, dtype=F32)[:, None] * freqs[None, :]
    per_row = lambda tab: jnp.repeat(tab, GRID_W, axis=0)
    per_col = lambda tab: jnp.tile(tab, (rows, 1))
    cos_r, sin_r = per_row(jnp.cos(ang_r)), per_row(jnp.sin(ang_r))
    cos_c, sin_c = per_col(jnp.cos(ang_c)), per_col(jnp.sin(ang_c))
    zero = jnp.zeros_like(cos_r)
    cos = jnp.concatenate([cos_r, cos_r, cos_c, cos_c], axis=-1)
    sin_lo = jnp.concatenate([-sin_r, zero, -sin_c, zero], axis=-1)
    sin_hi = jnp.concatenate([zero, sin_r, zero, sin_c], axis=-1)
    return cos, sin_lo, sin_hi


def _pad_lanes(x):
    x = x.reshape(1, -1)
    return jnp.pad(x, ((0, 0), (0, LANES - x.shape[1])))


def _encode(x, p, rope):
    h1 = _ffn(x, p["ffn1_norm"], p["ffn1_wg"], p["ffn1_wu"], p["ffn1_wd"], p["final_norm"], final=False)
    cos, sin_lo, sin_hi = rope
    qg, kg, vg, z, qa, ka, va, gcb, gct = _mix_proj(h1, p["mix_norm"], p["w_in"], p["w_qk"], p["w_vt"], p["w_ab"],
                                                    p["q_norm"], p["k_norm"], cos, sin_lo, sin_hi, p["alog"],
                                                    p["dtb"], p["conv_w"])
    o_fwd, o_bwd = _gdn_scan(qg, kg, vg, gcb, gct)
    oa = _flash_attn(qa, ka, va, p["attn_out_norm"])
    h2 = _out_proj(o_fwd, o_bwd, z, oa, h1, p["gdn_out_norm"], p["w_out"])
    return _ffn(h2, p["ffn2_norm"], p["ffn2_wg"], p["ffn2_wu"], p["ffn2_wd"], p["final_norm"], final=True)


def _prepare_params(ffn1_norm, ffn1_w_gate, ffn1_w_up, ffn1_w_down, mix_norm, w_in, conv_w, a_log_fwd, a_log_bwd,
                    dt_bias_fwd, dt_bias_bwd, gdn_out_norm, q_norm, k_norm, attn_out_norm, w_out, ffn2_norm,
                    ffn2_w_gate, ffn2_w_up, ffn2_w_down, final_norm):
    w_in0 = w_in[0]
    att0 = GDN_PROJ_W + GATE_COLS
    w_ab = jnp.pad(w_in0[:, GDN_PROJ_W:att0], ((0, 0), (0, LANES - GATE_COLS)))
    return dict(
        ffn1_norm=ffn1_norm[0].reshape(1, -1),
        ffn1_wg=ffn1_w_gate[0].astype(BF16), ffn1_wu=ffn1_w_up[0].astype(BF16), ffn1_wd=ffn1_w_down[0].astype(BF16),
        mix_norm=mix_norm[0].reshape(1, -1),
        w_in=w_in0[:, :GDN_PROJ_W].astype(BF16),
        w_qk=w_in0[:, att0:att0 + ATT_Q_W + ATT_KV_W].astype(BF16),
        w_vt=w_in0[:, att0 + ATT_Q_W + ATT_KV_W:].T.astype(BF16),
        w_ab=w_ab.astype(BF16),
        conv_w=jnp.pad(conv_w[0], ((0, SUBLANES - CONV_K), (0, 0))),
        alog=_pad_lanes(jnp.concatenate([a_log_fwd[0], a_log_bwd[0]])),
        dtb=_pad_lanes(jnp.concatenate([dt_bias_fwd[0], dt_bias_bwd[0]])),
        gdn_out_norm=gdn_out_norm[0].reshape(1, -1),
        q_norm=q_norm[0].reshape(1, -1), k_norm=k_norm[0].reshape(1, -1),
        attn_out_norm=attn_out_norm[0].reshape(1, -1),
        w_out=w_out[0].astype(BF16),
        ffn2_norm=ffn2_norm[0].reshape(1, -1),
        ffn2_wg=ffn2_w_gate[0].astype(BF16), ffn2_wu=ffn2_w_up[0].astype(BF16), ffn2_wd=ffn2_w_down[0].astype(BF16),
        final_norm=final_norm.reshape(1, -1),
    )


def kernel(x_prompt, x_sample, ffn1_norm, ffn1_w_gate, ffn1_w_up, ffn1_w_down, mix_norm, w_in, conv_w, a_log_fwd,
           a_log_bwd, dt_bias_fwd, dt_bias_bwd, gdn_out_norm, q_norm, k_norm, attn_out_norm, w_out, ffn2_norm,
           ffn2_w_gate, ffn2_w_up, ffn2_w_down, final_norm):
    assert x_prompt.shape[0] == 1 and x_sample.shape[0] == 1
    p = _prepare_params(ffn1_norm, ffn1_w_gate, ffn1_w_up, ffn1_w_down, mix_norm, w_in, conv_w, a_log_fwd,
                        a_log_bwd, dt_bias_fwd, dt_bias_bwd, gdn_out_norm, q_norm, k_norm, attn_out_norm, w_out,
                        ffn2_norm, ffn2_w_gate, ffn2_w_up, ffn2_w_down, final_norm)
    rope = _rope_tables(max(x_prompt.shape[1], x_sample.shape[1]))
    y_prompt = _encode(x_prompt[0], p, rope)
    y_sample = _encode(x_sample[0], p, rope)
    return (y_prompt[None], y_sample[None])
```

```python
import functools
import math

import jax
import jax.numpy as jnp
from jax import lax
from jax.experimental import pallas as pl
from jax.experimental.pallas import tpu as pltpu

D_MODEL = 2048
HEAD_DIM = 128
GDN_HEADS = 8
GDN_W = GDN_HEADS * HEAD_DIM
ATT_Q_HEADS = 8
ATT_KV_HEADS = 2
ATT_GROUP = ATT_Q_HEADS // ATT_KV_HEADS
ATT_Q_W = ATT_Q_HEADS * HEAD_DIM
ATT_KV_W = ATT_KV_HEADS * HEAD_DIM
GDN_PROJ_W = 4 * GDN_W
GATE_COLS = 4 * GDN_HEADS
D_FF = 5632
CONV_K = 5
GRID_W = 64
AXIS_DIM = HEAD_DIM // 2
ROPE_THETA = 10000.0
EPS = 1e-6

LANES = 128
SUBLANES = 8
FFN_HIDDEN_TILE = 512
SCAN_CHUNK = 128
PROJ_COL_TILE = 256
CONV_HALO = 16
INV_BASE_LEVELS = 2
INV_BASE = 2 ** INV_BASE_LEVELS
EXP2_SCALE = HEAD_DIM ** -0.5 * math.log2(math.e)
FLASH_ROW_BLOCK = 64
FLASH_SUM_ROWS = 16
FLASH_Q_BLOCK = 512
F32 = jnp.float32
BF16 = jnp.bfloat16

V7X_VMEM_BYTES = 64 * 1024 * 1024
_VMEM_LIMIT = V7X_VMEM_BYTES * 7 // 8


def _cparams(semantics):
    return pltpu.CompilerParams(dimension_semantics=semantics, vmem_limit_bytes=_VMEM_LIMIT)


def _rms(x, g):
    return x * lax.rsqrt(jnp.mean(x * x, axis=-1, keepdims=True) + EPS) * g


def _dot(a, b):
    return jnp.dot(a, b, preferred_element_type=F32)


def _dot_nt(a, b):
    return lax.dot_general(a, b, (((1,), (1,)), ((), ())), preferred_element_type=F32)


def _split3(x):
    hi = x.astype(BF16)
    r = x - hi.astype(F32)
    mid = r.astype(BF16)
    lo = (r - mid.astype(F32)).astype(BF16)
    return hi, mid, lo


def _ffn_kernel(x_ref, g_ref, wgu_ref, wd_ref, fg_ref, o_ref, xn_ref, *, final):
    j = pl.program_id(1)
    last = pl.num_programs(1) - 1
    tf = wd_ref.shape[0]

    def hidden_tile():
        gu = _dot(xn_ref[...], wgu_ref[...])
        gate = gu[:, 0:tf]
        up = gu[:, tf:2 * tf]
        act = (gate * jax.nn.sigmoid(gate) * up).astype(BF16)
        return _dot(act, wd_ref[...])

    @pl.when(j == 0)
    def _():
        xn_ref[...] = _rms(x_ref[...], g_ref[...]).astype(BF16)
        o_ref[...] = hidden_tile()

    @pl.when(jnp.logical_and(j > 0, j < last))
    def _():
        o_ref[...] += hidden_tile()

    @pl.when(j == last)
    def _():
        h = x_ref[...] + 0.5 * (o_ref[...] + hidden_tile())
        if final:
            h = _rms(h, fg_ref[...])
        o_ref[...] = h


def _interleave_gate_up(w_gate, w_up, tf):
    d, f = w_gate.shape
    tiles = jnp.stack([w_gate.reshape(d, f // tf, tf), w_up.reshape(d, f // tf, tf)], axis=2)
    return tiles.reshape(d, 2 * f)


def _ffn(x, norm_g, wgu, wd, final_g, *, final, tm=512, tf=FFN_HIDDEN_TILE):
    L = x.shape[0]
    assert L % tm == 0 and D_FF % tf == 0 and D_FF // tf >= 2
    grid = (L // tm, D_FF // tf)
    return pl.pallas_call(
        functools.partial(_ffn_kernel, final=final),
        grid=grid,
        in_specs=[
            pl.BlockSpec((tm, D_MODEL), lambda i, j: (i, 0)),
            pl.BlockSpec((1, D_MODEL), lambda i, j: (0, 0)),
            pl.BlockSpec((D_MODEL, 2 * tf), lambda i, j: (0, j)),
            pl.BlockSpec((tf, D_MODEL), lambda i, j: (j, 0)),
            pl.BlockSpec((1, D_MODEL), lambda i, j: (0, 0)),
        ],
        out_specs=pl.BlockSpec((tm, D_MODEL), lambda i, j: (i, 0)),
        out_shape=jax.ShapeDtypeStruct((L, D_MODEL), F32),
        scratch_shapes=[pltpu.VMEM((tm, D_MODEL), BF16)],
        compiler_params=_cparams(("parallel", "arbitrary")),
        name="ffn",
    )(x, norm_g, wgu, wd, final_g)


def _rope(x, cos, sin_lo, sin_hi):
    return (x * cos + pltpu.roll(x, AXIS_DIM // 2, axis=1) * sin_hi
            + pltpu.roll(x, HEAD_DIM - AXIS_DIM // 2, axis=1) * sin_lo)


def _mix_proj_kernel(x_ref, xprev_ref, xnext_ref, g_ref, wg_ref, w_ref, wvt_ref, wab_ref, qn_ref, kn_ref, cos_ref,
                     slo_ref, shi_ref, alog_ref, dtb_ref, cw_ref, qg_ref, kg_ref, vg_ref, z_ref, q_ref, k_ref,
                     vt_ref, gcb_ref, gct_ref, ext_ref, *, tm):
    i = pl.program_id(0)
    pad = CONV_K // 2
    g = g_ref[...]
    xn = _rms(x_ref[...], g).astype(BF16)

    xn_ext = jnp.concatenate([_rms(xprev_ref[...], g).astype(BF16), xn, _rms(xnext_ref[...], g).astype(BF16)], axis=0)
    keep_prev = jnp.where(i == 0, 0.0, 1.0)
    keep_next = jnp.where(i == pl.num_programs(0) - 1, 0.0, 1.0)
    cos = cos_ref[...]
    slo = slo_ref[...]
    shi = shi_ref[...]

    def project_ext(c):
        cs = slice(c, c + PROJ_COL_TILE)
        e = _dot(xn_ext, wg_ref[:, cs])
        ext_ref[0:CONV_HALO, cs] = e[0:CONV_HALO] * keep_prev
        ext_ref[CONV_HALO:CONV_HALO + tm, cs] = e[CONV_HALO:CONV_HALO + tm]
        ext_ref[CONV_HALO + tm:, cs] = e[CONV_HALO + tm:] * keep_next

    def project_z(c):
        z_ref[:, c:c + PROJ_COL_TILE] = _dot(xn, wg_ref[:, 3 * GDN_W + c:3 * GDN_W + c + PROJ_COL_TILE])

    def project_attn_pair(h):
        pair = _dot(xn, w_ref[:, h * HEAD_DIM:(h + 2) * HEAD_DIM])
        for half in range(2):
            head = pair[:, half * HEAD_DIM:(half + 1) * HEAD_DIM]
            n = h + half
            if n < ATT_Q_HEADS:
                q_ref[:, n * HEAD_DIM:(n + 1) * HEAD_DIM] = (
                    _rope(_rms(head, qn_ref[...]), cos, slo, shi) * EXP2_SCALE).astype(BF16)
            else:
                n -= ATT_Q_HEADS
                k_ref[:, n * HEAD_DIM:(n + 1) * HEAD_DIM] = _rope(_rms(head, kn_ref[...]), cos, slo, shi).astype(BF16)

    def project_vt():
        vt_ref[...] = _dot_nt(wvt_ref[...], xn).astype(BF16)

    def conv_slab(s):
        cols = slice(s * HEAD_DIM, (s + 1) * HEAD_DIM)
        y = None
        for t in range(CONV_K):
            term = ext_ref[CONV_HALO - pad + t:CONV_HALO - pad + t + tm, cols] * cw_ref[t:t + 1, cols]
            y = term if y is None else y + term
        y = y * jax.nn.sigmoid(y)
        part, h = divmod(s, GDN_HEADS)
        hc = slice(h * HEAD_DIM, (h + 1) * HEAD_DIM)
        if part == 0:
            qg_ref[:, hc] = y * (lax.rsqrt(jnp.sum(y * y, axis=-1, keepdims=True) + EPS) * (HEAD_DIM ** -0.5))
        elif part == 1:
            kg_ref[:, hc] = y * lax.rsqrt(jnp.sum(y * y, axis=-1, keepdims=True) + EPS)
        else:
            vg_ref[:, hc] = y

    tiles = range(0, GDN_W, PROJ_COL_TILE)
    mxu_q = [functools.partial(project_ext, c) for c in tiles]
    mxu_k = [functools.partial(project_ext, GDN_W + c) for c in tiles]
    mxu_v = [functools.partial(project_ext, 2 * GDN_W + c) for c in tiles]
    mxu_z = [functools.partial(project_z, c) for c in tiles]
    mxu_att = [functools.partial(project_attn_pair, h) for h in range(0, ATT_Q_HEADS + ATT_KV_HEADS, 2)]
    mxu_att.append(project_vt)
    for piece in mxu_q:
        piece()
    n_att = len(mxu_att) // 2
    stages = [(mxu_k + mxu_att[:n_att], range(0, GDN_HEADS)),
              (mxu_v + mxu_att[n_att:] + mxu_z[:2], range(GDN_HEADS, 2 * GDN_HEADS)),
              (mxu_z[2:], range(2 * GDN_HEADS, 3 * GDN_HEADS))]
    for pieces, slabs in stages:
        for n in range(max(len(pieces), len(slabs))):
            if n < len(pieces):
                pieces[n]()
            if n < len(slabs):
                conv_slab(slabs[n])

    ab = _dot(xn, wab_ref[...])
    t = ab + dtb_ref[...]
    softplus = jnp.maximum(t, 0.0) + jnp.log1p(jnp.exp(-jnp.abs(t)))
    log_decay = -jnp.exp(alog_ref[...]) * softplus
    lane = lax.broadcasted_iota(jnp.int32, ab.shape, 1)
    gb = jnp.where(lane < 2 * GDN_HEADS, log_decay, jax.nn.sigmoid(ab))

    r = lax.broadcasted_iota(jnp.int32, (tm, tm), 0)
    c = lax.broadcasted_iota(jnp.int32, (tm, tm), 1)
    same = (r // SCAN_CHUNK) == (c // SCAN_CHUNK)
    m_lo = jnp.where(same & (c <= r), 1.0, 0.0).astype(BF16)
    m_up = jnp.where(same & (c >= r), 1.0, 0.0).astype(BF16)
    parts = _split3(gb)
    gc_f = _dot(m_lo, parts[0]) + _dot(m_lo, parts[1]) + _dot(m_lo, parts[2])
    gc_b = _dot(m_up, parts[0]) + _dot(m_up, parts[1]) + _dot(m_up, parts[2])
    gcb = jnp.where(lane < GDN_HEADS, gc_f, jnp.where(lane < 2 * GDN_HEADS, gc_b, gb))
    gcb_ref[...] = gcb
    for n in range(tm // SCAN_CHUNK):
        gct = gcb[n * SCAN_CHUNK:(n + 1) * SCAN_CHUNK, :].T
        gct_ref[n * 2 * GDN_HEADS:(n + 1) * 2 * GDN_HEADS, :] = gct[0:2 * GDN_HEADS, :]


def _mix_proj(x, norm_g, w_in, w_qk, w_vt, w_ab, q_norm, k_norm, cos, sin_lo, sin_hi, alog, dtb, conv_w8, *, tm=256):
    L = x.shape[0]
    assert L % tm == 0 and tm % SCAN_CHUNK == 0 and tm % CONV_HALO == 0
    n_halo = L // CONV_HALO
    per = tm // CONV_HALO
    row = lambda i: (i, 0)
    fixed = lambda i: (0, 0)
    resident = pl.Buffered(1)
    return pl.pallas_call(
        functools.partial(_mix_proj_kernel, tm=tm),
        grid=(L // tm,),
        in_specs=[
            pl.BlockSpec((tm, D_MODEL), row),
            pl.BlockSpec((CONV_HALO, D_MODEL), lambda i: (jnp.maximum(i * per - 1, 0), 0)),
            pl.BlockSpec((CONV_HALO, D_MODEL), lambda i: (jnp.minimum((i + 1) * per, n_halo - 1), 0)),
            pl.BlockSpec((1, D_MODEL), fixed),
            pl.BlockSpec((D_MODEL, GDN_PROJ_W), fixed, pipeline_mode=resident),
            pl.BlockSpec((D_MODEL, ATT_Q_W + ATT_KV_W), fixed, pipeline_mode=resident),
            pl.BlockSpec((ATT_KV_W, D_MODEL), fixed, pipeline_mode=resident),
            pl.BlockSpec((D_MODEL, LANES), fixed),
            pl.BlockSpec((1, HEAD_DIM), fixed),
            pl.BlockSpec((1, HEAD_DIM), fixed),
            pl.BlockSpec((tm, HEAD_DIM), row),
            pl.BlockSpec((tm, HEAD_DIM), row),
            pl.BlockSpec((tm, HEAD_DIM), row),
            pl.BlockSpec((1, LANES), fixed),
            pl.BlockSpec((1, LANES), fixed),
            pl.BlockSpec((SUBLANES, 3 * GDN_W), fixed),
        ],
        out_specs=[
            pl.BlockSpec((tm, GDN_W), row),
            pl.BlockSpec((tm, GDN_W), row),
            pl.BlockSpec((tm, GDN_W), row),
            pl.BlockSpec((tm, GDN_W), row),
            pl.BlockSpec((tm, ATT_Q_W), row),
            pl.BlockSpec((tm, ATT_KV_W), row),
            pl.BlockSpec((ATT_KV_W, tm), lambda i: (0, i)),
            pl.BlockSpec((tm, LANES), row),
            pl.BlockSpec((tm // SCAN_CHUNK * 2 * GDN_HEADS, SCAN_CHUNK), row),
        ],
        out_shape=[
            jax.ShapeDtypeStruct((L, GDN_W), F32),
            jax.ShapeDtypeStruct((L, GDN_W), F32),
            jax.ShapeDtypeStruct((L, GDN_W), F32),
            jax.ShapeDtypeStruct((L, GDN_W), F32),
            jax.ShapeDtypeStruct((L, ATT_Q_W), BF16),
            jax.ShapeDtypeStruct((L, ATT_KV_W), BF16),
            jax.ShapeDtypeStruct((ATT_KV_W, L), BF16),
            jax.ShapeDtypeStruct((L, LANES), F32),
            jax.ShapeDtypeStruct((L // SCAN_CHUNK * 2 * GDN_HEADS, SCAN_CHUNK), F32),
        ],
        scratch_shapes=[pltpu.VMEM((tm + 2 * CONV_HALO, 3 * GDN_W), F32)],
        compiler_params=_cparams(("parallel",)),
        name="mix_proj",
    )(x, x, x, norm_g, w_in, w_qk, w_vt, w_ab, q_norm, k_norm, cos, sin_lo, sin_hi, alog, dtb, conv_w8)


def _gdn_scan_kernel(qf_ref, kf_ref, vf_ref, gcbf_ref, gctf_ref, qb_ref, kb_ref, vb_ref, gcbb_ref, gctb_ref,
                     of_ref, ob_ref, s_ref):
    C = SCAN_CHUNK

    @pl.when(pl.program_id(0) == 0)
    def _():
        s_ref[...] = jnp.zeros_like(s_ref)

    r = lax.broadcasted_iota(jnp.int32, (C, C), 0)
    c = lax.broadcasted_iota(jnp.int32, (C, C), 1)
    eye = jnp.where(r == c, 1.0, 0.0)

    chains = []
    for fwd, (q_ref, k_ref, v_ref, gcb_ref, gct_ref, o_ref) in (
            (True, (qf_ref, kf_ref, vf_ref, gcbf_ref, gctf_ref, of_ref)),
            (False, (qb_ref, kb_ref, vb_ref, gcbb_ref, gctb_ref, ob_ref))):
        dif = r - c if fwd else c - r
        lane0 = 0 if fwd else GDN_HEADS
        gcb = gcb_ref[...]
        gc = gcb[:, lane0:lane0 + GDN_HEADS]
        beta = gcb[:, 2 * GDN_HEADS + lane0:3 * GDN_HEADS + lane0]
        g_end = gc[C - 1:C, :] if fwd else gc[0:1, :]
        shared = dict(q_ref=q_ref, k_ref=k_ref, v_ref=v_ref, o_ref=o_ref, m_incl=dif >= 0, m_strict=dif > 0,
                      gc=gc, beta=beta, e_gc=jnp.exp(gc), e_rest=jnp.exp(g_end - gc), e_end=jnp.exp(g_end),
                      gct=gct_ref[...])
        for h in range(GDN_HEADS):
            chains.append(dict(shared, cols=slice(h * HEAD_DIM, (h + 1) * HEAD_DIM), col1=slice(h, h + 1),
                               state=lane0 + h))
    heads = range(len(chains))
    cols = [ch["cols"] for ch in chains]
    col1 = [ch["col1"] for ch in chains]
    beta = [ch["beta"] for ch in chains]
    e_gc = [ch["e_gc"] for ch in chains]

    k = [chains[h]["k_ref"][:, cols[h]] for h in heads]
    k16 = [k[h].astype(BF16) for h in heads]
    kb = [k[h] * beta[h][:, col1[h]] for h in heads]
    q16 = [chains[h]["q_ref"][:, cols[h]].astype(BF16) for h in heads]
    kq = [_dot_nt(jnp.concatenate([kb[h].astype(BF16), q16[h]], axis=0), k16[h]) for h in heads]
    decay = []
    for ch in chains:
        diff = ch["gc"][:, ch["col1"]] - ch["gct"][ch["col1"], :]
        decay.append(jnp.where(ch["m_incl"], jnp.exp(jnp.where(ch["m_incl"], diff, 0.0)), 0.0))
    a = [jnp.where(chains[h]["m_strict"], kq[h][0:C] * decay[h], 0.0) for h in heads]
    attn_qk16 = [(kq[h][C:2 * C] * decay[h]).astype(BF16) for h in heads]

    rb = r >> INV_BASE_LEVELS
    cb = c >> INV_BASE_LEVELS
    a_d = [jnp.where(rb == cb, a[h], 0.0) for h in heads]
    inv = [eye - a_d[h] for h in heads]
    a_d16 = [a_d[h].astype(BF16) for h in heads]
    a_pow16 = [_dot(a_d16[h], a_d16[h]).astype(BF16) for h in heads]
    for level in range(INV_BASE_LEVELS - 1):
        if level < INV_BASE_LEVELS - 2:
            both = [_dot(jnp.concatenate([inv[h].astype(BF16), a_pow16[h]], axis=0), a_pow16[h]) for h in heads]
            inv = [inv[h] + both[h][0:C] for h in heads]
            a_pow16 = [both[h][C:2 * C].astype(BF16) for h in heads]
        else:
            inv = [inv[h] + _dot(inv[h].astype(BF16), a_pow16[h]) for h in heads]
    b = INV_BASE
    while b < C:
        off = ((rb >> 1) == (cb >> 1)) & (rb != cb)
        a_off16 = [jnp.where(off, a[h], 0.0).astype(BF16) for h in heads]
        inv16 = [inv[h].astype(BF16) for h in heads]
        left = [_dot(inv16[h], a_off16[h]).astype(BF16) for h in heads]
        inv = [inv[h] - _dot(left[h], inv16[h]) for h in heads]
        rb = rb >> 1
        cb = cb >> 1
        b *= 2

    rhs16 = [jnp.concatenate([(chains[h]["v_ref"][:, cols[h]] * beta[h][:, col1[h]]).astype(BF16),
                              (kb[h] * e_gc[h][:, col1[h]]).astype(BF16)], axis=1) for h in heads]
    uw = [_dot(inv[h].astype(BF16), rhs16[h]) for h in heads]
    q_dec16 = [(chains[h]["q_ref"][:, cols[h]] * e_gc[h][:, col1[h]]).astype(BF16) for h in heads]
    k_dec_t16 = [(k[h] * chains[h]["e_rest"][:, col1[h]]).T.astype(BF16) for h in heads]

    s = [s_ref[ch["state"]] for ch in chains]
    s16 = [s[h].astype(BF16) for h in heads]
    ws = [_dot(jnp.concatenate([uw[h][:, HEAD_DIM:].astype(BF16), q_dec16[h]], axis=0), s16[h]) for h in heads]
    v_new16 = [(uw[h][:, 0:HEAD_DIM] - ws[h][0:C]).astype(BF16) for h in heads]
    for h, ch in enumerate(chains):
        ch["o_ref"][:, cols[h]] = ws[h][C:2 * C] + _dot(attn_qk16[h], v_new16[h])
    for h, ch in enumerate(chains):
        s_ref[ch["state"]] = s[h] * ch["e_end"][:, col1[h]] + _dot(k_dec_t16[h], v_new16[h])


def _gdn_scan(q, k, v, gcb, gct):
    L = q.shape[0]
    assert L % SCAN_CHUNK == 0
    n = L // SCAN_CHUNK
    fwd = lambda i: (i, 0)
    bwd = lambda i: (n - 1 - i, 0)
    tile_f = pl.BlockSpec((SCAN_CHUNK, GDN_W), fwd)
    tile_b = pl.BlockSpec((SCAN_CHUNK, GDN_W), bwd)
    return pl.pallas_call(
        _gdn_scan_kernel,
        grid=(n,),
        in_specs=[
            tile_f, tile_f, tile_f,
            pl.BlockSpec((SCAN_CHUNK, LANES), fwd),
            pl.BlockSpec((GDN_HEADS, SCAN_CHUNK), lambda i: (2 * i, 0)),
            tile_b, tile_b, tile_b,
            pl.BlockSpec((SCAN_CHUNK, LANES), bwd),
            pl.BlockSpec((GDN_HEADS, SCAN_CHUNK), lambda i: (2 * (n - 1 - i) + 1, 0)),
        ],
        out_specs=[tile_f, tile_b],
        out_shape=[jax.ShapeDtypeStruct((L, GDN_W), F32), jax.ShapeDtypeStruct((L, GDN_W), F32)],
        scratch_shapes=[pltpu.VMEM((2 * GDN_HEADS, HEAD_DIM, HEAD_DIM), F32)],
        compiler_params=_cparams(("arbitrary",)),
        name="gdn_scan",
    )(q, k, v, gcb, gct, q, k, v, gcb, gct)


def _flash_kernel(q_ref, k_ref, vt_ref, on_ref, o_ref, m_ref, acc_ref, st_ref, p_ref, *, tk):
    ki = pl.program_id(1)

    @pl.when(ki == 0)
    def _():
        m_ref[...] = jnp.full_like(m_ref, -jnp.inf)
        acc_ref[...] = jnp.zeros_like(acc_ref)

    ones = jnp.ones((FLASH_SUM_ROWS, tk), BF16)
    k = [k_ref[:, g * HEAD_DIM:(g + 1) * HEAD_DIM] for g in range(ATT_KV_HEADS)]
    vt1 = [jnp.concatenate([vt_ref[g * HEAD_DIM:(g + 1) * HEAD_DIM, :], ones], axis=0) for g in range(ATT_KV_HEADS)]
    blocks = [slice(r, r + FLASH_ROW_BLOCK) for r in range(0, tk, FLASH_ROW_BLOCK)]

    tq = q_ref.shape[0]
    units = [(h, slice(c, c + FLASH_Q_BLOCK)) for h in range(ATT_Q_HEADS) for c in range(0, tq, FLASH_Q_BLOCK)]

    def scores(h, qs):
        st_ref[h, :, qs] = _dot_nt(k[h // ATT_GROUP], q_ref[qs, h * HEAD_DIM:(h + 1) * HEAD_DIM])

    def softmax(h, qs):
        part = None
        for rows in blocks:
            x = st_ref[h, rows, qs]
            while x.shape[0] > SUBLANES:
                half = x.shape[0] // 2
                x = jnp.maximum(x[:half], x[half:])
            part = x if part is None else jnp.maximum(part, x)
        m_prev = m_ref[h, :, qs]
        m_new = jnp.maximum(m_prev, jnp.max(part, axis=0, keepdims=True))
        m_ref[h, :, qs] = m_new
        for rows in blocks:
            p_ref[h, rows, qs] = jnp.exp2(st_ref[h, rows, qs] - m_new).astype(BF16)
        return jnp.exp2(m_prev - m_new)

    def accumulate(h, qs, alpha):
        acc_ref[h, :, qs] = alpha * acc_ref[h, :, qs] + _dot(vt1[h // ATT_GROUP], p_ref[h, :, qs])

    scores(*units[0])
    pending = None
    for u, unit in enumerate(units):
        if u + 1 < len(units):
            scores(*units[u + 1])
        alpha = softmax(*unit)
        if pending is not None:
            accumulate(*pending)
        pending = (*unit, alpha)
    accumulate(*pending)

    @pl.when(ki == pl.num_programs(1) - 1)
    def _():
        for h in range(ATT_Q_HEADS):
            cols = slice(h * HEAD_DIM, (h + 1) * HEAD_DIM)
            o = (acc_ref[h, 0:HEAD_DIM, :] / acc_ref[h, HEAD_DIM:HEAD_DIM + 1, :]).T
            o_ref[:, cols] = _rms(o, on_ref[...]).astype(BF16)


def _flash_attn(q, k, vt, out_norm, *, tq=512, tk=1024):
    L = q.shape[0]
    assert L % tq == 0 and L % tk == 0 and tq % FLASH_Q_BLOCK == 0 and tk % FLASH_ROW_BLOCK == 0
    return pl.pallas_call(
        functools.partial(_flash_kernel, tk=tk),
        grid=(L // tq, L // tk),
        in_specs=[
            pl.BlockSpec((tq, ATT_Q_W), lambda i, j: (i, 0)),
            pl.BlockSpec((tk, ATT_KV_W), lambda i, j: (j, 0)),
            pl.BlockSpec((ATT_KV_W, tk), lambda i, j: (0, j)),
            pl.BlockSpec((1, HEAD_DIM), lambda i, j: (0, 0)),
        ],
        out_specs=pl.BlockSpec((tq, ATT_Q_W), lambda i, j: (i, 0)),
        out_shape=jax.ShapeDtypeStruct((L, ATT_Q_W), BF16),
        scratch_shapes=[
            pltpu.VMEM((ATT_Q_HEADS, 1, tq), F32),
            pltpu.VMEM((ATT_Q_HEADS, HEAD_DIM + FLASH_SUM_ROWS, tq), F32),
            pltpu.VMEM((ATT_Q_HEADS, tk, tq), F32),
            pltpu.VMEM((ATT_Q_HEADS, tk, tq), BF16),
        ],
        compiler_params=_cparams(("parallel", "arbitrary")),
        name="flash_attn",
    )(q, k, vt, out_norm)


def _out_proj_kernel(of_ref, ob_ref, z_ref, oa_ref, h_ref, gn_ref, w_ref, o_ref, mix_ref):
    att = _dot(oa_ref[...], w_ref[GDN_W:, :])
    for h in range(GDN_HEADS):
        cols = slice(h * HEAD_DIM, (h + 1) * HEAD_DIM)
        o = _rms(of_ref[:, cols] + ob_ref[:, cols], gn_ref[...])
        z = z_ref[:, cols]
        mix_ref[:, cols] = (o * (z * jax.nn.sigmoid(z))).astype(BF16)
    o_ref[...] = h_ref[...] + (att + _dot(mix_ref[...], w_ref[0:GDN_W, :]))


def _out_proj(o_fwd, o_bwd, z, oa, h1, gdn_norm, w_out, *, tm=512):
    L = h1.shape[0]
    assert L % tm == 0
    row = lambda i: (i, 0)
    return pl.pallas_call(
        _out_proj_kernel,
        grid=(L // tm,),
        in_specs=[
            pl.BlockSpec((tm, GDN_W), row),
            pl.BlockSpec((tm, GDN_W), row),
            pl.BlockSpec((tm, GDN_W), row),
            pl.BlockSpec((tm, ATT_Q_W), row),
            pl.BlockSpec((tm, D_MODEL), row),
            pl.BlockSpec((1, HEAD_DIM), lambda i: (0, 0)),
            pl.BlockSpec((GDN_W + ATT_Q_W, D_MODEL), lambda i: (0, 0)),
        ],
        out_specs=pl.BlockSpec((tm, D_MODEL), row),
        out_shape=jax.ShapeDtypeStruct((L, D_MODEL), F32),
        scratch_shapes=[pltpu.VMEM((tm, GDN_W), BF16)],
        compiler_params=_cparams(("parallel",)),
        name="out_proj",
    )(o_fwd, o_bwd, z, oa, h1, gdn_norm, w_out)


def _rope_tables(L):
    assert L % GRID_W == 0
    rows = L // GRID_W
    freqs = ROPE_THETA ** (-jnp.arange(0, AXIS_DIM, 2, dtype=F32) / AXIS_DIM)
    ang_r = jnp.arange(rows, dtype=F32)[:, None] * freqs[None, :]
    ang_c = jnp.arange(GRID_W, dtype=F32)[:, None] * freqs[None, :]
    per_row = lambda tab: jnp.repeat(tab, GRID_W, axis=0)
    per_col = lambda tab: jnp.tile(tab, (rows, 1))
    cos_r, sin_r = per_row(jnp.cos(ang_r)), per_row(jnp.sin(ang_r))
    cos_c, sin_c = per_col(jnp.cos(ang_c)), per_col(jnp.sin(ang_c))
    zero = jnp.zeros_like(cos_r)
    cos = jnp.concatenate([cos_r, cos_r, cos_c, cos_c], axis=-1)
    sin_lo = jnp.concatenate([-sin_r, zero, -sin_c, zero], axis=-1)
    sin_hi = jnp.concatenate([zero, sin_r, zero, sin_c], axis=-1)
    return cos, sin_lo, sin_hi


def _pad_lanes(x):
    x = x.reshape(1, -1)
    return jnp.pad(x, ((0, 0), (0, LANES - x.shape[1])))


def _encode(x, p, rope):
    h1 = _ffn(x, p["ffn1_norm"], p["ffn1_wgu"], p["ffn1_wd"], p["final_norm"], final=False)
    cos, sin_lo, sin_hi = rope
    qg, kg, vg, z, qa, ka, va, gcb, gct = _mix_proj(h1, p["mix_norm"], p["w_in"], p["w_qk"], p["w_vt"], p["w_ab"],
                                                    p["q_norm"], p["k_norm"], cos, sin_lo, sin_hi, p["alog"],
                                                    p["dtb"], p["conv_w"])
    o_fwd, o_bwd = _gdn_scan(qg, kg, vg, gcb, gct)
    oa = _flash_attn(qa, ka, va, p["attn_out_norm"])
    h2 = _out_proj(o_fwd, o_bwd, z, oa, h1, p["gdn_out_norm"], p["w_out"])
    return _ffn(h2, p["ffn2_norm"], p["ffn2_wgu"], p["ffn2_wd"], p["final_norm"], final=True)


def _prepare_params(ffn1_norm, ffn1_w_gate, ffn1_w_up, ffn1_w_down, mix_norm, w_in, conv_w, a_log_fwd, a_log_bwd,
                    dt_bias_fwd, dt_bias_bwd, gdn_out_norm, q_norm, k_norm, attn_out_norm, w_out, ffn2_norm,
                    ffn2_w_gate, ffn2_w_up, ffn2_w_down, final_norm):
    w_in0 = w_in[0]
    att0 = GDN_PROJ_W + GATE_COLS
    w_ab = jnp.pad(w_in0[:, GDN_PROJ_W:att0], ((0, 0), (0, LANES - GATE_COLS)))
    return dict(
        ffn1_norm=ffn1_norm[0].reshape(1, -1),
        ffn1_wgu=_interleave_gate_up(ffn1_w_gate[0], ffn1_w_up[0], FFN_HIDDEN_TILE).astype(BF16),
        ffn1_wd=ffn1_w_down[0].astype(BF16),
        mix_norm=mix_norm[0].reshape(1, -1),
        w_in=w_in0[:, :GDN_PROJ_W].astype(BF16),
        w_qk=w_in0[:, att0:att0 + ATT_Q_W + ATT_KV_W].astype(BF16),
        w_vt=w_in0[:, att0 + ATT_Q_W + ATT_KV_W:].T.astype(BF16),
        w_ab=w_ab.astype(BF16),
        conv_w=jnp.pad(conv_w[0], ((0, SUBLANES - CONV_K), (0, 0))),
        alog=_pad_lanes(jnp.concatenate([a_log_fwd[0], a_log_bwd[0]])),
        dtb=_pad_lanes(jnp.concatenate([dt_bias_fwd[0], dt_bias_bwd[0]])),
        gdn_out_norm=gdn_out_norm[0].reshape(1, -1),
        q_norm=q_norm[0].reshape(1, -1), k_norm=k_norm[0].reshape(1, -1),
        attn_out_norm=attn_out_norm[0].reshape(1, -1),
        w_out=w_out[0].astype(BF16),
        ffn2_norm=ffn2_norm[0].reshape(1, -1),
        ffn2_wgu=_interleave_gate_up(ffn2_w_gate[0], ffn2_w_up[0], FFN_HIDDEN_TILE).astype(BF16),
        ffn2_wd=ffn2_w_down[0].astype(BF16),
        final_norm=final_norm.reshape(1, -1),
    )


def kernel(x_prompt, x_sample, ffn1_norm, ffn1_w_gate, ffn1_w_up, ffn1_w_down, mix_norm, w_in, conv_w, a_log_fwd,
           a_log_bwd, dt_bias_fwd, dt_bias_bwd, gdn_out_norm, q_norm, k_norm, attn_out_norm, w_out, ffn2_norm,
           ffn2_w_gate, ffn2_w_up, ffn2_w_down, final_norm):
    assert x_prompt.shape[0] == 1 and x_sample.shape[0] == 1
    p = _prepare_params(ffn1_norm, ffn1_w_gate, ffn1_w_up, ffn1_w_down, mix_norm, w_in, conv_w, a_log_fwd,
                        a_log_bwd, dt_bias_fwd, dt_bias_bwd, gdn_out_norm, q_norm, k_norm, attn_out_norm, w_out,
                        ffn2_norm, ffn2_w_gate, ffn2_w_up, ffn2_w_down, final_norm)
    rope = _rope_tables(max(x_prompt.shape[1], x_sample.shape[1]))
    y_prompt = _encode(x_prompt[0], p, rope)
    y_sample = _encode(x_sample[0], p, rope)
    return (y_prompt[None], y_sample[None])
```

```python
import functools
import math

import jax
import jax.numpy as jnp
from jax import lax
from jax.experimental import pallas as pl
from jax.experimental.pallas import tpu as pltpu

D_MODEL = 2048
HEAD_DIM = 128
GDN_HEADS = 8
GDN_W = GDN_HEADS * HEAD_DIM
ATT_Q_HEADS = 8
ATT_KV_HEADS = 2
ATT_GROUP = ATT_Q_HEADS // ATT_KV_HEADS
ATT_Q_W = ATT_Q_HEADS * HEAD_DIM
ATT_KV_W = ATT_KV_HEADS * HEAD_DIM
GDN_PROJ_W = 4 * GDN_W
GATE_COLS = 4 * GDN_HEADS
D_FF = 5632
CONV_K = 5
GRID_W = 64
AXIS_DIM = HEAD_DIM // 2
ROPE_THETA = 10000.0
EPS = 1e-6

LANES = 128
SUBLANES = 8
SCAN_CHUNK = 128
PROJ_COL_TILE = 256
CONV_HALO = 16
INV_BASE_LEVELS = 2
INV_BASE = 2 ** INV_BASE_LEVELS
EXP2_SCALE = HEAD_DIM ** -0.5 * math.log2(math.e)
FLASH_ROW_BLOCK = 64
FLASH_SUM_ROWS = 16
FLASH_Q_BLOCK = 512
F32 = jnp.float32
BF16 = jnp.bfloat16

V7X_VMEM_BYTES = 64 * 1024 * 1024
_VMEM_LIMIT = V7X_VMEM_BYTES * 7 // 8


def _cparams(semantics):
    return pltpu.CompilerParams(dimension_semantics=semantics, vmem_limit_bytes=_VMEM_LIMIT)


def _rms(x, g):
    return x * lax.rsqrt(jnp.mean(x * x, axis=-1, keepdims=True) + EPS) * g


def _dot(a, b):
    return jnp.dot(a, b, preferred_element_type=F32)


def _dot_nt(a, b):
    return lax.dot_general(a, b, (((1,), (1,)), ((), ())), preferred_element_type=F32)


def _split3(x):
    hi = x.astype(BF16)
    r = x - hi.astype(F32)
    mid = r.astype(BF16)
    lo = (r - mid.astype(F32)).astype(BF16)
    return hi, mid, lo


def _ffn_kernel(x_ref, g_ref, wg_ref, wu_ref, wd_ref, fg_ref, o_ref, xn_ref, *, final):
    j = pl.program_id(1)
    last = pl.num_programs(1) - 1

    def hidden_tile():
        xn = xn_ref[...]
        gate = _dot(xn, wg_ref[...])
        up = _dot(xn, wu_ref[...])
        act = (gate * jax.nn.sigmoid(gate) * up).astype(BF16)
        return _dot(act, wd_ref[...])

    @pl.when(j == 0)
    def _():
        xn_ref[...] = _rms(x_ref[...], g_ref[...]).astype(BF16)
        o_ref[...] = hidden_tile()

    @pl.when(jnp.logical_and(j > 0, j < last))
    def _():
        o_ref[...] += hidden_tile()

    @pl.when(j == last)
    def _():
        h = x_ref[...] + 0.5 * (o_ref[...] + hidden_tile())
        if final:
            h = _rms(h, fg_ref[...])
        o_ref[...] = h


def _ffn(x, norm_g, wg, wu, wd, final_g, *, final, tm=512, tf=512):
    L = x.shape[0]
    assert L % tm == 0 and D_FF % tf == 0
    grid = (L // tm, D_FF // tf)
    return pl.pallas_call(
        functools.partial(_ffn_kernel, final=final),
        grid=grid,
        in_specs=[
            pl.BlockSpec((tm, D_MODEL), lambda i, j: (i, 0)),
            pl.BlockSpec((1, D_MODEL), lambda i, j: (0, 0)),
            pl.BlockSpec((D_MODEL, tf), lambda i, j: (0, j)),
            pl.BlockSpec((D_MODEL, tf), lambda i, j: (0, j)),
            pl.BlockSpec((tf, D_MODEL), lambda i, j: (j, 0)),
            pl.BlockSpec((1, D_MODEL), lambda i, j: (0, 0)),
        ],
        out_specs=pl.BlockSpec((tm, D_MODEL), lambda i, j: (i, 0)),
        out_shape=jax.ShapeDtypeStruct((L, D_MODEL), F32),
        scratch_shapes=[pltpu.VMEM((tm, D_MODEL), BF16)],
        compiler_params=_cparams(("parallel", "arbitrary")),
        name="ffn",
    )(x, norm_g, wg, wu, wd, final_g)


def _rope(x, cos, sin_lo, sin_hi):
    return (x * cos + pltpu.roll(x, AXIS_DIM // 2, axis=1) * sin_hi
            + pltpu.roll(x, HEAD_DIM - AXIS_DIM // 2, axis=1) * sin_lo)


def _mix_proj_kernel(x_ref, xprev_ref, xnext_ref, g_ref, wg_ref, w_ref, wvt_ref, wab_ref, qn_ref, kn_ref, cos_ref,
                     slo_ref, shi_ref, alog_ref, dtb_ref, cw_ref, qg_ref, kg_ref, vg_ref, z_ref, q_ref, k_ref,
                     vt_ref, gcb_ref, gct_ref, ext_ref, *, tm):
    i = pl.program_id(0)
    pad = CONV_K // 2
    g = g_ref[...]
    xn = _rms(x_ref[...], g).astype(BF16)

    xn_ext = jnp.concatenate([_rms(xprev_ref[...], g).astype(BF16), xn, _rms(xnext_ref[...], g).astype(BF16)], axis=0)
    keep_prev = jnp.where(i == 0, 0.0, 1.0)
    keep_next = jnp.where(i == pl.num_programs(0) - 1, 0.0, 1.0)
    cos = cos_ref[...]
    slo = slo_ref[...]
    shi = shi_ref[...]

    def project_ext(c):
        cs = slice(c, c + PROJ_COL_TILE)
        e = _dot(xn_ext, wg_ref[:, cs])
        ext_ref[0:CONV_HALO, cs] = e[0:CONV_HALO] * keep_prev
        ext_ref[CONV_HALO:CONV_HALO + tm, cs] = e[CONV_HALO:CONV_HALO + tm]
        ext_ref[CONV_HALO + tm:, cs] = e[CONV_HALO + tm:] * keep_next

    def project_z(c):
        z_ref[:, c:c + PROJ_COL_TILE] = _dot(xn, wg_ref[:, 3 * GDN_W + c:3 * GDN_W + c + PROJ_COL_TILE])

    def project_attn_pair(h):
        pair = _dot(xn, w_ref[:, h * HEAD_DIM:(h + 2) * HEAD_DIM])
        for half in range(2):
            head = pair[:, half * HEAD_DIM:(half + 1) * HEAD_DIM]
            n = h + half
            if n < ATT_Q_HEADS:
                q_ref[:, n * HEAD_DIM:(n + 1) * HEAD_DIM] = (
                    _rope(_rms(head, qn_ref[...]), cos, slo, shi) * EXP2_SCALE).astype(BF16)
            else:
                n -= ATT_Q_HEADS
                k_ref[:, n * HEAD_DIM:(n + 1) * HEAD_DIM] = _rope(_rms(head, kn_ref[...]), cos, slo, shi).astype(BF16)

    def project_vt():
        vt_ref[...] = _dot_nt(wvt_ref[...], xn).astype(BF16)

    def conv_slab(s):
        cols = slice(s * HEAD_DIM, (s + 1) * HEAD_DIM)
        y = None
        for t in range(CONV_K):
            term = ext_ref[CONV_HALO - pad + t:CONV_HALO - pad + t + tm, cols] * cw_ref[t:t + 1, cols]
            y = term if y is None else y + term
        y = y * jax.nn.sigmoid(y)
        part, h = divmod(s, GDN_HEADS)
        hc = slice(h * HEAD_DIM, (h + 1) * HEAD_DIM)
        if part == 0:
            qg_ref[:, hc] = y * (lax.rsqrt(jnp.sum(y * y, axis=-1, keepdims=True) + EPS) * (HEAD_DIM ** -0.5))
        elif part == 1:
            kg_ref[:, hc] = y * lax.rsqrt(jnp.sum(y * y, axis=-1, keepdims=True) + EPS)
        else:
            vg_ref[:, hc] = y

    tiles = range(0, GDN_W, PROJ_COL_TILE)
    mxu_q = [functools.partial(project_ext, c) for c in tiles]
    mxu_k = [functools.partial(project_ext, GDN_W + c) for c in tiles]
    mxu_v = [functools.partial(project_ext, 2 * GDN_W + c) for c in tiles]
    mxu_z = [functools.partial(project_z, c) for c in tiles]
    mxu_att = [functools.partial(project_attn_pair, h) for h in range(0, ATT_Q_HEADS + ATT_KV_HEADS, 2)]
    mxu_att.append(project_vt)
    for piece in mxu_q:
        piece()
    n_att = len(mxu_att) // 2
    stages = [(mxu_k + mxu_att[:n_att], range(0, GDN_HEADS)),
              (mxu_v + mxu_att[n_att:] + mxu_z[:2], range(GDN_HEADS, 2 * GDN_HEADS)),
              (mxu_z[2:], range(2 * GDN_HEADS, 3 * GDN_HEADS))]
    for pieces, slabs in stages:
        for n in range(max(len(pieces), len(slabs))):
            if n < len(pieces):
                pieces[n]()
            if n < len(slabs):
                conv_slab(slabs[n])

    ab = _dot(xn, wab_ref[...])
    t = ab + dtb_ref[...]
    softplus = jnp.maximum(t, 0.0) + jnp.log1p(jnp.exp(-jnp.abs(t)))
    log_decay = -jnp.exp(alog_ref[...]) * softplus
    lane = lax.broadcasted_iota(jnp.int32, ab.shape, 1)
    gb = jnp.where(lane < 2 * GDN_HEADS, log_decay, jax.nn.sigmoid(ab))

    r = lax.broadcasted_iota(jnp.int32, (tm, tm), 0)
    c = lax.broadcasted_iota(jnp.int32, (tm, tm), 1)
    same = (r // SCAN_CHUNK) == (c // SCAN_CHUNK)
    m_lo = jnp.where(same & (c <= r), 1.0, 0.0).astype(BF16)
    m_up = jnp.where(same & (c >= r), 1.0, 0.0).astype(BF16)
    parts = _split3(gb)
    gc_f = _dot(m_lo, parts[0]) + _dot(m_lo, parts[1]) + _dot(m_lo, parts[2])
    gc_b = _dot(m_up, parts[0]) + _dot(m_up, parts[1]) + _dot(m_up, parts[2])
    gcb = jnp.where(lane < GDN_HEADS, gc_f, jnp.where(lane < 2 * GDN_HEADS, gc_b, gb))
    gcb_ref[...] = gcb
    for n in range(tm // SCAN_CHUNK):
        gct = gcb[n * SCAN_CHUNK:(n + 1) * SCAN_CHUNK, :].T
        gct_ref[n * 2 * GDN_HEADS:(n + 1) * 2 * GDN_HEADS, :] = gct[0:2 * GDN_HEADS, :]


def _mix_proj(x, norm_g, w_in, w_qk, w_vt, w_ab, q_norm, k_norm, cos, sin_lo, sin_hi, alog, dtb, conv_w8, *, tm=256):
    L = x.shape[0]
    assert L % tm == 0 and tm % SCAN_CHUNK == 0 and tm % CONV_HALO == 0
    n_halo = L // CONV_HALO
    per = tm // CONV_HALO
    row = lambda i: (i, 0)
    fixed = lambda i: (0, 0)
    resident = pl.Buffered(1)
    return pl.pallas_call(
        functools.partial(_mix_proj_kernel, tm=tm),
        grid=(L // tm,),
        in_specs=[
            pl.BlockSpec((tm, D_MODEL), row),
            pl.BlockSpec((CONV_HALO, D_MODEL), lambda i: (jnp.maximum(i * per - 1, 0), 0)),
            pl.BlockSpec((CONV_HALO, D_MODEL), lambda i: (jnp.minimum((i + 1) * per, n_halo - 1), 0)),
            pl.BlockSpec((1, D_MODEL), fixed),
            pl.BlockSpec((D_MODEL, GDN_PROJ_W), fixed, pipeline_mode=resident),
            pl.BlockSpec((D_MODEL, ATT_Q_W + ATT_KV_W), fixed, pipeline_mode=resident),
            pl.BlockSpec((ATT_KV_W, D_MODEL), fixed, pipeline_mode=resident),
            pl.BlockSpec((D_MODEL, LANES), fixed),
            pl.BlockSpec((1, HEAD_DIM), fixed),
            pl.BlockSpec((1, HEAD_DIM), fixed),
            pl.BlockSpec((tm, HEAD_DIM), row),
            pl.BlockSpec((tm, HEAD_DIM), row),
            pl.BlockSpec((tm, HEAD_DIM), row),
            pl.BlockSpec((1, LANES), fixed),
            pl.BlockSpec((1, LANES), fixed),
            pl.BlockSpec((SUBLANES, 3 * GDN_W), fixed),
        ],
        out_specs=[
            pl.BlockSpec((tm, GDN_W), row),
            pl.BlockSpec((tm, GDN_W), row),
            pl.BlockSpec((tm, GDN_W), row),
            pl.BlockSpec((tm, GDN_W), row),
            pl.BlockSpec((tm, ATT_Q_W), row),
            pl.BlockSpec((tm, ATT_KV_W), row),
            pl.BlockSpec((ATT_KV_W, tm), lambda i: (0, i)),
            pl.BlockSpec((tm, LANES), row),
            pl.BlockSpec((tm // SCAN_CHUNK * 2 * GDN_HEADS, SCAN_CHUNK), row),
        ],
        out_shape=[
            jax.ShapeDtypeStruct((L, GDN_W), F32),
            jax.ShapeDtypeStruct((L, GDN_W), F32),
            jax.ShapeDtypeStruct((L, GDN_W), F32),
            jax.ShapeDtypeStruct((L, GDN_W), F32),
            jax.ShapeDtypeStruct((L, ATT_Q_W), BF16),
            jax.ShapeDtypeStruct((L, ATT_KV_W), BF16),
            jax.ShapeDtypeStruct((ATT_KV_W, L), BF16),
            jax.ShapeDtypeStruct((L, LANES), F32),
            jax.ShapeDtypeStruct((L // SCAN_CHUNK * 2 * GDN_HEADS, SCAN_CHUNK), F32),
        ],
        scratch_shapes=[pltpu.VMEM((tm + 2 * CONV_HALO, 3 * GDN_W), F32)],
        compiler_params=_cparams(("parallel",)),
        name="mix_proj",
    )(x, x, x, norm_g, w_in, w_qk, w_vt, w_ab, q_norm, k_norm, cos, sin_lo, sin_hi, alog, dtb, conv_w8)


def _gdn_scan_kernel(qf_ref, kf_ref, vf_ref, gcbf_ref, gctf_ref, qb_ref, kb_ref, vb_ref, gcbb_ref, gctb_ref,
                     of_ref, ob_ref, s_ref):
    C = SCAN_CHUNK

    @pl.when(pl.program_id(0) == 0)
    def _():
        s_ref[...] = jnp.zeros_like(s_ref)

    r = lax.broadcasted_iota(jnp.int32, (C, C), 0)
    c = lax.broadcasted_iota(jnp.int32, (C, C), 1)
    eye = jnp.where(r == c, 1.0, 0.0)

    chains = []
    for fwd, (q_ref, k_ref, v_ref, gcb_ref, gct_ref, o_ref) in (
            (True, (qf_ref, kf_ref, vf_ref, gcbf_ref, gctf_ref, of_ref)),
            (False, (qb_ref, kb_ref, vb_ref, gcbb_ref, gctb_ref, ob_ref))):
        dif = r - c if fwd else c - r
        lane0 = 0 if fwd else GDN_HEADS
        gcb = gcb_ref[...]
        gc = gcb[:, lane0:lane0 + GDN_HEADS]
        beta = gcb[:, 2 * GDN_HEADS + lane0:3 * GDN_HEADS + lane0]
        g_end = gc[C - 1:C, :] if fwd else gc[0:1, :]
        shared = dict(q_ref=q_ref, k_ref=k_ref, v_ref=v_ref, o_ref=o_ref, m_incl=dif >= 0, m_strict=dif > 0,
                      gc=gc, beta=beta, e_gc=jnp.exp(gc), e_rest=jnp.exp(g_end - gc), e_end=jnp.exp(g_end),
                      gct=gct_ref[...])
        for h in range(GDN_HEADS):
            chains.append(dict(shared, cols=slice(h * HEAD_DIM, (h + 1) * HEAD_DIM), col1=slice(h, h + 1),
                               state=lane0 + h))
    heads = range(len(chains))
    cols = [ch["cols"] for ch in chains]
    col1 = [ch["col1"] for ch in chains]
    beta = [ch["beta"] for ch in chains]
    e_gc = [ch["e_gc"] for ch in chains]

    k = [chains[h]["k_ref"][:, cols[h]] for h in heads]
    k16 = [k[h].astype(BF16) for h in heads]
    kb = [k[h] * beta[h][:, col1[h]] for h in heads]
    q16 = [chains[h]["q_ref"][:, cols[h]].astype(BF16) for h in heads]
    kq = [_dot_nt(jnp.concatenate([kb[h].astype(BF16), q16[h]], axis=0), k16[h]) for h in heads]
    decay = []
    for ch in chains:
        diff = ch["gc"][:, ch["col1"]] - ch["gct"][ch["col1"], :]
        decay.append(jnp.where(ch["m_incl"], jnp.exp(jnp.where(ch["m_incl"], diff, 0.0)), 0.0))
    a = [jnp.where(chains[h]["m_strict"], kq[h][0:C] * decay[h], 0.0) for h in heads]
    attn_qk16 = [(kq[h][C:2 * C] * decay[h]).astype(BF16) for h in heads]

    rb = r >> INV_BASE_LEVELS
    cb = c >> INV_BASE_LEVELS
    a_d = [jnp.where(rb == cb, a[h], 0.0) for h in heads]
    inv = [eye - a_d[h] for h in heads]
    a_d16 = [a_d[h].astype(BF16) for h in heads]
    a_pow16 = [_dot(a_d16[h], a_d16[h]).astype(BF16) for h in heads]
    for level in range(INV_BASE_LEVELS - 1):
        if level < INV_BASE_LEVELS - 2:
            both = [_dot(jnp.concatenate([inv[h].astype(BF16), a_pow16[h]], axis=0), a_pow16[h]) for h in heads]
            inv = [inv[h] + both[h][0:C] for h in heads]
            a_pow16 = [both[h][C:2 * C].astype(BF16) for h in heads]
        else:
            inv = [inv[h] + _dot(inv[h].astype(BF16), a_pow16[h]) for h in heads]
    b = INV_BASE
    while b < C:
        off = ((rb >> 1) == (cb >> 1)) & (rb != cb)
        a_off16 = [jnp.where(off, a[h], 0.0).astype(BF16) for h in heads]
        inv16 = [inv[h].astype(BF16) for h in heads]
        left = [_dot(inv16[h], a_off16[h]).astype(BF16) for h in heads]
        inv = [inv[h] - _dot(left[h], inv16[h]) for h in heads]
        rb = rb >> 1
        cb = cb >> 1
        b *= 2

    rhs16 = [jnp.concatenate([(chains[h]["v_ref"][:, cols[h]] * beta[h][:, col1[h]]).astype(BF16),
                              (kb[h] * e_gc[h][:, col1[h]]).astype(BF16)], axis=1) for h in heads]
    uw = [_dot(inv[h].astype(BF16), rhs16[h]) for h in heads]
    q_dec16 = [(chains[h]["q_ref"][:, cols[h]] * e_gc[h][:, col1[h]]).astype(BF16) for h in heads]
    k_dec_t16 = [(k[h] * chains[h]["e_rest"][:, col1[h]]).T.astype(BF16) for h in heads]

    s = [s_ref[ch["state"]] for ch in chains]
    s16 = [s[h].astype(BF16) for h in heads]
    ws = [_dot(jnp.concatenate([uw[h][:, HEAD_DIM:].astype(BF16), q_dec16[h]], axis=0), s16[h]) for h in heads]
    v_new16 = [(uw[h][:, 0:HEAD_DIM] - ws[h][0:C]).astype(BF16) for h in heads]
    for h, ch in enumerate(chains):
        ch["o_ref"][:, cols[h]] = ws[h][C:2 * C] + _dot(attn_qk16[h], v_new16[h])
    for h, ch in enumerate(chains):
        s_ref[ch["state"]] = s[h] * ch["e_end"][:, col1[h]] + _dot(k_dec_t16[h], v_new16[h])


def _gdn_scan(q, k, v, gcb, gct):
    L = q.shape[0]
    assert L % SCAN_CHUNK == 0
    n = L // SCAN_CHUNK
    fwd = lambda i: (i, 0)
    bwd = lambda i: (n - 1 - i, 0)
    tile_f = pl.BlockSpec((SCAN_CHUNK, GDN_W), fwd)
    tile_b = pl.BlockSpec((SCAN_CHUNK, GDN_W), bwd)
    return pl.pallas_call(
        _gdn_scan_kernel,
        grid=(n,),
        in_specs=[
            tile_f, tile_f, tile_f,
            pl.BlockSpec((SCAN_CHUNK, LANES), fwd),
            pl.BlockSpec((GDN_HEADS, SCAN_CHUNK), lambda i: (2 * i, 0)),
            tile_b, tile_b, tile_b,
            pl.BlockSpec((SCAN_CHUNK, LANES), bwd),
            pl.BlockSpec((GDN_HEADS, SCAN_CHUNK), lambda i: (2 * (n - 1 - i) + 1, 0)),
        ],
        out_specs=[tile_f, tile_b],
        out_shape=[jax.ShapeDtypeStruct((L, GDN_W), F32), jax.ShapeDtypeStruct((L, GDN_W), F32)],
        scratch_shapes=[pltpu.VMEM((2 * GDN_HEADS, HEAD_DIM, HEAD_DIM), F32)],
        compiler_params=_cparams(("arbitrary",)),
        name="gdn_scan",
    )(q, k, v, gcb, gct, q, k, v, gcb, gct)


def _flash_kernel(q_ref, k_ref, vt_ref, on_ref, o_ref, m_ref, acc_ref, st_ref, p_ref, *, tk):
    ki = pl.program_id(1)

    @pl.when(ki == 0)
    def _():
        m_ref[...] = jnp.full_like(m_ref, -jnp.inf)
        acc_ref[...] = jnp.zeros_like(acc_ref)

    ones = jnp.ones((FLASH_SUM_ROWS, tk), BF16)
    k = [k_ref[:, g * HEAD_DIM:(g + 1) * HEAD_DIM] for g in range(ATT_KV_HEADS)]
    vt1 = [jnp.concatenate([vt_ref[g * HEAD_DIM:(g + 1) * HEAD_DIM, :], ones], axis=0) for g in range(ATT_KV_HEADS)]
    blocks = [slice(r, r + FLASH_ROW_BLOCK) for r in range(0, tk, FLASH_ROW_BLOCK)]

    tq = q_ref.shape[0]
    units = [(h, slice(c, c + FLASH_Q_BLOCK)) for h in range(ATT_Q_HEADS) for c in range(0, tq, FLASH_Q_BLOCK)]

    def scores(h, qs):
        st_ref[h, :, qs] = _dot_nt(k[h // ATT_GROUP], q_ref[qs, h * HEAD_DIM:(h + 1) * HEAD_DIM]).astype(BF16)

    def softmax(h, qs):
        part = None
        for rows in blocks:
            x = st_ref[h, rows, qs]
            while x.shape[0] > 2 * SUBLANES:
                half = x.shape[0] // 2
                x = jnp.maximum(x[:half], x[half:])
            part = x if part is None else jnp.maximum(part, x)
        m_prev = m_ref[h, :, qs]
        m_new = jnp.maximum(m_prev, jnp.max(part.astype(F32), axis=0, keepdims=True))
        m_ref[h, :, qs] = m_new
        for rows in blocks:
            p_ref[h, rows, qs] = jnp.exp2(st_ref[h, rows, qs].astype(F32) - m_new).astype(BF16)
        return jnp.exp2(m_prev - m_new)

    def accumulate(h, qs, alpha):
        acc_ref[h, :, qs] = alpha * acc_ref[h, :, qs] + _dot(vt1[h // ATT_GROUP], p_ref[h, :, qs])

    scores(*units[0])
    pending = None
    for u, unit in enumerate(units):
        if u + 1 < len(units):
            scores(*units[u + 1])
        alpha = softmax(*unit)
        if pending is not None:
            accumulate(*pending)
        pending = (*unit, alpha)
    accumulate(*pending)

    @pl.when(ki == pl.num_programs(1) - 1)
    def _():
        for h in range(ATT_Q_HEADS):
            cols = slice(h * HEAD_DIM, (h + 1) * HEAD_DIM)
            o = (acc_ref[h, 0:HEAD_DIM, :] / acc_ref[h, HEAD_DIM:HEAD_DIM + 1, :]).T
            o_ref[:, cols] = _rms(o, on_ref[...]).astype(BF16)


def _flash_attn(q, k, vt, out_norm, *, tq=512, tk=1024):
    L = q.shape[0]
    assert L % tq == 0 and L % tk == 0 and tq % FLASH_Q_BLOCK == 0 and tk % FLASH_ROW_BLOCK == 0
    return pl.pallas_call(
        functools.partial(_flash_kernel, tk=tk),
        grid=(L // tq, L // tk),
        in_specs=[
            pl.BlockSpec((tq, ATT_Q_W), lambda i, j: (i, 0)),
            pl.BlockSpec((tk, ATT_KV_W), lambda i, j: (j, 0)),
            pl.BlockSpec((ATT_KV_W, tk), lambda i, j: (0, j)),
            pl.BlockSpec((1, HEAD_DIM), lambda i, j: (0, 0)),
        ],
        out_specs=pl.BlockSpec((tq, ATT_Q_W), lambda i, j: (i, 0)),
        out_shape=jax.ShapeDtypeStruct((L, ATT_Q_W), BF16),
        scratch_shapes=[
            pltpu.VMEM((ATT_Q_HEADS, 1, tq), F32),
            pltpu.VMEM((ATT_Q_HEADS, HEAD_DIM + FLASH_SUM_ROWS, tq), F32),
            pltpu.VMEM((ATT_Q_HEADS, tk, tq), BF16),
            pltpu.VMEM((ATT_Q_HEADS, tk, tq), BF16),
        ],
        compiler_params=_cparams(("parallel", "arbitrary")),
        name="flash_attn",
    )(q, k, vt, out_norm)


def _out_proj_kernel(of_ref, ob_ref, z_ref, oa_ref, h_ref, gn_ref, w_ref, o_ref, mix_ref):
    att = _dot(oa_ref[...], w_ref[GDN_W:, :])
    for h in range(GDN_HEADS):
        cols = slice(h * HEAD_DIM, (h + 1) * HEAD_DIM)
        o = _rms(of_ref[:, cols] + ob_ref[:, cols], gn_ref[...])
        z = z_ref[:, cols]
        mix_ref[:, cols] = (o * (z * jax.nn.sigmoid(z))).astype(BF16)
    o_ref[...] = h_ref[...] + (att + _dot(mix_ref[...], w_ref[0:GDN_W, :]))


def _out_proj(o_fwd, o_bwd, z, oa, h1, gdn_norm, w_out, *, tm=512):
    L = h1.shape[0]
    assert L % tm == 0
    row = lambda i: (i, 0)
    return pl.pallas_call(
        _out_proj_kernel,
        grid=(L // tm,),
        in_specs=[
            pl.BlockSpec((tm, GDN_W), row),
            pl.BlockSpec((tm, GDN_W), row),
            pl.BlockSpec((tm, GDN_W), row),
            pl.BlockSpec((tm, ATT_Q_W), row),
            pl.BlockSpec((tm, D_MODEL), row),
            pl.BlockSpec((1, HEAD_DIM), lambda i: (0, 0)),
            pl.BlockSpec((GDN_W + ATT_Q_W, D_MODEL), lambda i: (0, 0)),
        ],
        out_specs=pl.BlockSpec((tm, D_MODEL), row),
        out_shape=jax.ShapeDtypeStruct((L, D_MODEL), F32),
        scratch_shapes=[pltpu.VMEM((tm, GDN_W), BF16)],
        compiler_params=_cparams(("parallel",)),
        name="out_proj",
    )(o_fwd, o_bwd, z, oa, h1, gdn_norm, w_out)


def _rope_tables(L):
    assert L % GRID_W == 0
    rows = L // GRID_W
    freqs = ROPE_THETA ** (-jnp.arange(0, AXIS_DIM, 2, dtype=F32) / AXIS_DIM)
    ang_r = jnp.arange(rows, dtype=F32)[:, None] * freqs[None, :]
    ang_c = jnp.arange(GRID_W, dtype=F32)[:, None] * freqs[None, :]
    per_row = lambda tab: jnp.repeat(tab, GRID_W, axis=0)
    per_col = lambda tab: jnp.tile(tab, (rows, 1))
    cos_r, sin_r = per_row(jnp.cos(ang_r)), per_row(jnp.sin(ang_r))
    cos_c, sin_c = per_col(jnp.cos(ang_c)), per_col(jnp.sin(ang_c))
    zero = jnp.zeros_like(cos_r)
    cos = jnp.concatenate([cos_r, cos_r, cos_c, cos_c], axis=-1)
    sin_lo = jnp.concatenate([-sin_r, zero, -sin_c, zero], axis=-1)
    sin_hi = jnp.concatenate([zero, sin_r, zero, sin_c], axis=-1)
    return cos, sin_lo, sin_hi


def _pad_lanes(x):
    x = x.reshape(1, -1)
    return jnp.pad(x, ((0, 0), (0, LANES - x.shape[1])))


def _encode(x, p, rope):
    h1 = _ffn(x, p["ffn1_norm"], p["ffn1_wg"], p["ffn1_wu"], p["ffn1_wd"], p["final_norm"], final=False)
    cos, sin_lo, sin_hi = rope
    qg, kg, vg, z, qa, ka, va, gcb, gct = _mix_proj(h1, p["mix_norm"], p["w_in"], p["w_qk"], p["w_vt"], p["w_ab"],
                                                    p["q_norm"], p["k_norm"], cos, sin_lo, sin_hi, p["alog"],
                                                    p["dtb"], p["conv_w"])
    o_fwd, o_bwd = _gdn_scan(qg, kg, vg, gcb, gct)
    oa = _flash_attn(qa, ka, va, p["attn_out_norm"])
    h2 = _out_proj(o_fwd, o_bwd, z, oa, h1, p["gdn_out_norm"], p["w_out"])
    return _ffn(h2, p["ffn2_norm"], p["ffn2_wg"], p["ffn2_wu"], p["ffn2_wd"], p["final_norm"], final=True)


def _prepare_params(ffn1_norm, ffn1_w_gate, ffn1_w_up, ffn1_w_down, mix_norm, w_in, conv_w, a_log_fwd, a_log_bwd,
                    dt_bias_fwd, dt_bias_bwd, gdn_out_norm, q_norm, k_norm, attn_out_norm, w_out, ffn2_norm,
                    ffn2_w_gate, ffn2_w_up, ffn2_w_down, final_norm):
    w_in0 = w_in[0]
    att0 = GDN_PROJ_W + GATE_COLS
    w_ab = jnp.pad(w_in0[:, GDN_PROJ_W:att0], ((0, 0), (0, LANES - GATE_COLS)))
    return dict(
        ffn1_norm=ffn1_norm[0].reshape(1, -1),
        ffn1_wg=ffn1_w_gate[0].astype(BF16), ffn1_wu=ffn1_w_up[0].astype(BF16), ffn1_wd=ffn1_w_down[0].astype(BF16),
        mix_norm=mix_norm[0].reshape(1, -1),
        w_in=w_in0[:, :GDN_PROJ_W].astype(BF16),
        w_qk=w_in0[:, att0:att0 + ATT_Q_W + ATT_KV_W].astype(BF16),
        w_vt=w_in0[:, att0 + ATT_Q_W + ATT_KV_W:].T.astype(BF16),
        w_ab=w_ab.astype(BF16),
        conv_w=jnp.pad(conv_w[0], ((0, SUBLANES - CONV_K), (0, 0))),
        alog=_pad_lanes(jnp.concatenate([a_log_fwd[0], a_log_bwd[0]])),
        dtb=_pad_lanes(jnp.concatenate([dt_bias_fwd[0], dt_bias_bwd[0]])),
        gdn_out_norm=gdn_out_norm[0].reshape(1, -1),
        q_norm=q_norm[0].reshape(1, -1), k_norm=k_norm[0].reshape(1, -1),
        attn_out_norm=attn_out_norm[0].reshape(1, -1),
        w_out=w_out[0].astype(BF16),
        ffn2_norm=ffn2_norm[0].reshape(1, -1),
        ffn2_wg=ffn2_w_gate[0].astype(BF16), ffn2_wu=ffn2_w_up[0].astype(BF16), ffn2_wd=ffn2_w_down[0].astype(BF16),
        final_norm=final_norm.reshape(1, -1),
    )


def kernel(x_prompt, x_sample, ffn1_norm, ffn1_w_gate, ffn1_w_up, ffn1_w_down, mix_norm, w_in, conv_w, a_log_fwd,
           a_log_bwd, dt_bias_fwd, dt_bias_bwd, gdn_out_norm, q_norm, k_norm, attn_out_norm, w_out, ffn2_norm,
           ffn2_w_gate, ffn2_w_up, ffn2_w_down, final_norm):
    assert x_prompt.shape[0] == 1 and x_sample.shape[0] == 1
    p = _prepare_params(ffn1_norm, ffn1_w_gate, ffn1_w_up, ffn1_w_down, mix_norm, w_in, conv_w, a_log_fwd,
                        a_log_bwd, dt_bias_fwd, dt_bias_bwd, gdn_out_norm, q_norm, k_norm, attn_out_norm, w_out,
                        ffn2_norm, ffn2_w_gate, ffn2_w_up, ffn2_w_down, final_norm)
    rope = _rope_tables(max(x_prompt.shape[1], x_sample.shape[1]))
    y_prompt = _encode(x_prompt[0], p, rope)
    y_sample = _encode(x_sample[0], p, rope)
    return (y_prompt[None], y_sample[None])
```

```python
import functools
import math

import jax
import jax.numpy as jnp
from jax import lax
from jax.experimental import pallas as pl
from jax.experimental.pallas import tpu as pltpu

D_MODEL = 2048
HEAD_DIM = 128
GDN_HEADS = 8
GDN_W = GDN_HEADS * HEAD_DIM
ATT_Q_HEADS = 8
ATT_KV_HEADS = 2
ATT_GROUP = ATT_Q_HEADS // ATT_KV_HEADS
ATT_Q_W = ATT_Q_HEADS * HEAD_DIM
ATT_KV_W = ATT_KV_HEADS * HEAD_DIM
GDN_PROJ_W = 4 * GDN_W
GATE_COLS = 4 * GDN_HEADS
D_FF = 5632
CONV_K = 5
GRID_W = 64
AXIS_DIM = HEAD_DIM // 2
ROPE_THETA = 10000.0
EPS = 1e-6

LANES = 128
SUBLANES = 8
SCAN_CHUNK = 128
PROJ_COL_TILE = 256
CONV_HALO = 16
INV_BASE_LEVELS = 2
INV_BASE = 2 ** INV_BASE_LEVELS
EXP2_SCALE = HEAD_DIM ** -0.5 * math.log2(math.e)
FLASH_ROW_BLOCK = 64
FLASH_SUM_ROWS = 16
FLASH_Q_BLOCK = 512
F32 = jnp.float32
BF16 = jnp.bfloat16

V7X_VMEM_BYTES = 64 * 1024 * 1024
_VMEM_LIMIT = V7X_VMEM_BYTES * 7 // 8


def _cparams(semantics):
    return pltpu.CompilerParams(dimension_semantics=semantics, vmem_limit_bytes=_VMEM_LIMIT)


def _rms(x, g):
    return x * lax.rsqrt(jnp.mean(x * x, axis=-1, keepdims=True) + EPS) * g


def _dot(a, b):
    return jnp.dot(a, b, preferred_element_type=F32)


def _dot_nt(a, b):
    return lax.dot_general(a, b, (((1,), (1,)), ((), ())), preferred_element_type=F32)


def _split3(x):
    hi = x.astype(BF16)
    r = x - hi.astype(F32)
    mid = r.astype(BF16)
    lo = (r - mid.astype(F32)).astype(BF16)
    return hi, mid, lo


def _ffn_kernel(x_ref, g_ref, wg_ref, wu_ref, wd_ref, fg_ref, o_ref, xn_ref, *, final):
    j = pl.program_id(1)
    last = pl.num_programs(1) - 1

    def hidden_tile():
        xn = xn_ref[...]
        gate = _dot(xn, wg_ref[...])
        up = _dot(xn, wu_ref[...])
        act = (gate * jax.nn.sigmoid(gate) * up).astype(BF16)
        return _dot(act, wd_ref[...])

    @pl.when(j == 0)
    def _():
        xn_ref[...] = _rms(x_ref[...], g_ref[...]).astype(BF16)
        o_ref[...] = hidden_tile()

    @pl.when(jnp.logical_and(j > 0, j < last))
    def _():
        o_ref[...] += hidden_tile()

    @pl.when(j == last)
    def _():
        h = x_ref[...] + 0.5 * (o_ref[...] + hidden_tile())
        if final:
            h = _rms(h, fg_ref[...])
        o_ref[...] = h


def _ffn(x, norm_g, wg, wu, wd, final_g, *, final, tm=512, tf=512):
    L = x.shape[0]
    assert L % tm == 0 and D_FF % tf == 0
    grid = (L // tm, D_FF // tf)
    return pl.pallas_call(
        functools.partial(_ffn_kernel, final=final),
        grid=grid,
        in_specs=[
            pl.BlockSpec((tm, D_MODEL), lambda i, j: (i, 0)),
            pl.BlockSpec((1, D_MODEL), lambda i, j: (0, 0)),
            pl.BlockSpec((D_MODEL, tf), lambda i, j: (0, j)),
            pl.BlockSpec((D_MODEL, tf), lambda i, j: (0, j)),
            pl.BlockSpec((tf, D_MODEL), lambda i, j: (j, 0)),
            pl.BlockSpec((1, D_MODEL), lambda i, j: (0, 0)),
        ],
        out_specs=pl.BlockSpec((tm, D_MODEL), lambda i, j: (i, 0)),
        out_shape=jax.ShapeDtypeStruct((L, D_MODEL), F32),
        scratch_shapes=[pltpu.VMEM((tm, D_MODEL), BF16)],
        compiler_params=_cparams(("parallel", "arbitrary")),
        name="ffn",
    )(x, norm_g, wg, wu, wd, final_g)


def _rope(x, cos, sin_lo, sin_hi):
    return (x * cos + pltpu.roll(x, AXIS_DIM // 2, axis=1) * sin_hi
            + pltpu.roll(x, HEAD_DIM - AXIS_DIM // 2, axis=1) * sin_lo)


def _mix_proj_kernel(x_ref, xprev_ref, xnext_ref, g_ref, wg_ref, w_ref, wvt_ref, wab_ref, qn_ref, kn_ref,
                     rope_row_ref, rope_col_ref, alog_ref, dtb_ref, cw_ref, qg_ref, kg_ref, vg_ref, z_ref, q_ref,
                     k_ref, vt_ref, gcb_ref, gct_ref, ext_ref, *, tm):
    i = pl.program_id(0)
    pad = CONV_K // 2
    g = g_ref[...]
    xn = _rms(x_ref[...], g).astype(BF16)

    xn_ext = jnp.concatenate([_rms(xprev_ref[...], g).astype(BF16), xn, _rms(xnext_ref[...], g).astype(BF16)], axis=0)
    keep_prev = jnp.where(i == 0, 0.0, 1.0)
    keep_next = jnp.where(i == pl.num_programs(0) - 1, 0.0, 1.0)
    def rope_table(n):
        rows = rope_row_ref[n]
        col = rope_col_ref[n]
        return jnp.concatenate([jnp.broadcast_to(rows[g:g + 1, :], (GRID_W, HEAD_DIM)) + col
                                for g in range(tm // GRID_W)], axis=0)

    cos, slo, shi = rope_table(0), rope_table(1), rope_table(2)

    def project_ext(c):
        cs = slice(c, c + PROJ_COL_TILE)
        e = _dot(xn_ext, wg_ref[:, cs])
        ext_ref[0:CONV_HALO, cs] = e[0:CONV_HALO] * keep_prev
        ext_ref[CONV_HALO:CONV_HALO + tm, cs] = e[CONV_HALO:CONV_HALO + tm]
        ext_ref[CONV_HALO + tm:, cs] = e[CONV_HALO + tm:] * keep_next

    def project_z(c):
        z_ref[:, c:c + PROJ_COL_TILE] = _dot(xn, wg_ref[:, 3 * GDN_W + c:3 * GDN_W + c + PROJ_COL_TILE])

    def project_attn_pair(h):
        pair = _dot(xn, w_ref[:, h * HEAD_DIM:(h + 2) * HEAD_DIM])
        for half in range(2):
            head = pair[:, half * HEAD_DIM:(half + 1) * HEAD_DIM]
            n = h + half
            if n < ATT_Q_HEADS:
                q_ref[:, n * HEAD_DIM:(n + 1) * HEAD_DIM] = (
                    _rope(_rms(head, qn_ref[...]), cos, slo, shi) * EXP2_SCALE).astype(BF16)
            else:
                n -= ATT_Q_HEADS
                k_ref[:, n * HEAD_DIM:(n + 1) * HEAD_DIM] = _rope(_rms(head, kn_ref[...]), cos, slo, shi).astype(BF16)

    def project_vt():
        vt_ref[...] = _dot_nt(wvt_ref[...], xn).astype(BF16)

    def conv_slab(s):
        cols = slice(s * HEAD_DIM, (s + 1) * HEAD_DIM)
        y = None
        for t in range(CONV_K):
            term = ext_ref[CONV_HALO - pad + t:CONV_HALO - pad + t + tm, cols] * cw_ref[t:t + 1, cols]
            y = term if y is None else y + term
        y = y * jax.nn.sigmoid(y)
        part, h = divmod(s, GDN_HEADS)
        hc = slice(h * HEAD_DIM, (h + 1) * HEAD_DIM)
        if part == 0:
            qg_ref[:, hc] = y * (lax.rsqrt(jnp.sum(y * y, axis=-1, keepdims=True) + EPS) * (HEAD_DIM ** -0.5))
        elif part == 1:
            kg_ref[:, hc] = y * lax.rsqrt(jnp.sum(y * y, axis=-1, keepdims=True) + EPS)
        else:
            vg_ref[:, hc] = y

    tiles = range(0, GDN_W, PROJ_COL_TILE)
    mxu_q = [functools.partial(project_ext, c) for c in tiles]
    mxu_k = [functools.partial(project_ext, GDN_W + c) for c in tiles]
    mxu_v = [functools.partial(project_ext, 2 * GDN_W + c) for c in tiles]
    mxu_z = [functools.partial(project_z, c) for c in tiles]
    mxu_att = [functools.partial(project_attn_pair, h) for h in range(0, ATT_Q_HEADS + ATT_KV_HEADS, 2)]
    mxu_att.append(project_vt)
    for piece in mxu_q:
        piece()
    n_att = len(mxu_att) // 2
    stages = [(mxu_k + mxu_att[:n_att], range(0, GDN_HEADS)),
              (mxu_v + mxu_att[n_att:] + mxu_z[:2], range(GDN_HEADS, 2 * GDN_HEADS)),
              (mxu_z[2:], range(2 * GDN_HEADS, 3 * GDN_HEADS))]
    for pieces, slabs in stages:
        for n in range(max(len(pieces), len(slabs))):
            if n < len(pieces):
                pieces[n]()
            if n < len(slabs):
                conv_slab(slabs[n])

    ab = _dot(xn, wab_ref[...])
    t = ab + dtb_ref[...]
    softplus = jnp.maximum(t, 0.0) + jnp.log1p(jnp.exp(-jnp.abs(t)))
    log_decay = -jnp.exp(alog_ref[...]) * softplus
    lane = lax.broadcasted_iota(jnp.int32, ab.shape, 1)
    gb = jnp.where(lane < 2 * GDN_HEADS, log_decay, jax.nn.sigmoid(ab))

    r = lax.broadcasted_iota(jnp.int32, (tm, tm), 0)
    c = lax.broadcasted_iota(jnp.int32, (tm, tm), 1)
    same = (r // SCAN_CHUNK) == (c // SCAN_CHUNK)
    m_lo = jnp.where(same & (c <= r), 1.0, 0.0).astype(BF16)
    m_up = jnp.where(same & (c >= r), 1.0, 0.0).astype(BF16)
    parts = _split3(gb)
    gc_f = _dot(m_lo, parts[0]) + _dot(m_lo, parts[1]) + _dot(m_lo, parts[2])
    gc_b = _dot(m_up, parts[0]) + _dot(m_up, parts[1]) + _dot(m_up, parts[2])
    gcb = jnp.where(lane < GDN_HEADS, gc_f, jnp.where(lane < 2 * GDN_HEADS, gc_b, gb))
    gcb_ref[...] = gcb
    for n in range(tm // SCAN_CHUNK):
        gct = gcb[n * SCAN_CHUNK:(n + 1) * SCAN_CHUNK, :].T
        gct_ref[n * 2 * GDN_HEADS:(n + 1) * 2 * GDN_HEADS, :] = gct[0:2 * GDN_HEADS, :]


def _mix_proj(x, norm_g, w_in, w_qk, w_vt, w_ab, q_norm, k_norm, rope_row, rope_col, alog, dtb, conv_w8, *, tm=256):
    L = x.shape[0]
    assert L % tm == 0 and tm % SCAN_CHUNK == 0 and tm % CONV_HALO == 0 and tm % GRID_W == 0
    n_halo = L // CONV_HALO
    per = tm // CONV_HALO
    rows_per_tile = tm // GRID_W
    assert rows_per_tile <= SUBLANES
    rope_row = rope_row[:, :L // GRID_W].reshape(3, L // tm, rows_per_tile, HEAD_DIM)
    rope_row = jnp.pad(rope_row, ((0, 0), (0, 0), (0, SUBLANES - rows_per_tile), (0, 0)))
    row = lambda i: (i, 0)
    fixed = lambda i: (0, 0)
    resident = pl.Buffered(1)
    return pl.pallas_call(
        functools.partial(_mix_proj_kernel, tm=tm),
        grid=(L // tm,),
        in_specs=[
            pl.BlockSpec((tm, D_MODEL), row),
            pl.BlockSpec((CONV_HALO, D_MODEL), lambda i: (jnp.maximum(i * per - 1, 0), 0)),
            pl.BlockSpec((CONV_HALO, D_MODEL), lambda i: (jnp.minimum((i + 1) * per, n_halo - 1), 0)),
            pl.BlockSpec((1, D_MODEL), fixed),
            pl.BlockSpec((D_MODEL, GDN_PROJ_W), fixed, pipeline_mode=resident),
            pl.BlockSpec((D_MODEL, ATT_Q_W + ATT_KV_W), fixed, pipeline_mode=resident),
            pl.BlockSpec((ATT_KV_W, D_MODEL), fixed, pipeline_mode=resident),
            pl.BlockSpec((D_MODEL, LANES), fixed),
            pl.BlockSpec((1, HEAD_DIM), fixed),
            pl.BlockSpec((1, HEAD_DIM), fixed),
            pl.BlockSpec((3, None, SUBLANES, HEAD_DIM), lambda i: (0, i, 0, 0)),
            pl.BlockSpec((3, GRID_W, HEAD_DIM), lambda i: (0, 0, 0)),
            pl.BlockSpec((1, LANES), fixed),
            pl.BlockSpec((1, LANES), fixed),
            pl.BlockSpec((SUBLANES, 3 * GDN_W), fixed),
        ],
        out_specs=[
            pl.BlockSpec((tm, GDN_W), row),
            pl.BlockSpec((tm, GDN_W), row),
            pl.BlockSpec((tm, GDN_W), row),
            pl.BlockSpec((tm, GDN_W), row),
            pl.BlockSpec((tm, ATT_Q_W), row),
            pl.BlockSpec((tm, ATT_KV_W), row),
            pl.BlockSpec((ATT_KV_W, tm), lambda i: (0, i)),
            pl.BlockSpec((tm, LANES), row),
            pl.BlockSpec((tm // SCAN_CHUNK * 2 * GDN_HEADS, SCAN_CHUNK), row),
        ],
        out_shape=[
            jax.ShapeDtypeStruct((L, GDN_W), F32),
            jax.ShapeDtypeStruct((L, GDN_W), F32),
            jax.ShapeDtypeStruct((L, GDN_W), F32),
            jax.ShapeDtypeStruct((L, GDN_W), F32),
            jax.ShapeDtypeStruct((L, ATT_Q_W), BF16),
            jax.ShapeDtypeStruct((L, ATT_KV_W), BF16),
            jax.ShapeDtypeStruct((ATT_KV_W, L), BF16),
            jax.ShapeDtypeStruct((L, LANES), F32),
            jax.ShapeDtypeStruct((L // SCAN_CHUNK * 2 * GDN_HEADS, SCAN_CHUNK), F32),
        ],
        scratch_shapes=[pltpu.VMEM((tm + 2 * CONV_HALO, 3 * GDN_W), F32)],
        compiler_params=_cparams(("parallel",)),
        name="mix_proj",
    )(x, x, x, norm_g, w_in, w_qk, w_vt, w_ab, q_norm, k_norm, rope_row, rope_col, alog, dtb, conv_w8)


def _gdn_scan_kernel(qf_ref, kf_ref, vf_ref, gcbf_ref, gctf_ref, qb_ref, kb_ref, vb_ref, gcbb_ref, gctb_ref,
                     of_ref, ob_ref, s_ref):
    C = SCAN_CHUNK

    @pl.when(pl.program_id(0) == 0)
    def _():
        s_ref[...] = jnp.zeros_like(s_ref)

    r = lax.broadcasted_iota(jnp.int32, (C, C), 0)
    c = lax.broadcasted_iota(jnp.int32, (C, C), 1)
    eye = jnp.where(r == c, 1.0, 0.0)

    chains = []
    for fwd, (q_ref, k_ref, v_ref, gcb_ref, gct_ref, o_ref) in (
            (True, (qf_ref, kf_ref, vf_ref, gcbf_ref, gctf_ref, of_ref)),
            (False, (qb_ref, kb_ref, vb_ref, gcbb_ref, gctb_ref, ob_ref))):
        dif = r - c if fwd else c - r
        lane0 = 0 if fwd else GDN_HEADS
        gcb = gcb_ref[...]
        gc = gcb[:, lane0:lane0 + GDN_HEADS]
        beta = gcb[:, 2 * GDN_HEADS + lane0:3 * GDN_HEADS + lane0]
        g_end = gc[C - 1:C, :] if fwd else gc[0:1, :]
        shared = dict(q_ref=q_ref, k_ref=k_ref, v_ref=v_ref, o_ref=o_ref, m_incl=dif >= 0, m_strict=dif > 0,
                      gc=gc, beta=beta, e_gc=jnp.exp(gc), e_rest=jnp.exp(g_end - gc), e_end=jnp.exp(g_end),
                      gct=gct_ref[...])
        for h in range(GDN_HEADS):
            chains.append(dict(shared, cols=slice(h * HEAD_DIM, (h + 1) * HEAD_DIM), col1=slice(h, h + 1),
                               state=lane0 + h))
    heads = range(len(chains))
    cols = [ch["cols"] for ch in chains]
    col1 = [ch["col1"] for ch in chains]
    beta = [ch["beta"] for ch in chains]
    e_gc = [ch["e_gc"] for ch in chains]

    k = [chains[h]["k_ref"][:, cols[h]] for h in heads]
    k16 = [k[h].astype(BF16) for h in heads]
    kb = [k[h] * beta[h][:, col1[h]] for h in heads]
    q16 = [chains[h]["q_ref"][:, cols[h]].astype(BF16) for h in heads]
    kq = [_dot_nt(jnp.concatenate([kb[h].astype(BF16), q16[h]], axis=0), k16[h]) for h in heads]
    decay = []
    for ch in chains:
        diff = ch["gc"][:, ch["col1"]] - ch["gct"][ch["col1"], :]
        decay.append(jnp.where(ch["m_incl"], jnp.exp(jnp.where(ch["m_incl"], diff, 0.0)), 0.0))
    a = [jnp.where(chains[h]["m_strict"], kq[h][0:C] * decay[h], 0.0) for h in heads]
    attn_qk16 = [(kq[h][C:2 * C] * decay[h]).astype(BF16) for h in heads]

    rb = r >> INV_BASE_LEVELS
    cb = c >> INV_BASE_LEVELS
    a_d = [jnp.where(rb == cb, a[h], 0.0) for h in heads]
    inv = [eye - a_d[h] for h in heads]
    a_d16 = [a_d[h].astype(BF16) for h in heads]
    a_pow16 = [_dot(a_d16[h], a_d16[h]).astype(BF16) for h in heads]
    for level in range(INV_BASE_LEVELS - 1):
        if level < INV_BASE_LEVELS - 2:
            both = [_dot(jnp.concatenate([inv[h].astype(BF16), a_pow16[h]], axis=0), a_pow16[h]) for h in heads]
            inv = [inv[h] + both[h][0:C] for h in heads]
            a_pow16 = [both[h][C:2 * C].astype(BF16) for h in heads]
        else:
            inv = [inv[h] + _dot(inv[h].astype(BF16), a_pow16[h]) for h in heads]
    b = INV_BASE
    while b < C:
        off = ((rb >> 1) == (cb >> 1)) & (rb != cb)
        a_off16 = [jnp.where(off, a[h], 0.0).astype(BF16) for h in heads]
        inv16 = [inv[h].astype(BF16) for h in heads]
        left = [_dot(inv16[h], a_off16[h]).astype(BF16) for h in heads]
        inv = [inv[h] - _dot(left[h], inv16[h]) for h in heads]
        rb = rb >> 1
        cb = cb >> 1
        b *= 2

    rhs16 = [jnp.concatenate([(chains[h]["v_ref"][:, cols[h]] * beta[h][:, col1[h]]).astype(BF16),
                              (kb[h] * e_gc[h][:, col1[h]]).astype(BF16)], axis=1) for h in heads]
    uw = [_dot(inv[h].astype(BF16), rhs16[h]) for h in heads]
    q_dec16 = [(chains[h]["q_ref"][:, cols[h]] * e_gc[h][:, col1[h]]).astype(BF16) for h in heads]
    k_dec_t16 = [(k[h] * chains[h]["e_rest"][:, col1[h]]).T.astype(BF16) for h in heads]

    s = [s_ref[ch["state"]] for ch in chains]
    s16 = [s[h].astype(BF16) for h in heads]
    ws = [_dot(jnp.concatenate([uw[h][:, HEAD_DIM:].astype(BF16), q_dec16[h]], axis=0), s16[h]) for h in heads]
    v_new16 = [(uw[h][:, 0:HEAD_DIM] - ws[h][0:C]).astype(BF16) for h in heads]
    for h, ch in enumerate(chains):
        ch["o_ref"][:, cols[h]] = ws[h][C:2 * C] + _dot(attn_qk16[h], v_new16[h])
    for h, ch in enumerate(chains):
        s_ref[ch["state"]] = s[h] * ch["e_end"][:, col1[h]] + _dot(k_dec_t16[h], v_new16[h])


def _gdn_scan(q, k, v, gcb, gct):
    L = q.shape[0]
    assert L % SCAN_CHUNK == 0
    n = L // SCAN_CHUNK
    fwd = lambda i: (i, 0)
    bwd = lambda i: (n - 1 - i, 0)
    tile_f = pl.BlockSpec((SCAN_CHUNK, GDN_W), fwd)
    tile_b = pl.BlockSpec((SCAN_CHUNK, GDN_W), bwd)
    return pl.pallas_call(
        _gdn_scan_kernel,
        grid=(n,),
        in_specs=[
            tile_f, tile_f, tile_f,
            pl.BlockSpec((SCAN_CHUNK, LANES), fwd),
            pl.BlockSpec((GDN_HEADS, SCAN_CHUNK), lambda i: (2 * i, 0)),
            tile_b, tile_b, tile_b,
            pl.BlockSpec((SCAN_CHUNK, LANES), bwd),
            pl.BlockSpec((GDN_HEADS, SCAN_CHUNK), lambda i: (2 * (n - 1 - i) + 1, 0)),
        ],
        out_specs=[tile_f, tile_b],
        out_shape=[jax.ShapeDtypeStruct((L, GDN_W), F32), jax.ShapeDtypeStruct((L, GDN_W), F32)],
        scratch_shapes=[pltpu.VMEM((2 * GDN_HEADS, HEAD_DIM, HEAD_DIM), F32)],
        compiler_params=_cparams(("arbitrary",)),
        name="gdn_scan",
    )(q, k, v, gcb, gct, q, k, v, gcb, gct)


def _flash_kernel(q_ref, k_ref, vt_ref, on_ref, o_ref, m_ref, acc_ref, st_ref, p_ref, *, tk):
    ki = pl.program_id(1)

    @pl.when(ki == 0)
    def _():
        m_ref[...] = jnp.full_like(m_ref, -jnp.inf)
        acc_ref[...] = jnp.zeros_like(acc_ref)

    ones = jnp.ones((FLASH_SUM_ROWS, tk), BF16)
    k = [k_ref[:, g * HEAD_DIM:(g + 1) * HEAD_DIM] for g in range(ATT_KV_HEADS)]
    vt1 = [jnp.concatenate([vt_ref[g * HEAD_DIM:(g + 1) * HEAD_DIM, :], ones], axis=0) for g in range(ATT_KV_HEADS)]
    blocks = [slice(r, r + FLASH_ROW_BLOCK) for r in range(0, tk, FLASH_ROW_BLOCK)]

    tq = q_ref.shape[0]
    units = [(h, slice(c, c + FLASH_Q_BLOCK)) for h in range(ATT_Q_HEADS) for c in range(0, tq, FLASH_Q_BLOCK)]

    def scores(h, qs):
        st_ref[h, :, qs] = _dot_nt(k[h // ATT_GROUP], q_ref[qs, h * HEAD_DIM:(h + 1) * HEAD_DIM])

    def softmax(h, qs):
        part = None
        for rows in blocks:
            x = st_ref[h, rows, qs]
            while x.shape[0] > SUBLANES:
                half = x.shape[0] // 2
                x = jnp.maximum(x[:half], x[half:])
            part = x if part is None else jnp.maximum(part, x)
        m_prev = m_ref[h, :, qs]
        m_new = jnp.maximum(m_prev, jnp.max(part, axis=0, keepdims=True))
        m_ref[h, :, qs] = m_new
        for rows in blocks:
            p_ref[h, rows, qs] = jnp.exp2(st_ref[h, rows, qs] - m_new).astype(BF16)
        return jnp.exp2(m_prev - m_new)

    def accumulate(h, qs, alpha):
        acc_ref[h, :, qs] = alpha * acc_ref[h, :, qs] + _dot(vt1[h // ATT_GROUP], p_ref[h, :, qs])

    scores(*units[0])
    pending = None
    for u, unit in enumerate(units):
        if u + 1 < len(units):
            scores(*units[u + 1])
        alpha = softmax(*unit)
        if pending is not None:
            accumulate(*pending)
        pending = (*unit, alpha)
    accumulate(*pending)

    @pl.when(ki == pl.num_programs(1) - 1)
    def _():
        for h in range(ATT_Q_HEADS):
            cols = slice(h * HEAD_DIM, (h + 1) * HEAD_DIM)
            o = (acc_ref[h, 0:HEAD_DIM, :] / acc_ref[h, HEAD_DIM:HEAD_DIM + 1, :]).T
            o_ref[:, cols] = _rms(o, on_ref[...]).astype(BF16)


def _flash_attn(q, k, vt, out_norm, *, tq=512, tk=1024):
    L = q.shape[0]
    assert L % tq == 0 and L % tk == 0 and tq % FLASH_Q_BLOCK == 0 and tk % FLASH_ROW_BLOCK == 0
    return pl.pallas_call(
        functools.partial(_flash_kernel, tk=tk),
        grid=(L // tq, L // tk),
        in_specs=[
            pl.BlockSpec((tq, ATT_Q_W), lambda i, j: (i, 0)),
            pl.BlockSpec((tk, ATT_KV_W), lambda i, j: (j, 0)),
            pl.BlockSpec((ATT_KV_W, tk), lambda i, j: (0, j)),
            pl.BlockSpec((1, HEAD_DIM), lambda i, j: (0, 0)),
        ],
        out_specs=pl.BlockSpec((tq, ATT_Q_W), lambda i, j: (i, 0)),
        out_shape=jax.ShapeDtypeStruct((L, ATT_Q_W), BF16),
        scratch_shapes=[
            pltpu.VMEM((ATT_Q_HEADS, 1, tq), F32),
            pltpu.VMEM((ATT_Q_HEADS, HEAD_DIM + FLASH_SUM_ROWS, tq), F32),
            pltpu.VMEM((ATT_Q_HEADS, tk, tq), F32),
            pltpu.VMEM((ATT_Q_HEADS, tk, tq), BF16),
        ],
        compiler_params=_cparams(("parallel", "arbitrary")),
        name="flash_attn",
    )(q, k, vt, out_norm)


def _out_proj_kernel(of_ref, ob_ref, z_ref, oa_ref, h_ref, gn_ref, w_ref, o_ref, mix_ref):
    att = _dot(oa_ref[...], w_ref[GDN_W:, :])
    for h in range(GDN_HEADS):
        cols = slice(h * HEAD_DIM, (h + 1) * HEAD_DIM)
        o = _rms(of_ref[:, cols] + ob_ref[:, cols], gn_ref[...])
        z = z_ref[:, cols]
        mix_ref[:, cols] = (o * (z * jax.nn.sigmoid(z))).astype(BF16)
    o_ref[...] = h_ref[...] + (att + _dot(mix_ref[...], w_ref[0:GDN_W, :]))


def _out_proj(o_fwd, o_bwd, z, oa, h1, gdn_norm, w_out, *, tm=512):
    L = h1.shape[0]
    assert L % tm == 0
    row = lambda i: (i, 0)
    return pl.pallas_call(
        _out_proj_kernel,
        grid=(L // tm,),
        in_specs=[
            pl.BlockSpec((tm, GDN_W), row),
            pl.BlockSpec((tm, GDN_W), row),
            pl.BlockSpec((tm, GDN_W), row),
            pl.BlockSpec((tm, ATT_Q_W), row),
            pl.BlockSpec((tm, D_MODEL), row),
            pl.BlockSpec((1, HEAD_DIM), lambda i: (0, 0)),
            pl.BlockSpec((GDN_W + ATT_Q_W, D_MODEL), lambda i: (0, 0)),
        ],
        out_specs=pl.BlockSpec((tm, D_MODEL), row),
        out_shape=jax.ShapeDtypeStruct((L, D_MODEL), F32),
        scratch_shapes=[pltpu.VMEM((tm, GDN_W), BF16)],
        compiler_params=_cparams(("parallel",)),
        name="out_proj",
    )(o_fwd, o_bwd, z, oa, h1, gdn_norm, w_out)


def _rope_tables(L):
    assert L % GRID_W == 0
    rows = L // GRID_W
    freqs = ROPE_THETA ** (-jnp.arange(0, AXIS_DIM, 2, dtype=F32) / AXIS_DIM)
    ang_r = jnp.arange(rows, dtype=F32)[:, None] * freqs[None, :]
    ang_c = jnp.arange(GRID_W, dtype=F32)[:, None] * freqs[None, :]
    cos_r, sin_r, cos_c, sin_c = jnp.cos(ang_r), jnp.sin(ang_r), jnp.cos(ang_c), jnp.sin(ang_c)
    zr, zc = jnp.zeros_like(cos_r), jnp.zeros_like(cos_c)
    row_part = jnp.stack([jnp.concatenate([cos_r, cos_r, zr, zr], axis=-1),
                          jnp.concatenate([-sin_r, zr, zr, zr], axis=-1),
                          jnp.concatenate([zr, sin_r, zr, zr], axis=-1)])
    col_part = jnp.stack([jnp.concatenate([zc, zc, cos_c, cos_c], axis=-1),
                          jnp.concatenate([zc, zc, -sin_c, zc], axis=-1),
                          jnp.concatenate([zc, zc, zc, sin_c], axis=-1)])
    return row_part, col_part


def _pad_lanes(x):
    x = x.reshape(1, -1)
    return jnp.pad(x, ((0, 0), (0, LANES - x.shape[1])))


def _encode(x, p, rope):
    h1 = _ffn(x, p["ffn1_norm"], p["ffn1_wg"], p["ffn1_wu"], p["ffn1_wd"], p["final_norm"], final=False)
    rope_row, rope_col = rope
    qg, kg, vg, z, qa, ka, va, gcb, gct = _mix_proj(h1, p["mix_norm"], p["w_in"], p["w_qk"], p["w_vt"], p["w_ab"],
                                                    p["q_norm"], p["k_norm"], rope_row, rope_col, p["alog"],
                                                    p["dtb"], p["conv_w"])
    o_fwd, o_bwd = _gdn_scan(qg, kg, vg, gcb, gct)
    oa = _flash_attn(qa, ka, va, p["attn_out_norm"])
    h2 = _out_proj(o_fwd, o_bwd, z, oa, h1, p["gdn_out_norm"], p["w_out"])
    return _ffn(h2, p["ffn2_norm"], p["ffn2_wg"], p["ffn2_wu"], p["ffn2_wd"], p["final_norm"], final=True)


def _prepare_params(ffn1_norm, ffn1_w_gate, ffn1_w_up, ffn1_w_down, mix_norm, w_in, conv_w, a_log_fwd, a_log_bwd,
                    dt_bias_fwd, dt_bias_bwd, gdn_out_norm, q_norm, k_norm, attn_out_norm, w_out, ffn2_norm,
                    ffn2_w_gate, ffn2_w_up, ffn2_w_down, final_norm):
    w_in0 = w_in[0]
    att0 = GDN_PROJ_W + GATE_COLS
    w_ab = jnp.pad(w_in0[:, GDN_PROJ_W:att0], ((0, 0), (0, LANES - GATE_COLS)))
    return dict(
        ffn1_norm=ffn1_norm[0].reshape(1, -1),
        ffn1_wg=ffn1_w_gate[0].astype(BF16), ffn1_wu=ffn1_w_up[0].astype(BF16), ffn1_wd=ffn1_w_down[0].astype(BF16),
        mix_norm=mix_norm[0].reshape(1, -1),
        w_in=w_in0[:, :GDN_PROJ_W].astype(BF16),
        w_qk=w_in0[:, att0:att0 + ATT_Q_W + ATT_KV_W].astype(BF16),
        w_vt=w_in0[:, att0 + ATT_Q_W + ATT_KV_W:].T.astype(BF16),
        w_ab=w_ab.astype(BF16),
        conv_w=jnp.pad(conv_w[0], ((0, SUBLANES - CONV_K), (0, 0))),
        alog=_pad_lanes(jnp.concatenate([a_log_fwd[0], a_log_bwd[0]])),
        dtb=_pad_lanes(jnp.concatenate([dt_bias_fwd[0], dt_bias_bwd[0]])),
        gdn_out_norm=gdn_out_norm[0].reshape(1, -1),
        q_norm=q_norm[0].reshape(1, -1), k_norm=k_norm[0].reshape(1, -1),
        attn_out_norm=attn_out_norm[0].reshape(1, -1),
        w_out=w_out[0].astype(BF16),
        ffn2_norm=ffn2_norm[0].reshape(1, -1),
        ffn2_wg=ffn2_w_gate[0].astype(BF16), ffn2_wu=ffn2_w_up[0].astype(BF16), ffn2_wd=ffn2_w_down[0].astype(BF16),
        final_norm=final_norm.reshape(1, -1),
    )


def kernel(x_prompt, x_sample, ffn1_norm, ffn1_w_gate, ffn1_w_up, ffn1_w_down, mix_norm, w_in, conv_w, a_log_fwd,
           a_log_bwd, dt_bias_fwd, dt_bias_bwd, gdn_out_norm, q_norm, k_norm, attn_out_norm, w_out, ffn2_norm,
           ffn2_w_gate, ffn2_w_up, ffn2_w_down, final_norm):
    assert x_prompt.shape[0] == 1 and x_sample.shape[0] == 1
    p = _prepare_params(ffn1_norm, ffn1_w_gate, ffn1_w_up, ffn1_w_down, mix_norm, w_in, conv_w, a_log_fwd,
                        a_log_bwd, dt_bias_fwd, dt_bias_bwd, gdn_out_norm, q_norm, k_norm, attn_out_norm, w_out,
                        ffn2_norm, ffn2_w_gate, ffn2_w_up, ffn2_w_down, final_norm)
    rope = _rope_tables(max(x_prompt.shape[1], x_sample.shape[1]))
    y_prompt = _encode(x_prompt[0], p, rope)
    y_sample = _encode(x_sample[0], p, rope)
    return (y_prompt[None], y_sample[None])
```

```python
import functools
import math

import jax
import jax.numpy as jnp
from jax import lax
from jax.experimental import pallas as pl
from jax.experimental.pallas import tpu as pltpu

D_MODEL = 2048
HEAD_DIM = 128
GDN_HEADS = 8
GDN_W = GDN_HEADS * HEAD_DIM
ATT_Q_HEADS = 8
ATT_KV_HEADS = 2
ATT_GROUP = ATT_Q_HEADS // ATT_KV_HEADS
ATT_Q_W = ATT_Q_HEADS * HEAD_DIM
ATT_KV_W = ATT_KV_HEADS * HEAD_DIM
GDN_PROJ_W = 4 * GDN_W
GATE_COLS = 4 * GDN_HEADS
D_FF = 5632
CONV_K = 5
GRID_W = 64
AXIS_DIM = HEAD_DIM // 2
ROPE_THETA = 10000.0
EPS = 1e-6

LANES = 128
SUBLANES = 8
SCAN_CHUNK = 128
PROJ_COL_TILE = 256
CONV_HALO = 16
INV_BASE_LEVELS = 2
INV_BASE = 2 ** INV_BASE_LEVELS
EXP2_SCALE = HEAD_DIM ** -0.5 * math.log2(math.e)
FLASH_ROW_BLOCK = 64
FLASH_SUM_ROWS = 16
FLASH_Q_BLOCK = 512
F32 = jnp.float32
BF16 = jnp.bfloat16

V7X_VMEM_BYTES = 64 * 1024 * 1024
_VMEM_LIMIT = V7X_VMEM_BYTES * 7 // 8


def _cparams(semantics):
    return pltpu.CompilerParams(dimension_semantics=semantics, vmem_limit_bytes=_VMEM_LIMIT)


def _rms(x, g):
    return x * lax.rsqrt(jnp.mean(x * x, axis=-1, keepdims=True) + EPS) * g


def _dot(a, b):
    return jnp.dot(a, b, preferred_element_type=F32)


def _dot_nt(a, b):
    return lax.dot_general(a, b, (((1,), (1,)), ((), ())), preferred_element_type=F32)


def _split3(x):
    hi = x.astype(BF16)
    r = x - hi.astype(F32)
    mid = r.astype(BF16)
    lo = (r - mid.astype(F32)).astype(BF16)
    return hi, mid, lo


def _ffn_kernel(x_ref, g_ref, wg_ref, wu_ref, wd_ref, fg_ref, o_ref, xn_ref, *, final):
    j = pl.program_id(1)
    last = pl.num_programs(1) - 1

    def hidden_tile():
        xn = xn_ref[...]
        gate = _dot(xn, wg_ref[...])
        up = _dot(xn, wu_ref[...])
        act = (gate * jax.nn.sigmoid(gate) * up).astype(BF16)
        return _dot(act, wd_ref[...].astype(BF16))

    @pl.when(j == 0)
    def _():
        xn_ref[...] = _rms(x_ref[...], g_ref[...]).astype(BF16)
        o_ref[...] = hidden_tile()

    @pl.when(jnp.logical_and(j > 0, j < last))
    def _():
        o_ref[...] += hidden_tile()

    @pl.when(j == last)
    def _():
        h = x_ref[...] + 0.5 * (o_ref[...] + hidden_tile())
        if final:
            h = _rms(h, fg_ref[...])
        o_ref[...] = h


def _ffn(x, norm_g, wg, wu, wd, final_g, *, final, tm=512, tf=512):
    L = x.shape[0]
    assert L % tm == 0 and D_FF % tf == 0
    grid = (L // tm, D_FF // tf)
    return pl.pallas_call(
        functools.partial(_ffn_kernel, final=final),
        grid=grid,
        in_specs=[
            pl.BlockSpec((tm, D_MODEL), lambda i, j: (i, 0)),
            pl.BlockSpec((1, D_MODEL), lambda i, j: (0, 0)),
            pl.BlockSpec((D_MODEL, tf), lambda i, j: (0, j)),
            pl.BlockSpec((D_MODEL, tf), lambda i, j: (0, j)),
            pl.BlockSpec((tf, D_MODEL), lambda i, j: (j, 0)),
            pl.BlockSpec((1, D_MODEL), lambda i, j: (0, 0)),
        ],
        out_specs=pl.BlockSpec((tm, D_MODEL), lambda i, j: (i, 0)),
        out_shape=jax.ShapeDtypeStruct((L, D_MODEL), F32),
        scratch_shapes=[pltpu.VMEM((tm, D_MODEL), BF16)],
        compiler_params=_cparams(("parallel", "arbitrary")),
        name="ffn",
    )(x, norm_g, wg, wu, wd, final_g)


def _rope(x, cos, sin_lo, sin_hi):
    return (x * cos + pltpu.roll(x, AXIS_DIM // 2, axis=1) * sin_hi
            + pltpu.roll(x, HEAD_DIM - AXIS_DIM // 2, axis=1) * sin_lo)


def _mix_proj_kernel(x_ref, xprev_ref, xnext_ref, g_ref, wg_ref, w_ref, wvt_ref, wab_ref, qn_ref, kn_ref,
                     rope_row_ref, rope_col_ref, alog_ref, dtb_ref, cw_ref, qg_ref, kg_ref, vg_ref, z_ref, q_ref,
                     k_ref, vt_ref, gcb_ref, gct_ref, ext_ref, *, tm):
    i = pl.program_id(0)
    pad = CONV_K // 2
    g = g_ref[...]
    xn = _rms(x_ref[...], g).astype(BF16)

    xn_ext = jnp.concatenate([_rms(xprev_ref[...], g).astype(BF16), xn, _rms(xnext_ref[...], g).astype(BF16)], axis=0)
    keep_prev = jnp.where(i == 0, 0.0, 1.0)
    keep_next = jnp.where(i == pl.num_programs(0) - 1, 0.0, 1.0)
    def rope_table(n):
        rows = rope_row_ref[n]
        col = rope_col_ref[n]
        return jnp.concatenate([jnp.broadcast_to(rows[g:g + 1, :], (GRID_W, HEAD_DIM)) + col
                                for g in range(tm // GRID_W)], axis=0)

    cos, slo, shi = rope_table(0), rope_table(1), rope_table(2)

    def project_ext(c):
        cs = slice(c, c + PROJ_COL_TILE)
        e = _dot(xn_ext, wg_ref[:, cs])
        ext_ref[0:CONV_HALO, cs] = e[0:CONV_HALO] * keep_prev
        ext_ref[CONV_HALO:CONV_HALO + tm, cs] = e[CONV_HALO:CONV_HALO + tm]
        ext_ref[CONV_HALO + tm:, cs] = e[CONV_HALO + tm:] * keep_next

    def project_z(c):
        z_ref[:, c:c + PROJ_COL_TILE] = _dot(xn, wg_ref[:, 3 * GDN_W + c:3 * GDN_W + c + PROJ_COL_TILE])

    def project_attn_pair(h):
        pair = _dot(xn, w_ref[:, h * HEAD_DIM:(h + 2) * HEAD_DIM])
        for half in range(2):
            head = pair[:, half * HEAD_DIM:(half + 1) * HEAD_DIM]
            n = h + half
            if n < ATT_Q_HEADS:
                q_ref[:, n * HEAD_DIM:(n + 1) * HEAD_DIM] = (
                    _rope(_rms(head, qn_ref[...]), cos, slo, shi) * EXP2_SCALE).astype(BF16)
            else:
                n -= ATT_Q_HEADS
                k_ref[:, n * HEAD_DIM:(n + 1) * HEAD_DIM] = _rope(_rms(head, kn_ref[...]), cos, slo, shi).astype(BF16)

    def project_vt():
        vt_ref[...] = _dot_nt(wvt_ref[...], xn).astype(BF16)

    def conv_slab(s):
        cols = slice(s * HEAD_DIM, (s + 1) * HEAD_DIM)
        y = None
        for t in range(CONV_K):
            term = ext_ref[CONV_HALO - pad + t:CONV_HALO - pad + t + tm, cols] * cw_ref[t:t + 1, cols]
            y = term if y is None else y + term
        y = y * jax.nn.sigmoid(y)
        part, h = divmod(s, GDN_HEADS)
        hc = slice(h * HEAD_DIM, (h + 1) * HEAD_DIM)
        if part == 0:
            qg_ref[:, hc] = y * (lax.rsqrt(jnp.sum(y * y, axis=-1, keepdims=True) + EPS) * (HEAD_DIM ** -0.5))
        elif part == 1:
            kg_ref[:, hc] = y * lax.rsqrt(jnp.sum(y * y, axis=-1, keepdims=True) + EPS)
        else:
            vg_ref[:, hc] = y

    tiles = range(0, GDN_W, PROJ_COL_TILE)
    mxu_q = [functools.partial(project_ext, c) for c in tiles]
    mxu_k = [functools.partial(project_ext, GDN_W + c) for c in tiles]
    mxu_v = [functools.partial(project_ext, 2 * GDN_W + c) for c in tiles]
    mxu_z = [functools.partial(project_z, c) for c in tiles]
    mxu_att = [functools.partial(project_attn_pair, h) for h in range(0, ATT_Q_HEADS + ATT_KV_HEADS, 2)]
    mxu_att.append(project_vt)
    for piece in mxu_q:
        piece()
    n_att = len(mxu_att) // 2
    stages = [(mxu_k + mxu_att[:n_att], range(0, GDN_HEADS)),
              (mxu_v + mxu_att[n_att:] + mxu_z[:2], range(GDN_HEADS, 2 * GDN_HEADS)),
              (mxu_z[2:], range(2 * GDN_HEADS, 3 * GDN_HEADS))]
    for pieces, slabs in stages:
        for n in range(max(len(pieces), len(slabs))):
            if n < len(pieces):
                pieces[n]()
            if n < len(slabs):
                conv_slab(slabs[n])

    ab = _dot(xn, wab_ref[...])
    t = ab + dtb_ref[...]
    softplus = jnp.maximum(t, 0.0) + jnp.log1p(jnp.exp(-jnp.abs(t)))
    log_decay = -jnp.exp(alog_ref[...]) * softplus
    lane = lax.broadcasted_iota(jnp.int32, ab.shape, 1)
    gb = jnp.where(lane < 2 * GDN_HEADS, log_decay, jax.nn.sigmoid(ab))

    r = lax.broadcasted_iota(jnp.int32, (tm, tm), 0)
    c = lax.broadcasted_iota(jnp.int32, (tm, tm), 1)
    same = (r // SCAN_CHUNK) == (c // SCAN_CHUNK)
    m_lo = jnp.where(same & (c <= r), 1.0, 0.0).astype(BF16)
    m_up = jnp.where(same & (c >= r), 1.0, 0.0).astype(BF16)
    parts = _split3(gb)
    gc_f = _dot(m_lo, parts[0]) + _dot(m_lo, parts[1]) + _dot(m_lo, parts[2])
    gc_b = _dot(m_up, parts[0]) + _dot(m_up, parts[1]) + _dot(m_up, parts[2])
    gcb = jnp.where(lane < GDN_HEADS, gc_f, jnp.where(lane < 2 * GDN_HEADS, gc_b, gb))
    gcb_ref[...] = gcb
    for n in range(tm // SCAN_CHUNK):
        gct = gcb[n * SCAN_CHUNK:(n + 1) * SCAN_CHUNK, :].T
        gct_ref[n * 2 * GDN_HEADS:(n + 1) * 2 * GDN_HEADS, :] = gct[0:2 * GDN_HEADS, :]


def _mix_proj(x, norm_g, w_in, w_qk, w_vt, w_ab, q_norm, k_norm, rope_row, rope_col, alog, dtb, conv_w8, *, tm=256):
    L = x.shape[0]
    assert L % tm == 0 and tm % SCAN_CHUNK == 0 and tm % CONV_HALO == 0 and tm % GRID_W == 0
    n_halo = L // CONV_HALO
    per = tm // CONV_HALO
    rows_per_tile = tm // GRID_W
    assert rows_per_tile <= SUBLANES
    rope_row = rope_row[:, :L // GRID_W].reshape(3, L // tm, rows_per_tile, HEAD_DIM)
    rope_row = jnp.pad(rope_row, ((0, 0), (0, 0), (0, SUBLANES - rows_per_tile), (0, 0)))
    row = lambda i: (i, 0)
    fixed = lambda i: (0, 0)
    resident = pl.Buffered(1)
    return pl.pallas_call(
        functools.partial(_mix_proj_kernel, tm=tm),
        grid=(L // tm,),
        in_specs=[
            pl.BlockSpec((tm, D_MODEL), row),
            pl.BlockSpec((CONV_HALO, D_MODEL), lambda i: (jnp.maximum(i * per - 1, 0), 0)),
            pl.BlockSpec((CONV_HALO, D_MODEL), lambda i: (jnp.minimum((i + 1) * per, n_halo - 1), 0)),
            pl.BlockSpec((1, D_MODEL), fixed),
            pl.BlockSpec((D_MODEL, GDN_PROJ_W), fixed, pipeline_mode=resident),
            pl.BlockSpec((D_MODEL, ATT_Q_W + ATT_KV_W), fixed, pipeline_mode=resident),
            pl.BlockSpec((ATT_KV_W, D_MODEL), fixed, pipeline_mode=resident),
            pl.BlockSpec((D_MODEL, LANES), fixed),
            pl.BlockSpec((1, HEAD_DIM), fixed),
            pl.BlockSpec((1, HEAD_DIM), fixed),
            pl.BlockSpec((3, None, SUBLANES, HEAD_DIM), lambda i: (0, i, 0, 0)),
            pl.BlockSpec((3, GRID_W, HEAD_DIM), lambda i: (0, 0, 0)),
            pl.BlockSpec((1, LANES), fixed),
            pl.BlockSpec((1, LANES), fixed),
            pl.BlockSpec((SUBLANES, 3 * GDN_W), fixed),
        ],
        out_specs=[
            pl.BlockSpec((tm, GDN_W), row),
            pl.BlockSpec((tm, GDN_W), row),
            pl.BlockSpec((tm, GDN_W), row),
            pl.BlockSpec((tm, GDN_W), row),
            pl.BlockSpec((tm, ATT_Q_W), row),
            pl.BlockSpec((tm, ATT_KV_W), row),
            pl.BlockSpec((ATT_KV_W, tm), lambda i: (0, i)),
            pl.BlockSpec((tm, LANES), row),
            pl.BlockSpec((tm // SCAN_CHUNK * 2 * GDN_HEADS, SCAN_CHUNK), row),
        ],
        out_shape=[
            jax.ShapeDtypeStruct((L, GDN_W), F32),
            jax.ShapeDtypeStruct((L, GDN_W), F32),
            jax.ShapeDtypeStruct((L, GDN_W), F32),
            jax.ShapeDtypeStruct((L, GDN_W), F32),
            jax.ShapeDtypeStruct((L, ATT_Q_W), BF16),
            jax.ShapeDtypeStruct((L, ATT_KV_W), BF16),
            jax.ShapeDtypeStruct((ATT_KV_W, L), BF16),
            jax.ShapeDtypeStruct((L, LANES), F32),
            jax.ShapeDtypeStruct((L // SCAN_CHUNK * 2 * GDN_HEADS, SCAN_CHUNK), F32),
        ],
        scratch_shapes=[pltpu.VMEM((tm + 2 * CONV_HALO, 3 * GDN_W), F32)],
        compiler_params=_cparams(("parallel",)),
        name="mix_proj",
    )(x, x, x, norm_g, w_in, w_qk, w_vt, w_ab, q_norm, k_norm, rope_row, rope_col, alog, dtb, conv_w8)


def _gdn_scan_kernel(qf_ref, kf_ref, vf_ref, gcbf_ref, gctf_ref, qb_ref, kb_ref, vb_ref, gcbb_ref, gctb_ref,
                     of_ref, ob_ref, s_ref):
    C = SCAN_CHUNK

    @pl.when(pl.program_id(0) == 0)
    def _():
        s_ref[...] = jnp.zeros_like(s_ref)

    r = lax.broadcasted_iota(jnp.int32, (C, C), 0)
    c = lax.broadcasted_iota(jnp.int32, (C, C), 1)
    eye = jnp.where(r == c, 1.0, 0.0)

    chains = []
    for fwd, (q_ref, k_ref, v_ref, gcb_ref, gct_ref, o_ref) in (
            (True, (qf_ref, kf_ref, vf_ref, gcbf_ref, gctf_ref, of_ref)),
            (False, (qb_ref, kb_ref, vb_ref, gcbb_ref, gctb_ref, ob_ref))):
        dif = r - c if fwd else c - r
        lane0 = 0 if fwd else GDN_HEADS
        gcb = gcb_ref[...]
        gc = gcb[:, lane0:lane0 + GDN_HEADS]
        beta = gcb[:, 2 * GDN_HEADS + lane0:3 * GDN_HEADS + lane0]
        g_end = gc[C - 1:C, :] if fwd else gc[0:1, :]
        shared = dict(q_ref=q_ref, k_ref=k_ref, v_ref=v_ref, o_ref=o_ref, m_incl=dif >= 0, m_strict=dif > 0,
                      gc=gc, beta=beta, e_gc=jnp.exp(gc), e_rest=jnp.exp(g_end - gc), e_end=jnp.exp(g_end),
                      gct=gct_ref[...])
        for h in range(GDN_HEADS):
            chains.append(dict(shared, cols=slice(h * HEAD_DIM, (h + 1) * HEAD_DIM), col1=slice(h, h + 1),
                               state=lane0 + h))
    heads = range(len(chains))
    cols = [ch["cols"] for ch in chains]
    col1 = [ch["col1"] for ch in chains]
    beta = [ch["beta"] for ch in chains]
    e_gc = [ch["e_gc"] for ch in chains]

    k = [chains[h]["k_ref"][:, cols[h]] for h in heads]
    k16 = [k[h].astype(BF16) for h in heads]
    kb = [k[h] * beta[h][:, col1[h]] for h in heads]
    q16 = [chains[h]["q_ref"][:, cols[h]].astype(BF16) for h in heads]
    kq = [_dot_nt(jnp.concatenate([kb[h].astype(BF16), q16[h]], axis=0), k16[h]) for h in heads]
    decay = []
    for ch in chains:
        diff = ch["gc"][:, ch["col1"]] - ch["gct"][ch["col1"], :]
        decay.append(jnp.where(ch["m_incl"], jnp.exp(jnp.where(ch["m_incl"], diff, 0.0)), 0.0))
    a = [jnp.where(chains[h]["m_strict"], kq[h][0:C] * decay[h], 0.0) for h in heads]
    attn_qk16 = [(kq[h][C:2 * C] * decay[h]).astype(BF16) for h in heads]

    rb = r >> INV_BASE_LEVELS
    cb = c >> INV_BASE_LEVELS
    a_d = [jnp.where(rb == cb, a[h], 0.0) for h in heads]
    inv = [eye - a_d[h] for h in heads]
    a_d16 = [a_d[h].astype(BF16) for h in heads]
    a_pow16 = [_dot(a_d16[h], a_d16[h]).astype(BF16) for h in heads]
    for level in range(INV_BASE_LEVELS - 1):
        if level < INV_BASE_LEVELS - 2:
            both = [_dot(jnp.concatenate([inv[h].astype(BF16), a_pow16[h]], axis=0), a_pow16[h]) for h in heads]
            inv = [inv[h] + both[h][0:C] for h in heads]
            a_pow16 = [both[h][C:2 * C].astype(BF16) for h in heads]
        else:
            inv = [inv[h] + _dot(inv[h].astype(BF16), a_pow16[h]) for h in heads]
    b = INV_BASE
    while b < C:
        off = ((rb >> 1) == (cb >> 1)) & (rb != cb)
        a_off16 = [jnp.where(off, a[h], 0.0).astype(BF16) for h in heads]
        inv16 = [inv[h].astype(BF16) for h in heads]
        left = [_dot(inv16[h], a_off16[h]).astype(BF16) for h in heads]
        inv = [inv[h] - _dot(left[h], inv16[h]) for h in heads]
        rb = rb >> 1
        cb = cb >> 1
        b *= 2

    rhs16 = [jnp.concatenate([(chains[h]["v_ref"][:, cols[h]] * beta[h][:, col1[h]]).astype(BF16),
                              (kb[h] * e_gc[h][:, col1[h]]).astype(BF16)], axis=1) for h in heads]
    uw = [_dot(inv[h].astype(BF16), rhs16[h]) for h in heads]
    q_dec16 = [(chains[h]["q_ref"][:, cols[h]] * e_gc[h][:, col1[h]]).astype(BF16) for h in heads]
    k_dec_t16 = [(k[h] * chains[h]["e_rest"][:, col1[h]]).T.astype(BF16) for h in heads]

    s = [s_ref[ch["state"]] for ch in chains]
    s16 = [s[h].astype(BF16) for h in heads]
    ws = [_dot(jnp.concatenate([uw[h][:, HEAD_DIM:].astype(BF16), q_dec16[h]], axis=0), s16[h]) for h in heads]
    v_new16 = [(uw[h][:, 0:HEAD_DIM] - ws[h][0:C]).astype(BF16) for h in heads]
    for h, ch in enumerate(chains):
        ch["o_ref"][:, cols[h]] = ws[h][C:2 * C] + _dot(attn_qk16[h], v_new16[h])
    for h, ch in enumerate(chains):
        s_ref[ch["state"]] = s[h] * ch["e_end"][:, col1[h]] + _dot(k_dec_t16[h], v_new16[h])


def _gdn_scan(q, k, v, gcb, gct):
    L = q.shape[0]
    assert L % SCAN_CHUNK == 0
    n = L // SCAN_CHUNK
    fwd = lambda i: (i, 0)
    bwd = lambda i: (n - 1 - i, 0)
    tile_f = pl.BlockSpec((SCAN_CHUNK, GDN_W), fwd)
    tile_b = pl.BlockSpec((SCAN_CHUNK, GDN_W), bwd)
    return pl.pallas_call(
        _gdn_scan_kernel,
        grid=(n,),
        in_specs=[
            tile_f, tile_f, tile_f,
            pl.BlockSpec((SCAN_CHUNK, LANES), fwd),
            pl.BlockSpec((GDN_HEADS, SCAN_CHUNK), lambda i: (2 * i, 0)),
            tile_b, tile_b, tile_b,
            pl.BlockSpec((SCAN_CHUNK, LANES), bwd),
            pl.BlockSpec((GDN_HEADS, SCAN_CHUNK), lambda i: (2 * (n - 1 - i) + 1, 0)),
        ],
        out_specs=[tile_f, tile_b],
        out_shape=[jax.ShapeDtypeStruct((L, GDN_W), F32), jax.ShapeDtypeStruct((L, GDN_W), F32)],
        scratch_shapes=[pltpu.VMEM((2 * GDN_HEADS, HEAD_DIM, HEAD_DIM), F32)],
        compiler_params=_cparams(("arbitrary",)),
        name="gdn_scan",
    )(q, k, v, gcb, gct, q, k, v, gcb, gct)


def _flash_kernel(q_ref, k_ref, vt_ref, on_ref, o_ref, m_ref, acc_ref, st_ref, p_ref, *, tk):
    ki = pl.program_id(1)

    @pl.when(ki == 0)
    def _():
        m_ref[...] = jnp.full_like(m_ref, -jnp.inf)
        acc_ref[...] = jnp.zeros_like(acc_ref)

    ones = jnp.ones((FLASH_SUM_ROWS, tk), BF16)
    k = [k_ref[:, g * HEAD_DIM:(g + 1) * HEAD_DIM] for g in range(ATT_KV_HEADS)]
    vt1 = [jnp.concatenate([vt_ref[g * HEAD_DIM:(g + 1) * HEAD_DIM, :], ones], axis=0) for g in range(ATT_KV_HEADS)]
    blocks = [slice(r, r + FLASH_ROW_BLOCK) for r in range(0, tk, FLASH_ROW_BLOCK)]

    tq = q_ref.shape[0]
    units = [(h, slice(c, c + FLASH_Q_BLOCK)) for h in range(ATT_Q_HEADS) for c in range(0, tq, FLASH_Q_BLOCK)]

    def scores(h, qs):
        st_ref[h, :, qs] = _dot_nt(k[h // ATT_GROUP], q_ref[qs, h * HEAD_DIM:(h + 1) * HEAD_DIM])

    def softmax(h, qs):
        part = None
        for rows in blocks:
            x = st_ref[h, rows, qs]
            while x.shape[0] > SUBLANES:
                half = x.shape[0] // 2
                x = jnp.maximum(x[:half], x[half:])
            part = x if part is None else jnp.maximum(part, x)
        m_prev = m_ref[h, :, qs]
        m_new = jnp.maximum(m_prev, jnp.max(part, axis=0, keepdims=True))
        m_ref[h, :, qs] = m_new
        for rows in blocks:
            p_ref[h, rows, qs] = jnp.exp2(st_ref[h, rows, qs] - m_new).astype(BF16)
        return jnp.exp2(m_prev - m_new)

    def accumulate(h, qs, alpha):
        acc_ref[h, :, qs] = alpha * acc_ref[h, :, qs] + _dot(vt1[h // ATT_GROUP], p_ref[h, :, qs])

    scores(*units[0])
    pending = None
    for u, unit in enumerate(units):
        if u + 1 < len(units):
            scores(*units[u + 1])
        alpha = softmax(*unit)
        if pending is not None:
            accumulate(*pending)
        pending = (*unit, alpha)
    accumulate(*pending)

    @pl.when(ki == pl.num_programs(1) - 1)
    def _():
        for h in range(ATT_Q_HEADS):
            cols = slice(h * HEAD_DIM, (h + 1) * HEAD_DIM)
            o = (acc_ref[h, 0:HEAD_DIM, :] / acc_ref[h, HEAD_DIM:HEAD_DIM + 1, :]).T
            o_ref[:, cols] = _rms(o, on_ref[...]).astype(BF16)


def _flash_attn(q, k, vt, out_norm, *, tq=512, tk=1024):
    L = q.shape[0]
    assert L % tq == 0 and L % tk == 0 and tq % FLASH_Q_BLOCK == 0 and tk % FLASH_ROW_BLOCK == 0
    return pl.pallas_call(
        functools.partial(_flash_kernel, tk=tk),
        grid=(L // tq, L // tk),
        in_specs=[
            pl.BlockSpec((tq, ATT_Q_W), lambda i, j: (i, 0)),
            pl.BlockSpec((tk, ATT_KV_W), lambda i, j: (j, 0)),
            pl.BlockSpec((ATT_KV_W, tk), lambda i, j: (0, j)),
            pl.BlockSpec((1, HEAD_DIM), lambda i, j: (0, 0)),
        ],
        out_specs=pl.BlockSpec((tq, ATT_Q_W), lambda i, j: (i, 0)),
        out_shape=jax.ShapeDtypeStruct((L, ATT_Q_W), BF16),
        scratch_shapes=[
            pltpu.VMEM((ATT_Q_HEADS, 1, tq), F32),
            pltpu.VMEM((ATT_Q_HEADS, HEAD_DIM + FLASH_SUM_ROWS, tq), F32),
            pltpu.VMEM((ATT_Q_HEADS, tk, tq), F32),
            pltpu.VMEM((ATT_Q_HEADS, tk, tq), BF16),
        ],
        compiler_params=_cparams(("parallel", "arbitrary")),
        name="flash_attn",
    )(q, k, vt, out_norm)


def _out_proj_kernel(of_ref, ob_ref, z_ref, oa_ref, h_ref, gn_ref, w_ref, o_ref, mix_ref):
    att = _dot(oa_ref[...], w_ref[GDN_W:, :])
    for h in range(GDN_HEADS):
        cols = slice(h * HEAD_DIM, (h + 1) * HEAD_DIM)
        o = _rms(of_ref[:, cols] + ob_ref[:, cols], gn_ref[...])
        z = z_ref[:, cols]
        mix_ref[:, cols] = (o * (z * jax.nn.sigmoid(z))).astype(BF16)
    o_ref[...] = h_ref[...] + (att + _dot(mix_ref[...], w_ref[0:GDN_W, :]))


def _out_proj(o_fwd, o_bwd, z, oa, h1, gdn_norm, w_out, *, tm=512):
    L = h1.shape[0]
    assert L % tm == 0
    row = lambda i: (i, 0)
    return pl.pallas_call(
        _out_proj_kernel,
        grid=(L // tm,),
        in_specs=[
            pl.BlockSpec((tm, GDN_W), row),
            pl.BlockSpec((tm, GDN_W), row),
            pl.BlockSpec((tm, GDN_W), row),
            pl.BlockSpec((tm, ATT_Q_W), row),
            pl.BlockSpec((tm, D_MODEL), row),
            pl.BlockSpec((1, HEAD_DIM), lambda i: (0, 0)),
            pl.BlockSpec((GDN_W + ATT_Q_W, D_MODEL), lambda i: (0, 0)),
        ],
        out_specs=pl.BlockSpec((tm, D_MODEL), row),
        out_shape=jax.ShapeDtypeStruct((L, D_MODEL), F32),
        scratch_shapes=[pltpu.VMEM((tm, GDN_W), BF16)],
        compiler_params=_cparams(("parallel",)),
        name="out_proj",
    )(o_fwd, o_bwd, z, oa, h1, gdn_norm, w_out)


def _rope_tables(L):
    assert L % GRID_W == 0
    rows = L // GRID_W
    freqs = ROPE_THETA ** (-jnp.arange(0, AXIS_DIM, 2, dtype=F32) / AXIS_DIM)
    ang_r = jnp.arange(rows, dtype=F32)[:, None] * freqs[None, :]
    ang_c = jnp.arange(GRID_W, dtype=F32)[:, None] * freqs[None, :]
    cos_r, sin_r, cos_c, sin_c = jnp.cos(ang_r), jnp.sin(ang_r), jnp.cos(ang_c), jnp.sin(ang_c)
    zr, zc = jnp.zeros_like(cos_r), jnp.zeros_like(cos_c)
    row_part = jnp.stack([jnp.concatenate([cos_r, cos_r, zr, zr], axis=-1),
                          jnp.concatenate([-sin_r, zr, zr, zr], axis=-1),
                          jnp.concatenate([zr, sin_r, zr, zr], axis=-1)])
    col_part = jnp.stack([jnp.concatenate([zc, zc, cos_c, cos_c], axis=-1),
                          jnp.concatenate([zc, zc, -sin_c, zc], axis=-1),
                          jnp.concatenate([zc, zc, zc, sin_c], axis=-1)])
    return row_part, col_part


def _pad_lanes(x):
    x = x.reshape(1, -1)
    return jnp.pad(x, ((0, 0), (0, LANES - x.shape[1])))


def _encode(x, p, rope):
    h1 = _ffn(x, p["ffn1_norm"], p["ffn1_wg"], p["ffn1_wu"], p["ffn1_wd"], p["final_norm"], final=False)
    rope_row, rope_col = rope
    qg, kg, vg, z, qa, ka, va, gcb, gct = _mix_proj(h1, p["mix_norm"], p["w_in"], p["w_qk"], p["w_vt"], p["w_ab"],
                                                    p["q_norm"], p["k_norm"], rope_row, rope_col, p["alog"],
                                                    p["dtb"], p["conv_w"])
    o_fwd, o_bwd = _gdn_scan(qg, kg, vg, gcb, gct)
    oa = _flash_attn(qa, ka, va, p["attn_out_norm"])
    h2 = _out_proj(o_fwd, o_bwd, z, oa, h1, p["gdn_out_norm"], p["w_out"])
    return _ffn(h2, p["ffn2_norm"], p["ffn2_wg"], p["ffn2_wu"], p["ffn2_wd"], p["final_norm"], final=True)


def _prepare_params(ffn1_norm, ffn1_w_gate, ffn1_w_up, ffn1_w_down, mix_norm, w_in, conv_w, a_log_fwd, a_log_bwd,
                    dt_bias_fwd, dt_bias_bwd, gdn_out_norm, q_norm, k_norm, attn_out_norm, w_out, ffn2_norm,
                    ffn2_w_gate, ffn2_w_up, ffn2_w_down, final_norm):
    w_in0 = w_in[0]
    att0 = GDN_PROJ_W + GATE_COLS
    w_ab = jnp.pad(w_in0[:, GDN_PROJ_W:att0], ((0, 0), (0, LANES - GATE_COLS)))
    return dict(
        ffn1_norm=ffn1_norm[0].reshape(1, -1),
        ffn1_wg=ffn1_w_gate[0].astype(BF16), ffn1_wu=ffn1_w_up[0].astype(BF16), ffn1_wd=ffn1_w_down[0],
        mix_norm=mix_norm[0].reshape(1, -1),
        w_in=w_in0[:, :GDN_PROJ_W].astype(BF16),
        w_qk=w_in0[:, att0:att0 + ATT_Q_W + ATT_KV_W].astype(BF16),
        w_vt=w_in0[:, att0 + ATT_Q_W + ATT_KV_W:].T.astype(BF16),
        w_ab=w_ab.astype(BF16),
        conv_w=jnp.pad(conv_w[0], ((0, SUBLANES - CONV_K), (0, 0))),
        alog=_pad_lanes(jnp.concatenate([a_log_fwd[0], a_log_bwd[0]])),
        dtb=_pad_lanes(jnp.concatenate([dt_bias_fwd[0], dt_bias_bwd[0]])),
        gdn_out_norm=gdn_out_norm[0].reshape(1, -1),
        q_norm=q_norm[0].reshape(1, -1), k_norm=k_norm[0].reshape(1, -1),
        attn_out_norm=attn_out_norm[0].reshape(1, -1),
        w_out=w_out[0].astype(BF16),
        ffn2_norm=ffn2_norm[0].reshape(1, -1),
        ffn2_wg=ffn2_w_gate[0].astype(BF16), ffn2_wu=ffn2_w_up[0].astype(BF16), ffn2_wd=ffn2_w_down[0],
        final_norm=final_norm.reshape(1, -1),
    )


def kernel(x_prompt, x_sample, ffn1_norm, ffn1_w_gate, ffn1_w_up, ffn1_w_down, mix_norm, w_in, conv_w, a_log_fwd,
           a_log_bwd, dt_bias_fwd, dt_bias_bwd, gdn_out_norm, q_norm, k_norm, attn_out_norm, w_out, ffn2_norm,
           ffn2_w_gate, ffn2_w_up, ffn2_w_down, final_norm):
    assert x_prompt.shape[0] == 1 and x_sample.shape[0] == 1
    p = _prepare_params(ffn1_norm, ffn1_w_gate, ffn1_w_up, ffn1_w_down, mix_norm, w_in, conv_w, a_log_fwd,
                        a_log_bwd, dt_bias_fwd, dt_bias_bwd, gdn_out_norm, q_norm, k_norm, attn_out_norm, w_out,
                        ffn2_norm, ffn2_w_gate, ffn2_w_up, ffn2_w_down, final_norm)
    rope = _rope_tables(max(x_prompt.shape[1], x_sample.shape[1]))
    y_prompt = _encode(x_prompt[0], p, rope)
    y_sample = _encode(x_sample[0], p, rope)
    return (y_prompt[None], y_sample[None])
```

```python
import functools
import math

import jax
import jax.numpy as jnp
from jax import lax
from jax.experimental import pallas as pl
from jax.experimental.pallas import tpu as pltpu

D_MODEL = 2048
HEAD_DIM = 128
GDN_HEADS = 8
GDN_W = GDN_HEADS * HEAD_DIM
ATT_Q_HEADS = 8
ATT_KV_HEADS = 2
ATT_GROUP = ATT_Q_HEADS // ATT_KV_HEADS
ATT_Q_W = ATT_Q_HEADS * HEAD_DIM
ATT_KV_W = ATT_KV_HEADS * HEAD_DIM
GDN_PROJ_W = 4 * GDN_W
GATE_COLS = 4 * GDN_HEADS
D_FF = 5632
CONV_K = 5
GRID_W = 64
AXIS_DIM = HEAD_DIM // 2
ROPE_THETA = 10000.0
EPS = 1e-6

LANES = 128
SUBLANES = 8
SCAN_CHUNK = 128
PROJ_COL_TILE = 256
CONV_HALO = 16
INV_BASE_LEVELS = 2
INV_BASE = 2 ** INV_BASE_LEVELS
EXP2_SCALE = HEAD_DIM ** -0.5 * math.log2(math.e)
FLASH_ROW_BLOCK = 64
FLASH_SUM_ROWS = 16
FLASH_Q_BLOCK = 512
F32 = jnp.float32
BF16 = jnp.bfloat16

V7X_VMEM_BYTES = 64 * 1024 * 1024
_VMEM_LIMIT = V7X_VMEM_BYTES * 7 // 8


def _cparams(semantics):
    return pltpu.CompilerParams(dimension_semantics=semantics, vmem_limit_bytes=_VMEM_LIMIT)


def _rms(x, g):
    return x * lax.rsqrt(jnp.mean(x * x, axis=-1, keepdims=True) + EPS) * g


def _dot(a, b):
    return jnp.dot(a, b, preferred_element_type=F32)


def _dot_nt(a, b):
    return lax.dot_general(a, b, (((1,), (1,)), ((), ())), preferred_element_type=F32)


def _split3(x):
    hi = x.astype(BF16)
    r = x - hi.astype(F32)
    mid = r.astype(BF16)
    lo = (r - mid.astype(F32)).astype(BF16)
    return hi, mid, lo


def _ffn_kernel(x_ref, g_ref, wg_ref, wu_ref, wd_ref, fg_ref, o_ref, xn_ref, *, final):
    j = pl.program_id(1)
    last = pl.num_programs(1) - 1

    def hidden_tile():
        xn = xn_ref[...]
        gate = _dot(xn, wg_ref[...])
        up = _dot(xn, wu_ref[...])
        act = (gate * jax.nn.sigmoid(gate) * up).astype(BF16)
        return _dot(act, wd_ref[...])

    @pl.when(j == 0)
    def _():
        xn_ref[...] = _rms(x_ref[...], g_ref[...]).astype(BF16)
        o_ref[...] = hidden_tile()

    @pl.when(jnp.logical_and(j > 0, j < last))
    def _():
        o_ref[...] += hidden_tile()

    @pl.when(j == last)
    def _():
        h = x_ref[...] + 0.5 * (o_ref[...] + hidden_tile())
        if final:
            h = _rms(h, fg_ref[...])
        o_ref[...] = h


def _ffn(x, norm_g, wg, wu, wd, final_g, *, final, tm=1024, tf=256):
    L = x.shape[0]
    assert L % tm == 0 and D_FF % tf == 0
    grid = (L // tm, D_FF // tf)
    return pl.pallas_call(
        functools.partial(_ffn_kernel, final=final),
        grid=grid,
        in_specs=[
            pl.BlockSpec((tm, D_MODEL), lambda i, j: (i, 0)),
            pl.BlockSpec((1, D_MODEL), lambda i, j: (0, 0)),
            pl.BlockSpec((D_MODEL, tf), lambda i, j: (0, j)),
            pl.BlockSpec((D_MODEL, tf), lambda i, j: (0, j)),
            pl.BlockSpec((tf, D_MODEL), lambda i, j: (j, 0)),
            pl.BlockSpec((1, D_MODEL), lambda i, j: (0, 0)),
        ],
        out_specs=pl.BlockSpec((tm, D_MODEL), lambda i, j: (i, 0)),
        out_shape=jax.ShapeDtypeStruct((L, D_MODEL), F32),
        scratch_shapes=[pltpu.VMEM((tm, D_MODEL), BF16)],
        compiler_params=_cparams(("parallel", "arbitrary")),
        name="ffn",
    )(x, norm_g, wg, wu, wd, final_g)


def _rope(x, cos, sin_lo, sin_hi):
    return (x * cos + pltpu.roll(x, AXIS_DIM // 2, axis=1) * sin_hi
            + pltpu.roll(x, HEAD_DIM - AXIS_DIM // 2, axis=1) * sin_lo)


def _mix_proj_kernel(x_ref, xprev_ref, xnext_ref, g_ref, wg_ref, w_ref, wvt_ref, wab_ref, qn_ref, kn_ref,
                     rope_row_ref, rope_col_ref, alog_ref, dtb_ref, cw_ref, qg_ref, kg_ref, vg_ref, z_ref, q_ref,
                     k_ref, vt_ref, gcb_ref, gct_ref, ext_ref, *, tm):
    i = pl.program_id(0)
    pad = CONV_K // 2
    g = g_ref[...]
    xn = _rms(x_ref[...], g).astype(BF16)

    xn_ext = jnp.concatenate([_rms(xprev_ref[...], g).astype(BF16), xn, _rms(xnext_ref[...], g).astype(BF16)], axis=0)
    keep_prev = jnp.where(i == 0, 0.0, 1.0)
    keep_next = jnp.where(i == pl.num_programs(0) - 1, 0.0, 1.0)
    def rope_table(n):
        rows = rope_row_ref[n]
        col = rope_col_ref[n]
        return jnp.concatenate([jnp.broadcast_to(rows[g:g + 1, :], (GRID_W, HEAD_DIM)) + col
                                for g in range(tm // GRID_W)], axis=0)

    cos, slo, shi = rope_table(0), rope_table(1), rope_table(2)

    def project_ext(c):
        cs = slice(c, c + PROJ_COL_TILE)
        e = _dot(xn_ext, wg_ref[:, cs])
        ext_ref[0:CONV_HALO, cs] = e[0:CONV_HALO] * keep_prev
        ext_ref[CONV_HALO:CONV_HALO + tm, cs] = e[CONV_HALO:CONV_HALO + tm]
        ext_ref[CONV_HALO + tm:, cs] = e[CONV_HALO + tm:] * keep_next

    def project_z(c):
        z_ref[:, c:c + PROJ_COL_TILE] = _dot(xn, wg_ref[:, 3 * GDN_W + c:3 * GDN_W + c + PROJ_COL_TILE])

    def project_attn_pair(h):
        pair = _dot(xn, w_ref[:, h * HEAD_DIM:(h + 2) * HEAD_DIM])
        for half in range(2):
            head = pair[:, half * HEAD_DIM:(half + 1) * HEAD_DIM]
            n = h + half
            if n < ATT_Q_HEADS:
                q_ref[:, n * HEAD_DIM:(n + 1) * HEAD_DIM] = (
                    _rope(_rms(head, qn_ref[...]), cos, slo, shi) * EXP2_SCALE).astype(BF16)
            else:
                n -= ATT_Q_HEADS
                k_ref[:, n * HEAD_DIM:(n + 1) * HEAD_DIM] = _rope(_rms(head, kn_ref[...]), cos, slo, shi).astype(BF16)

    def project_vt():
        vt_ref[...] = _dot_nt(wvt_ref[...], xn).astype(BF16)

    def conv_slab(s):
        cols = slice(s * HEAD_DIM, (s + 1) * HEAD_DIM)
        y = None
        for t in range(CONV_K):
            term = ext_ref[CONV_HALO - pad + t:CONV_HALO - pad + t + tm, cols] * cw_ref[t:t + 1, cols]
            y = term if y is None else y + term
        y = y * jax.nn.sigmoid(y)
        part, h = divmod(s, GDN_HEADS)
        hc = slice(h * HEAD_DIM, (h + 1) * HEAD_DIM)
        if part == 0:
            qg_ref[:, hc] = y * (lax.rsqrt(jnp.sum(y * y, axis=-1, keepdims=True) + EPS) * (HEAD_DIM ** -0.5))
        elif part == 1:
            kg_ref[:, hc] = y * lax.rsqrt(jnp.sum(y * y, axis=-1, keepdims=True) + EPS)
        else:
            vg_ref[:, hc] = y

    tiles = range(0, GDN_W, PROJ_COL_TILE)
    mxu_q = [functools.partial(project_ext, c) for c in tiles]
    mxu_k = [functools.partial(project_ext, GDN_W + c) for c in tiles]
    mxu_v = [functools.partial(project_ext, 2 * GDN_W + c) for c in tiles]
    mxu_z = [functools.partial(project_z, c) for c in tiles]
    mxu_att = [functools.partial(project_attn_pair, h) for h in range(0, ATT_Q_HEADS + ATT_KV_HEADS, 2)]
    mxu_att.append(project_vt)
    for piece in mxu_q:
        piece()
    n_att = len(mxu_att) // 2
    stages = [(mxu_k + mxu_att[:n_att], range(0, GDN_HEADS)),
              (mxu_v + mxu_att[n_att:] + mxu_z[:2], range(GDN_HEADS, 2 * GDN_HEADS)),
              (mxu_z[2:], range(2 * GDN_HEADS, 3 * GDN_HEADS))]
    for pieces, slabs in stages:
        for n in range(max(len(pieces), len(slabs))):
            if n < len(pieces):
                pieces[n]()
            if n < len(slabs):
                conv_slab(slabs[n])

    ab = _dot(xn, wab_ref[...])
    t = ab + dtb_ref[...]
    softplus = jnp.maximum(t, 0.0) + jnp.log1p(jnp.exp(-jnp.abs(t)))
    log_decay = -jnp.exp(alog_ref[...]) * softplus
    lane = lax.broadcasted_iota(jnp.int32, ab.shape, 1)
    gb = jnp.where(lane < 2 * GDN_HEADS, log_decay, jax.nn.sigmoid(ab))

    r = lax.broadcasted_iota(jnp.int32, (tm, tm), 0)
    c = lax.broadcasted_iota(jnp.int32, (tm, tm), 1)
    same = (r // SCAN_CHUNK) == (c // SCAN_CHUNK)
    m_lo = jnp.where(same & (c <= r), 1.0, 0.0).astype(BF16)
    m_up = jnp.where(same & (c >= r), 1.0, 0.0).astype(BF16)
    parts = _split3(gb)
    gc_f = _dot(m_lo, parts[0]) + _dot(m_lo, parts[1]) + _dot(m_lo, parts[2])
    gc_b = _dot(m_up, parts[0]) + _dot(m_up, parts[1]) + _dot(m_up, parts[2])
    gcb = jnp.where(lane < GDN_HEADS, gc_f, jnp.where(lane < 2 * GDN_HEADS, gc_b, gb))
    gcb_ref[...] = gcb
    for n in range(tm // SCAN_CHUNK):
        gct = gcb[n * SCAN_CHUNK:(n + 1) * SCAN_CHUNK, :].T
        gct_ref[n * 2 * GDN_HEADS:(n + 1) * 2 * GDN_HEADS, :] = gct[0:2 * GDN_HEADS, :]


def _mix_proj(x, norm_g, w_in, w_qk, w_vt, w_ab, q_norm, k_norm, rope_row, rope_col, alog, dtb, conv_w8, *, tm=256):
    L = x.shape[0]
    assert L % tm == 0 and tm % SCAN_CHUNK == 0 and tm % CONV_HALO == 0 and tm % GRID_W == 0
    n_halo = L // CONV_HALO
    per = tm // CONV_HALO
    rows_per_tile = tm // GRID_W
    assert rows_per_tile <= SUBLANES
    rope_row = rope_row[:, :L // GRID_W].reshape(3, L // tm, rows_per_tile, HEAD_DIM)
    rope_row = jnp.pad(rope_row, ((0, 0), (0, 0), (0, SUBLANES - rows_per_tile), (0, 0)))
    row = lambda i: (i, 0)
    fixed = lambda i: (0, 0)
    resident = pl.Buffered(1)
    return pl.pallas_call(
        functools.partial(_mix_proj_kernel, tm=tm),
        grid=(L // tm,),
        in_specs=[
            pl.BlockSpec((tm, D_MODEL), row),
            pl.BlockSpec((CONV_HALO, D_MODEL), lambda i: (jnp.maximum(i * per - 1, 0), 0)),
            pl.BlockSpec((CONV_HALO, D_MODEL), lambda i: (jnp.minimum((i + 1) * per, n_halo - 1), 0)),
            pl.BlockSpec((1, D_MODEL), fixed),
            pl.BlockSpec((D_MODEL, GDN_PROJ_W), fixed, pipeline_mode=resident),
            pl.BlockSpec((D_MODEL, ATT_Q_W + ATT_KV_W), fixed, pipeline_mode=resident),
            pl.BlockSpec((ATT_KV_W, D_MODEL), fixed, pipeline_mode=resident),
            pl.BlockSpec((D_MODEL, LANES), fixed),
            pl.BlockSpec((1, HEAD_DIM), fixed),
            pl.BlockSpec((1, HEAD_DIM), fixed),
            pl.BlockSpec((3, None, SUBLANES, HEAD_DIM), lambda i: (0, i, 0, 0)),
            pl.BlockSpec((3, GRID_W, HEAD_DIM), lambda i: (0, 0, 0)),
            pl.BlockSpec((1, LANES), fixed),
            pl.BlockSpec((1, LANES), fixed),
            pl.BlockSpec((SUBLANES, 3 * GDN_W), fixed),
        ],
        out_specs=[
            pl.BlockSpec((tm, GDN_W), row),
            pl.BlockSpec((tm, GDN_W), row),
            pl.BlockSpec((tm, GDN_W), row),
            pl.BlockSpec((tm, GDN_W), row),
            pl.BlockSpec((tm, ATT_Q_W), row),
            pl.BlockSpec((tm, ATT_KV_W), row),
            pl.BlockSpec((ATT_KV_W, tm), lambda i: (0, i)),
            pl.BlockSpec((tm, LANES), row),
            pl.BlockSpec((tm // SCAN_CHUNK * 2 * GDN_HEADS, SCAN_CHUNK), row),
        ],
        out_shape=[
            jax.ShapeDtypeStruct((L, GDN_W), F32),
            jax.ShapeDtypeStruct((L, GDN_W), F32),
            jax.ShapeDtypeStruct((L, GDN_W), F32),
            jax.ShapeDtypeStruct((L, GDN_W), F32),
            jax.ShapeDtypeStruct((L, ATT_Q_W), BF16),
            jax.ShapeDtypeStruct((L, ATT_KV_W), BF16),
            jax.ShapeDtypeStruct((ATT_KV_W, L), BF16),
            jax.ShapeDtypeStruct((L, LANES), F32),
            jax.ShapeDtypeStruct((L // SCAN_CHUNK * 2 * GDN_HEADS, SCAN_CHUNK), F32),
        ],
        scratch_shapes=[pltpu.VMEM((tm + 2 * CONV_HALO, 3 * GDN_W), F32)],
        compiler_params=_cparams(("parallel",)),
        name="mix_proj",
    )(x, x, x, norm_g, w_in, w_qk, w_vt, w_ab, q_norm, k_norm, rope_row, rope_col, alog, dtb, conv_w8)


def _gdn_scan_kernel(qf_ref, kf_ref, vf_ref, gcbf_ref, gctf_ref, qb_ref, kb_ref, vb_ref, gcbb_ref, gctb_ref,
                     of_ref, ob_ref, s_ref):
    C = SCAN_CHUNK

    @pl.when(pl.program_id(0) == 0)
    def _():
        s_ref[...] = jnp.zeros_like(s_ref)

    r = lax.broadcasted_iota(jnp.int32, (C, C), 0)
    c = lax.broadcasted_iota(jnp.int32, (C, C), 1)
    eye = jnp.where(r == c, 1.0, 0.0)

    chains = []
    for fwd, (q_ref, k_ref, v_ref, gcb_ref, gct_ref, o_ref) in (
            (True, (qf_ref, kf_ref, vf_ref, gcbf_ref, gctf_ref, of_ref)),
            (False, (qb_ref, kb_ref, vb_ref, gcbb_ref, gctb_ref, ob_ref))):
        dif = r - c if fwd else c - r
        lane0 = 0 if fwd else GDN_HEADS
        gcb = gcb_ref[...]
        gc = gcb[:, lane0:lane0 + GDN_HEADS]
        beta = gcb[:, 2 * GDN_HEADS + lane0:3 * GDN_HEADS + lane0]
        g_end = gc[C - 1:C, :] if fwd else gc[0:1, :]
        shared = dict(q_ref=q_ref, k_ref=k_ref, v_ref=v_ref, o_ref=o_ref, m_incl=dif >= 0, m_strict=dif > 0,
                      gc=gc, beta=beta, e_gc=jnp.exp(gc), e_rest=jnp.exp(g_end - gc), e_end=jnp.exp(g_end),
                      gct=gct_ref[...])
        for h in range(GDN_HEADS):
            chains.append(dict(shared, cols=slice(h * HEAD_DIM, (h + 1) * HEAD_DIM), col1=slice(h, h + 1),
                               state=lane0 + h))
    heads = range(len(chains))
    cols = [ch["cols"] for ch in chains]
    col1 = [ch["col1"] for ch in chains]
    beta = [ch["beta"] for ch in chains]
    e_gc = [ch["e_gc"] for ch in chains]

    k = [chains[h]["k_ref"][:, cols[h]] for h in heads]
    k16 = [k[h].astype(BF16) for h in heads]
    kb = [k[h] * beta[h][:, col1[h]] for h in heads]
    q16 = [chains[h]["q_ref"][:, cols[h]].astype(BF16) for h in heads]
    kq = [_dot_nt(jnp.concatenate([kb[h].astype(BF16), q16[h]], axis=0), k16[h]) for h in heads]
    decay = []
    for ch in chains:
        diff = ch["gc"][:, ch["col1"]] - ch["gct"][ch["col1"], :]
        decay.append(jnp.where(ch["m_incl"], jnp.exp(jnp.where(ch["m_incl"], diff, 0.0)), 0.0))
    a = [jnp.where(chains[h]["m_strict"], kq[h][0:C] * decay[h], 0.0) for h in heads]
    attn_qk16 = [(kq[h][C:2 * C] * decay[h]).astype(BF16) for h in heads]

    rb = r >> INV_BASE_LEVELS
    cb = c >> INV_BASE_LEVELS
    a_d = [jnp.where(rb == cb, a[h], 0.0) for h in heads]
    inv = [eye - a_d[h] for h in heads]
    a_d16 = [a_d[h].astype(BF16) for h in heads]
    a_pow16 = [_dot(a_d16[h], a_d16[h]).astype(BF16) for h in heads]
    for level in range(INV_BASE_LEVELS - 1):
        if level < INV_BASE_LEVELS - 2:
            both = [_dot(jnp.concatenate([inv[h].astype(BF16), a_pow16[h]], axis=0), a_pow16[h]) for h in heads]
            inv = [inv[h] + both[h][0:C] for h in heads]
            a_pow16 = [both[h][C:2 * C].astype(BF16) for h in heads]
        else:
            inv = [inv[h] + _dot(inv[h].astype(BF16), a_pow16[h]) for h in heads]
    b = INV_BASE
    while b < C:
        off = ((rb >> 1) == (cb >> 1)) & (rb != cb)
        a_off16 = [jnp.where(off, a[h], 0.0).astype(BF16) for h in heads]
        inv16 = [inv[h].astype(BF16) for h in heads]
        left = [_dot(inv16[h], a_off16[h]).astype(BF16) for h in heads]
        inv = [inv[h] - _dot(left[h], inv16[h]) for h in heads]
        rb = rb >> 1
        cb = cb >> 1
        b *= 2

    rhs16 = [jnp.concatenate([(chains[h]["v_ref"][:, cols[h]] * beta[h][:, col1[h]]).astype(BF16),
                              (kb[h] * e_gc[h][:, col1[h]]).astype(BF16)], axis=1) for h in heads]
    uw = [_dot(inv[h].astype(BF16), rhs16[h]) for h in heads]
    q_dec16 = [(chains[h]["q_ref"][:, cols[h]] * e_gc[h][:, col1[h]]).astype(BF16) for h in heads]
    k_dec_t16 = [(k[h] * chains[h]["e_rest"][:, col1[h]]).T.astype(BF16) for h in heads]

    s = [s_ref[ch["state"]] for ch in chains]
    s16 = [s[h].astype(BF16) for h in heads]
    ws = [_dot(jnp.concatenate([uw[h][:, HEAD_DIM:].astype(BF16), q_dec16[h]], axis=0), s16[h]) for h in heads]
    v_new16 = [(uw[h][:, 0:HEAD_DIM] - ws[h][0:C]).astype(BF16) for h in heads]
    for h, ch in enumerate(chains):
        ch["o_ref"][:, cols[h]] = ws[h][C:2 * C] + _dot(attn_qk16[h], v_new16[h])
    for h, ch in enumerate(chains):
        s_ref[ch["state"]] = s[h] * ch["e_end"][:, col1[h]] + _dot(k_dec_t16[h], v_new16[h])


def _gdn_scan(q, k, v, gcb, gct):
    L = q.shape[0]
    assert L % SCAN_CHUNK == 0
    n = L // SCAN_CHUNK
    fwd = lambda i: (i, 0)
    bwd = lambda i: (n - 1 - i, 0)
    tile_f = pl.BlockSpec((SCAN_CHUNK, GDN_W), fwd)
    tile_b = pl.BlockSpec((SCAN_CHUNK, GDN_W), bwd)
    return pl.pallas_call(
        _gdn_scan_kernel,
        grid=(n,),
        in_specs=[
            tile_f, tile_f, tile_f,
            pl.BlockSpec((SCAN_CHUNK, LANES), fwd),
            pl.BlockSpec((GDN_HEADS, SCAN_CHUNK), lambda i: (2 * i, 0)),
            tile_b, tile_b, tile_b,
            pl.BlockSpec((SCAN_CHUNK, LANES), bwd),
            pl.BlockSpec((GDN_HEADS, SCAN_CHUNK), lambda i: (2 * (n - 1 - i) + 1, 0)),
        ],
        out_specs=[tile_f, tile_b],
        out_shape=[jax.ShapeDtypeStruct((L, GDN_W), F32), jax.ShapeDtypeStruct((L, GDN_W), F32)],
        scratch_shapes=[pltpu.VMEM((2 * GDN_HEADS, HEAD_DIM, HEAD_DIM), F32)],
        compiler_params=_cparams(("arbitrary",)),
        name="gdn_scan",
    )(q, k, v, gcb, gct, q, k, v, gcb, gct)


def _flash_kernel(q_ref, k_ref, vt_ref, on_ref, o_ref, m_ref, acc_ref, st_ref, p_ref, *, tk):
    ki = pl.program_id(1)

    @pl.when(ki == 0)
    def _():
        m_ref[...] = jnp.full_like(m_ref, -jnp.inf)
        acc_ref[...] = jnp.zeros_like(acc_ref)

    ones = jnp.ones((FLASH_SUM_ROWS, tk), BF16)
    k = [k_ref[:, g * HEAD_DIM:(g + 1) * HEAD_DIM] for g in range(ATT_KV_HEADS)]
    vt1 = [jnp.concatenate([vt_ref[g * HEAD_DIM:(g + 1) * HEAD_DIM, :], ones], axis=0) for g in range(ATT_KV_HEADS)]
    blocks = [slice(r, r + FLASH_ROW_BLOCK) for r in range(0, tk, FLASH_ROW_BLOCK)]

    tq = q_ref.shape[0]
    units = [(h, slice(c, c + FLASH_Q_BLOCK)) for h in range(ATT_Q_HEADS) for c in range(0, tq, FLASH_Q_BLOCK)]

    def scores(h, qs):
        st_ref[h, :, qs] = _dot_nt(k[h // ATT_GROUP], q_ref[qs, h * HEAD_DIM:(h + 1) * HEAD_DIM])

    def softmax(h, qs):
        part = None
        for rows in blocks:
            x = st_ref[h, rows, qs]
            while x.shape[0] > SUBLANES:
                half = x.shape[0] // 2
                x = jnp.maximum(x[:half], x[half:])
            part = x if part is None else jnp.maximum(part, x)
        m_prev = m_ref[h, :, qs]
        m_new = jnp.maximum(m_prev, jnp.max(part, axis=0, keepdims=True))
        m_ref[h, :, qs] = m_new
        for rows in blocks:
            p_ref[h, rows, qs] = jnp.exp2(st_ref[h, rows, qs] - m_new).astype(BF16)
        return jnp.exp2(m_prev - m_new)

    def accumulate(h, qs, alpha):
        acc_ref[h, :, qs] = alpha * acc_ref[h, :, qs] + _dot(vt1[h // ATT_GROUP], p_ref[h, :, qs])

    scores(*units[0])
    pending = None
    for u, unit in enumerate(units):
        if u + 1 < len(units):
            scores(*units[u + 1])
        alpha = softmax(*unit)
        if pending is not None:
            accumulate(*pending)
        pending = (*unit, alpha)
    accumulate(*pending)

    @pl.when(ki == pl.num_programs(1) - 1)
    def _():
        for h in range(ATT_Q_HEADS):
            cols = slice(h * HEAD_DIM, (h + 1) * HEAD_DIM)
            o = (acc_ref[h, 0:HEAD_DIM, :] / acc_ref[h, HEAD_DIM:HEAD_DIM + 1, :]).T
            o_ref[:, cols] = _rms(o, on_ref[...]).astype(BF16)


def _flash_attn(q, k, vt, out_norm, *, tq=512, tk=1024):
    L = q.shape[0]
    assert L % tq == 0 and L % tk == 0 and tq % FLASH_Q_BLOCK == 0 and tk % FLASH_ROW_BLOCK == 0
    return pl.pallas_call(
        functools.partial(_flash_kernel, tk=tk),
        grid=(L // tq, L // tk),
        in_specs=[
            pl.BlockSpec((tq, ATT_Q_W), lambda i, j: (i, 0)),
            pl.BlockSpec((tk, ATT_KV_W), lambda i, j: (j, 0)),
            pl.BlockSpec((ATT_KV_W, tk), lambda i, j: (0, j)),
            pl.BlockSpec((1, HEAD_DIM), lambda i, j: (0, 0)),
        ],
        out_specs=pl.BlockSpec((tq, ATT_Q_W), lambda i, j: (i, 0)),
        out_shape=jax.ShapeDtypeStruct((L, ATT_Q_W), BF16),
        scratch_shapes=[
            pltpu.VMEM((ATT_Q_HEADS, 1, tq), F32),
            pltpu.VMEM((ATT_Q_HEADS, HEAD_DIM + FLASH_SUM_ROWS, tq), F32),
            pltpu.VMEM((ATT_Q_HEADS, tk, tq), F32),
            pltpu.VMEM((ATT_Q_HEADS, tk, tq), BF16),
        ],
        compiler_params=_cparams(("parallel", "arbitrary")),
        name="flash_attn",
    )(q, k, vt, out_norm)


def _out_proj_kernel(of_ref, ob_ref, z_ref, oa_ref, h_ref, gn_ref, w_ref, o_ref, mix_ref):
    att = _dot(oa_ref[...], w_ref[GDN_W:, :])
    for h in range(GDN_HEADS):
        cols = slice(h * HEAD_DIM, (h + 1) * HEAD_DIM)
        o = _rms(of_ref[:, cols] + ob_ref[:, cols], gn_ref[...])
        z = z_ref[:, cols]
        mix_ref[:, cols] = (o * (z * jax.nn.sigmoid(z))).astype(BF16)
    o_ref[...] = h_ref[...] + (att + _dot(mix_ref[...], w_ref[0:GDN_W, :]))


def _out_proj(o_fwd, o_bwd, z, oa, h1, gdn_norm, w_out, *, tm=512):
    L = h1.shape[0]
    assert L % tm == 0
    row = lambda i: (i, 0)
    return pl.pallas_call(
        _out_proj_kernel,
        grid=(L // tm,),
        in_specs=[
            pl.BlockSpec((tm, GDN_W), row),
            pl.BlockSpec((tm, GDN_W), row),
            pl.BlockSpec((tm, GDN_W), row),
            pl.BlockSpec((tm, ATT_Q_W), row),
            pl.BlockSpec((tm, D_MODEL), row),
            pl.BlockSpec((1, HEAD_DIM), lambda i: (0, 0)),
            pl.BlockSpec((GDN_W + ATT_Q_W, D_MODEL), lambda i: (0, 0)),
        ],
        out_specs=pl.BlockSpec((tm, D_MODEL), row),
        out_shape=jax.ShapeDtypeStruct((L, D_MODEL), F32),
        scratch_shapes=[pltpu.VMEM((tm, GDN_W), BF16)],
        compiler_params=_cparams(("parallel",)),
        name="out_proj",
    )(o_fwd, o_bwd, z, oa, h1, gdn_norm, w_out)


def _rope_tables(L):
    assert L % GRID_W == 0
    rows = L // GRID_W
    freqs = ROPE_THETA ** (-jnp.arange(0, AXIS_DIM, 2, dtype=F32) / AXIS_DIM)
    ang_r = jnp.arange(rows, dtype=F32)[:, None] * freqs[None, :]
    ang_c = jnp.arange(GRID_W, dtype=F32)[:, None] * freqs[None, :]
    cos_r, sin_r, cos_c, sin_c = jnp.cos(ang_r), jnp.sin(ang_r), jnp.cos(ang_c), jnp.sin(ang_c)
    zr, zc = jnp.zeros_like(cos_r), jnp.zeros_like(cos_c)
    row_part = jnp.stack([jnp.concatenate([cos_r, cos_r, zr, zr], axis=-1),
                          jnp.concatenate([-sin_r, zr, zr, zr], axis=-1),
                          jnp.concatenate([zr, sin_r, zr, zr], axis=-1)])
    col_part = jnp.stack([jnp.concatenate([zc, zc, cos_c, cos_c], axis=-1),
                          jnp.concatenate([zc, zc, -sin_c, zc], axis=-1),
                          jnp.concatenate([zc, zc, zc, sin_c], axis=-1)])
    return row_part, col_part


def _pad_lanes(x):
    x = x.reshape(1, -1)
    return jnp.pad(x, ((0, 0), (0, LANES - x.shape[1])))


def _encode(x, p, rope):
    h1 = _ffn(x, p["ffn1_norm"], p["ffn1_wg"], p["ffn1_wu"], p["ffn1_wd"], p["final_norm"], final=False)
    rope_row, rope_col = rope
    qg, kg, vg, z, qa, ka, va, gcb, gct = _mix_proj(h1, p["mix_norm"], p["w_in"], p["w_qk"], p["w_vt"], p["w_ab"],
                                                    p["q_norm"], p["k_norm"], rope_row, rope_col, p["alog"],
                                                    p["dtb"], p["conv_w"])
    o_fwd, o_bwd = _gdn_scan(qg, kg, vg, gcb, gct)
    oa = _flash_attn(qa, ka, va, p["attn_out_norm"])
    h2 = _out_proj(o_fwd, o_bwd, z, oa, h1, p["gdn_out_norm"], p["w_out"])
    return _ffn(h2, p["ffn2_norm"], p["ffn2_wg"], p["ffn2_wu"], p["ffn2_wd"], p["final_norm"], final=True)


def _prepare_params(ffn1_norm, ffn1_w_gate, ffn1_w_up, ffn1_w_down, mix_norm, w_in, conv_w, a_log_fwd, a_log_bwd,
                    dt_bias_fwd, dt_bias_bwd, gdn_out_norm, q_norm, k_norm, attn_out_norm, w_out, ffn2_norm,
                    ffn2_w_gate, ffn2_w_up, ffn2_w_down, final_norm):
    w_in0 = w_in[0]
    att0 = GDN_PROJ_W + GATE_COLS
    w_ab = jnp.pad(w_in0[:, GDN_PROJ_W:att0], ((0, 0), (0, LANES - GATE_COLS)))
    return dict(
        ffn1_norm=ffn1_norm[0].reshape(1, -1),
        ffn1_wg=ffn1_w_gate[0].astype(BF16), ffn1_wu=ffn1_w_up[0].astype(BF16), ffn1_wd=ffn1_w_down[0].astype(BF16),
        mix_norm=mix_norm[0].reshape(1, -1),
        w_in=w_in0[:, :GDN_PROJ_W].astype(BF16),
        w_qk=w_in0[:, att0:att0 + ATT_Q_W + ATT_KV_W].astype(BF16),
        w_vt=w_in0[:, att0 + ATT_Q_W + ATT_KV_W:].T.astype(BF16),
        w_ab=w_ab.astype(BF16),
        conv_w=jnp.pad(conv_w[0], ((0, SUBLANES - CONV_K), (0, 0))),
        alog=_pad_lanes(jnp.concatenate([a_log_fwd[0], a_log_bwd[0]])),
        dtb=_pad_lanes(jnp.concatenate([dt_bias_fwd[0], dt_bias_bwd[0]])),
        gdn_out_norm=gdn_out_norm[0].reshape(1, -1),
        q_norm=q_norm[0].reshape(1, -1), k_norm=k_norm[0].reshape(1, -1),
        attn_out_norm=attn_out_norm[0].reshape(1, -1),
        w_out=w_out[0].astype(BF16),
        ffn2_norm=ffn2_norm[0].reshape(1, -1),
        ffn2_wg=ffn2_w_gate[0].astype(BF16), ffn2_wu=ffn2_w_up[0].astype(BF16), ffn2_wd=ffn2_w_down[0].astype(BF16),
        final_norm=final_norm.reshape(1, -1),
    )


def kernel(x_prompt, x_sample, ffn1_norm, ffn1_w_gate, ffn1_w_up, ffn1_w_down, mix_norm, w_in, conv_w, a_log_fwd,
           a_log_bwd, dt_bias_fwd, dt_bias_bwd, gdn_out_norm, q_norm, k_norm, attn_out_norm, w_out, ffn2_norm,
           ffn2_w_gate, ffn2_w_up, ffn2_w_down, final_norm):
    assert x_prompt.shape[0] == 1 and x_sample.shape[0] == 1
    p = _prepare_params(ffn1_norm, ffn1_w_gate, ffn1_w_up, ffn1_w_down, mix_norm, w_in, conv_w, a_log_fwd,
                        a_log_bwd, dt_bias_fwd, dt_bias_bwd, gdn_out_norm, q_norm, k_norm, attn_out_norm, w_out,
                        ffn2_norm, ffn2_w_gate, ffn2_w_up, ffn2_w_down, final_norm)
    rope = _rope_tables(max(x_prompt.shape[1], x_sample.shape[1]))
    y_prompt = _encode(x_prompt[0], p, rope)
    y_sample = _encode(x_sample[0], p, rope)
    return (y_prompt[None], y_sample[None])
```
